```python
import jax, jax.numpy as jnp
from jax import lax
import numpy as np

D_MODEL = 1024
BATCH = 16
SEQ = 4096
DEPTH = 4
DEC_BATCH = 8
DEC_SEQ = 16
PAST_LEN = 4096

CHUNK = 64
D_RNN = 1024
N_LRU_BLOCKS = 16
LRU_BLOCK = D_RNN // N_LRU_BLOCKS
CONV_W = 4
LRU_C = 8.0
SB_HEADS = 4
SB_HEAD_DIM = 128
SB_WIDTH = SB_HEADS * SB_HEAD_DIM
Q_BLOCK = 128
KEY_BLOCK = 128
ML_HEADS = 4
ML_QK_DIM = 64
ML_V_DIM = 128
ML_QK_WIDTH = ML_HEADS * ML_QK_DIM
ML_V_WIDTH = ML_HEADS * ML_V_DIM
N_BRANCH = 3
N_EXPERTS = 32
TOP_K = 4
D_FF = 512
SWIGLU_LIMIT = 7.0
SWIGLU_ALPHA = 1.702
MAX_GROUP_BLOCK = 1024
D_PLE = 256
DN_ALPHA = (2 * DEPTH) ** 0.25
DN_BETA = (8 * DEPTH) ** -0.25
LN_EPS = 1e-5
IN_WIDTHS = (D_RNN, D_RNN, SB_WIDTH, SB_WIDTH, SB_WIDTH, ML_QK_WIDTH, ML_QK_WIDTH,
             ML_V_WIDTH, ML_V_WIDTH, ML_HEADS, ML_HEADS, N_BRANCH * D_MODEL)
N_IN = sum(IN_WIDTHS)
F_GATE_OFF = sum(IN_WIDTHS[:10])
F32 = jnp.float32

kernel_name = 'hybrid_streaming_encoder_step'


def _layer_norm(x, g, b):
    xf = x.astype(F32)
    mu = jnp.mean(xf, axis=-1, keepdims=True)
    var = jnp.mean(jnp.square(xf - mu), axis=-1, keepdims=True)
    return ((xf - mu) * lax.rsqrt(var + LN_EPS) * g.astype(F32) + b.astype(F32)).astype(x.dtype)


def _causal_conv(u, buf, w, b):
    L = u.shape[1]
    ext = jnp.concatenate([buf.astype(u.dtype), u], axis=1)
    y = b + ext[:, 0:L] * w[0]
    for j in range(1, CONV_W):
        y = y + ext[:, j:j + L] * w[j]
    return y, ext[:, L:]


def _block_diag(u, w, b):
    ub = u.reshape(u.shape[:-1] + (N_LRU_BLOCKS, LRU_BLOCK))
    return jnp.einsum('blnc,ncd->blnd', ub, w).reshape(u.shape) + b


def _rg_lru(u, h0, wr, br, wi, bi, lam, first_frame):
    uf = u.astype(F32)
    r = jax.nn.sigmoid(_block_diag(uf, wr.astype(F32), br.astype(F32)))
    i = jax.nn.sigmoid(_block_diag(uf, wi.astype(F32), bi.astype(F32)))
    log_a = -LRU_C * r * jax.nn.softplus(-lam.astype(F32))
    a = jnp.exp(log_a)
    mult = jnp.sqrt(-jnp.expm1(2.0 * log_a))
    if first_frame:
        mult = mult.at[:, 0].set(1.0)
    bx = mult * (i * uf)
    bx = bx.at[:, 0].add(a[:, 0] * h0.astype(F32))

    def combine(e1, e2):
        a1, b1 = e1
        a2, b2 = e2
        return a1 * a2, a2 * b1 + b2

    _, h = lax.associative_scan(combine, (a, bx), axis=1)
    return h, h[:, -1]


def _stick_breaking(q, k, v, q_pos, k_pos):
    B, Tk, H, d = k.shape
    Tq = q.shape[1]
    nkb = Tk // KEY_BLOCK
    z = jnp.einsum('bqhd,bkhd->bhqk', q, k).astype(F32).reshape(B, H, Tq, nkb, KEY_BLOCK)
    earlier = (k_pos[None, :] < q_pos[:, None]).reshape(Tq, nkb, KEY_BLOCK)
    u = jnp.where(earlier, jnp.maximum(z, 0.0) + jnp.log1p(jnp.exp(-jnp.abs(z))), 0.0)
    strict = jnp.triu(jnp.ones((KEY_BLOCK, KEY_BLOCK), F32), 1)
    within = jnp.einsum('bhqnj,jl->bhqnl', u, strict)
    blk = jnp.sum(u, axis=-1)
    rest = jnp.sum(blk, axis=-1, keepdims=True) - (jnp.cumsum(blk, axis=-1) - blk)
    w = jnp.where(earlier, jnp.exp(z + within - rest[..., None]), 0.0)
    vb = v.reshape(B, nkb, KEY_BLOCK, H, d)
    return jnp.einsum('bhqnk,bnkhd->bqhd', w.astype(v.dtype), vb)


def _sb_prompt(q, k, v):
    S = q.shape[1]
    outs = []
    for qb in range(S // Q_BLOCK):
        lo, hi = qb * Q_BLOCK, (qb + 1) * Q_BLOCK
        outs.append(_stick_breaking(q[:, lo:hi], k[:, :hi], v[:, :hi], jnp.arange(lo, hi), jnp.arange(hi)))
    return jnp.concatenate(outs, axis=1)


def _mlstm_block(q, k, v, ig, lf, c, n, m):
    L = q.shape[2]
    b = jnp.cumsum(lf, axis=-1)
    causal = jnp.tril(jnp.ones((L, L), dtype=bool))
    dmat = jnp.where(causal, b[..., :, None] - b[..., None, :] + ig[..., None, :], -jnp.inf)
    inter = b + m[..., None]
    m_t = jnp.maximum(inter, jnp.max(dmat, axis=-1))
    s_inter = jnp.exp(inter - m_t)
    wqk = jnp.exp(dmat - m_t[..., None]) * jnp.einsum('bhtd,bhsd->bhts', q, k)
    num = s_inter[..., None] * jnp.einsum('bhtd,bhde->bhte', q, c) + jnp.einsum('bhts,bhse->bhte', wqk, v)
    den = s_inter * jnp.einsum('bhtd,bhd->bht', q, n) + jnp.sum(wqk, axis=-1)
    h = num / jnp.maximum(jnp.abs(den), jnp.exp(-m_t))[..., None]
    b_end = b[..., -1]
    g = b_end[..., None] - b + ig
    m_new = jnp.maximum(b_end + m, jnp.max(g, axis=-1))
    s_old = jnp.exp(b_end + m - m_new)
    wg = jnp.exp(g - m_new[..., None])
    c_new = s_old[..., None, None] * c + jnp.einsum('bhs,bhsd,bhse->bhde', wg, k, v)
    n_new = s_old[..., None] * n + jnp.einsum('bhs,bhsd->bhd', wg, k)
    return h, (c_new, n_new, m_new)


def _mlstm_prompt(q, k, v, ig, lf):
    B, H, S, _ = q.shape
    nc = S // CHUNK

    def chunks(t):
        return jnp.moveaxis(t.reshape(t.shape[:2] + (nc, CHUNK) + t.shape[3:]), 2, 0)

    init = (jnp.zeros((B, H, ML_QK_DIM, ML_V_DIM), F32), jnp.zeros((B, H, ML_QK_DIM), F32),
            jnp.zeros((B, H), F32))

    def step(carry, xs):
        h, new = _mlstm_block(*xs, *carry)
        return new, h

    state, hs = lax.scan(step, init, (chunks(q), chunks(k), chunks(v), chunks(ig), chunks(lf)))
    return jnp.moveaxis(hs, 0, 2).reshape(B, H, S, ML_V_DIM), state


def _moe(x, rw, rb, w1, b1, w2, b2):
    B, L, D = x.shape
    T = B * L
    t = x.reshape(T, D)
    logits = (t @ rw + rb).astype(F32)
    top_v, top_i = lax.top_k(logits, TOP_K)
    gate = jax.nn.softmax(top_v, axis=-1)
    TK = T * TOP_K
    M = max(8, min(MAX_GROUP_BLOCK, TK // N_EXPERTS))
    NB = -(-TK // M) + N_EXPERTS
    R = NB * M
    e_flat = top_i.reshape(TK).astype(jnp.int32)
    order = jnp.argsort(e_flat).astype(jnp.int32)
    e_sorted = e_flat[order]
    tok_sorted = order // TOP_K
    g_sorted = gate.reshape(TK)[order].astype(t.dtype)
    counts = jnp.zeros((N_EXPERTS,), jnp.int32).at[e_flat].add(1)
    padded = (counts + (M - 1)) // M * M
    pend = jnp.cumsum(padded)
    shift = (pend - padded) - (jnp.cumsum(counts) - counts)
    dest = jnp.arange(TK, dtype=jnp.int32) + shift[e_sorted]
    xs = jnp.zeros((R, D), t.dtype).at[dest].set(t[tok_sorted])
    gs = jnp.zeros((R,), t.dtype).at[dest].set(g_sorted)
    ts = jnp.zeros((R,), jnp.int32).at[dest].set(tok_sorted)
    block_e = jnp.minimum(jnp.searchsorted(pend, jnp.arange(NB, dtype=jnp.int32) * M, side='right'),
                          N_EXPERTS - 1)
    gu = jnp.einsum('nmd,ndf->nmf', xs.reshape(NB, M, D), w1[block_e]) + b1[block_e][:, None]
    g_ = jnp.minimum(gu[..., :D_FF], SWIGLU_LIMIT)
    up = jnp.clip(gu[..., D_FF:], -SWIGLU_LIMIT, SWIGLU_LIMIT)
    act = (up + 1.0) * g_ * jax.nn.sigmoid(SWIGLU_ALPHA * g_)
    y = jnp.einsum('nmf,nfd->nmd', act, w2[block_e]) + b2[block_e][:, None]
    y = y.reshape(R, D) * gs[:, None]
    out = jax.ops.segment_sum(y, ts, num_segments=T)
    return out.reshape(B, L, D)


def _layer(x, p, lw, conv_buf, lru_h, ml_c, ml_n, ml_m, k_past, v_past, prompt):
    (w_in, b_in, conv_w, conv_b, lru_wr, lru_br, lru_wi, lru_bi, lru_lambda, ml_norm_g,
     w_pa, w_pb, w_pc, w_out, ln1_g, ln1_b, router_w, router_b,
     exp_w1, exp_b1, exp_w2, exp_b2, ple_w, ple_gate_w, ple_gate_b, ln2_g, ln2_b) = lw
    B, L, _ = x.shape
    z = x @ w_in + b_in
    offs = [int(o) for o in np.cumsum(IN_WIDTHS)[:-1]]
    (a_x, a_g, sb_q, sb_k, sb_v, ml_q, ml_k, ml_v, ml_o, ml_i, ml_f, mg) = jnp.split(z, offs, axis=-1)

    ua, conv_new = _causal_conv(a_x, conv_buf, conv_w, conv_b)
    ha, lru_new = _rg_lru(ua, lru_h, lru_wr, lru_br, lru_wi, lru_bi, lru_lambda, prompt)
    ya = ha.astype(x.dtype) * jax.nn.gelu(a_g)

    q = sb_q.reshape(B, L, SB_HEADS, SB_HEAD_DIM) * (SB_HEAD_DIM ** -0.5)
    k = sb_k.reshape(B, L, SB_HEADS, SB_HEAD_DIM)
    v = sb_v.reshape(B, L, SB_HEADS, SB_HEAD_DIM)
    if prompt:
        yb = _sb_prompt(q, k, v)
    else:
        P = k_past.shape[1]
        pad = (-(P + L)) % KEY_BLOCK
        zpad = jnp.zeros((B, pad, SB_HEADS, SB_HEAD_DIM), k.dtype)
        k_all = jnp.concatenate([k_past.astype(k.dtype), k, zpad], axis=1)
        v_all = jnp.concatenate([v_past.astype(v.dtype), v, zpad], axis=1)
        yb = _stick_breaking(q, k_all, v_all, P + jnp.arange(L), jnp.arange(P + L + pad))
    yb = yb.reshape(B, L, SB_WIDTH)

    def heads(t, d):
        return t.reshape(B, L, ML_HEADS, d).transpose(0, 2, 1, 3).astype(F32)

    mq = heads(ml_q, ML_QK_DIM)
    mk = heads(ml_k, ML_QK_DIM) * (ML_QK_DIM ** -0.5)
    mv = heads(ml_v, ML_V_DIM)
    ig = ml_i.astype(F32).transpose(0, 2, 1)
    lf = jax.nn.log_sigmoid(ml_f.astype(F32)).transpose(0, 2, 1)
    if prompt:
        hc, (c_new, n_new, m_new) = _mlstm_prompt(mq, mk, mv, ig, lf)
    else:
        hc, (c_new, n_new, m_new) = _mlstm_block(mq, mk, mv, ig, lf, ml_c.astype(F32),
                                                 ml_n.astype(F32), ml_m.astype(F32))
    mu = jnp.mean(hc, axis=-1, keepdims=True)
    var = jnp.mean(jnp.square(hc - mu), axis=-1, keepdims=True)
    hc = ((hc - mu) * lax.rsqrt(var + LN_EPS)).transpose(0, 2, 1, 3).reshape(B, L, ML_V_WIDTH)
    yc = (hc * ml_norm_g.astype(F32)).astype(x.dtype) * jax.nn.sigmoid(ml_o)

    g = jax.nn.sigmoid(mg).reshape(B, L, N_BRANCH, D_MODEL)
    mixed = g[:, :, 0] * (ya @ w_pa) + g[:, :, 1] * (yb @ w_pb) + g[:, :, 2] * (yc @ w_pc)
    x = _layer_norm(DN_ALPHA * x + mixed @ w_out, ln1_g, ln1_b)

    f = _moe(x, router_w, router_b, exp_w1, exp_b1, exp_w2, exp_b2)
    ple = jax.nn.sigmoid(x @ ple_gate_w + ple_gate_b) * (p @ ple_w)
    x = _layer_norm(DN_ALPHA * x + f + ple, ln2_g, ln2_b)
    return x, (k, v, conv_new, lru_new, c_new, n_new, m_new)


def setup_inputs(seed: int = 0) -> dict:
    key = jax.random.key(seed)
    ks = iter(jax.random.split(key, 64))

    def nrm(shape, scale=1.0):
        return jax.random.normal(next(ks), shape, F32) * scale

    L = DEPTH
    x_prompt = nrm((BATCH, SEQ, D_MODEL))
    x_sample = nrm((DEC_BATCH, DEC_SEQ, D_MODEL))
    cache_sb_k = nrm((L, DEC_BATCH, PAST_LEN, SB_HEADS, SB_HEAD_DIM))
    cache_sb_v = nrm((L, DEC_BATCH, PAST_LEN, SB_HEADS, SB_HEAD_DIM))
    state_conv = nrm((L, DEC_BATCH, CONV_W - 1, D_RNN))
    state_lru = nrm((L, DEC_BATCH, D_RNN), 0.5)
    state_mlstm_c = nrm((L, DEC_BATCH, ML_HEADS, ML_QK_DIM, ML_V_DIM), 0.5)
    state_mlstm_n = nrm((L, DEC_BATCH, ML_HEADS, ML_QK_DIM), 0.5)
    state_mlstm_m = nrm((L, DEC_BATCH, ML_HEADS))
    p_prompt = nrm((L, BATCH, SEQ, D_PLE))
    p_sample = nrm((L, DEC_BATCH, DEC_SEQ, D_PLE))
    w_in = nrm((L, D_MODEL, N_IN), D_MODEL ** -0.5)
    b_in = nrm((L, N_IN), 0.01)
    b_in = b_in.at[:, F_GATE_OFF:F_GATE_OFF + ML_HEADS].add(jnp.linspace(3.0, 6.0, ML_HEADS))
    conv_w = nrm((L, CONV_W, D_RNN), CONV_W ** -0.5)
    conv_b = nrm((L, D_RNN), 0.01)
    lru_wr = nrm((L, N_LRU_BLOCKS, LRU_BLOCK, LRU_BLOCK), LRU_BLOCK ** -0.5)
    lru_br = nrm((L, D_RNN), 0.01)
    lru_wi = nrm((L, N_LRU_BLOCKS, LRU_BLOCK, LRU_BLOCK), LRU_BLOCK ** -0.5)
    lru_bi = nrm((L, D_RNN), 0.01)
    a_c = jax.random.uniform(next(ks), (L, D_RNN), F32, 0.9, 0.999)
    s = a_c ** (1.0 / LRU_C)
    lru_lambda = jnp.log(s) - jnp.log1p(-s)
    ml_norm_g = 1.0 + nrm((L, ML_V_WIDTH), 0.02)
    w_pa = nrm((L, D_RNN, D_MODEL), DN_BETA * D_RNN ** -0.5)
    w_pb = nrm((L, SB_WIDTH, D_MODEL), DN_BETA * SB_WIDTH ** -0.5)
    w_pc = nrm((L, ML_V_WIDTH, D_MODEL), DN_BETA * ML_V_WIDTH ** -0.5)
    w_out = nrm((L, D_MODEL, D_MODEL), DN_BETA * D_MODEL ** -0.5)
    ln1_g = 1.0 + nrm((L, D_MODEL), 0.02)
    ln1_b = nrm((L, D_MODEL), 0.01)
    router_w = nrm((L, D_MODEL, N_EXPERTS), D_MODEL ** -0.5)
    router_b = nrm((L, N_EXPERTS), 0.01)
    exp_w1 = nrm((L, N_EXPERTS, D_MODEL, 2 * D_FF), D_MODEL ** -0.5)
    exp_b1 = nrm((L, N_EXPERTS, 2 * D_FF), 0.01)
    exp_w2 = nrm((L, N_EXPERTS, D_FF, D_MODEL), DN_BETA * D_FF ** -0.5)
    exp_b2 = nrm((L, N_EXPERTS, D_MODEL), 0.01)
    ple_w = nrm((L, D_PLE, D_MODEL), DN_BETA * D_PLE ** -0.5)
    ple_gate_w = nrm((L, D_MODEL, D_MODEL), D_MODEL ** -0.5)
    ple_gate_b = nrm((L, D_MODEL), 0.01)
    ln2_g = 1.0 + nrm((L, D_MODEL), 0.02)
    ln2_b = nrm((L, D_MODEL), 0.01)
    return {'x_prompt': x_prompt, 'x_sample': x_sample, 'cache_sb_k': cache_sb_k, 'cache_sb_v': cache_sb_v,
            'state_conv': state_conv, 'state_lru': state_lru, 'state_mlstm_c': state_mlstm_c,
            'state_mlstm_n': state_mlstm_n, 'state_mlstm_m': state_mlstm_m,
            'p_prompt': p_prompt, 'p_sample': p_sample,
            'w_in': w_in, 'b_in': b_in, 'conv_w': conv_w, 'conv_b': conv_b,
            'lru_wr': lru_wr, 'lru_br': lru_br, 'lru_wi': lru_wi, 'lru_bi': lru_bi, 'lru_lambda': lru_lambda,
            'ml_norm_g': ml_norm_g, 'w_pa': w_pa, 'w_pb': w_pb, 'w_pc': w_pc, 'w_out': w_out,
            'ln1_g': ln1_g, 'ln1_b': ln1_b, 'router_w': router_w, 'router_b': router_b,
            'exp_w1': exp_w1, 'exp_b1': exp_b1, 'exp_w2': exp_w2, 'exp_b2': exp_b2,
            'ple_w': ple_w, 'ple_gate_w': ple_gate_w, 'ple_gate_b': ple_gate_b,
            'ln2_g': ln2_g, 'ln2_b': ln2_b}


def reference(x_prompt, x_sample, cache_sb_k, cache_sb_v, state_conv, state_lru,
              state_mlstm_c, state_mlstm_n, state_mlstm_m, p_prompt, p_sample,
              w_in, b_in, conv_w, conv_b, lru_wr, lru_br, lru_wi, lru_bi, lru_lambda,
              ml_norm_g, w_pa, w_pb, w_pc, w_out, ln1_g, ln1_b, router_w, router_b,
              exp_w1, exp_b1, exp_w2, exp_b2, ple_w, ple_gate_w, ple_gate_b, ln2_g, ln2_b):
    B = x_prompt.shape[0]
    y_prompt = x_prompt
    y_sample = x_sample
    st_p = []
    st_s = []
    for i in range(DEPTH):
        lw = (w_in[i], b_in[i], conv_w[i], conv_b[i], lru_wr[i], lru_br[i], lru_wi[i], lru_bi[i],
              lru_lambda[i], ml_norm_g[i], w_pa[i], w_pb[i], w_pc[i], w_out[i], ln1_g[i], ln1_b[i],
              router_w[i], router_b[i], exp_w1[i], exp_b1[i], exp_w2[i], exp_b2[i],
              ple_w[i], ple_gate_w[i], ple_gate_b[i], ln2_g[i], ln2_b[i])
        y_prompt, sp = _layer(y_prompt, p_prompt[i], lw,
                              jnp.zeros((B, CONV_W - 1, D_RNN), x_prompt.dtype),
                              jnp.zeros((B, D_RNN), F32),
                              jnp.zeros((B, ML_HEADS, ML_QK_DIM, ML_V_DIM), F32),
                              jnp.zeros((B, ML_HEADS, ML_QK_DIM), F32),
                              jnp.zeros((B, ML_HEADS), F32), None, None, True)
        y_sample, ss = _layer(y_sample, p_sample[i], lw, state_conv[i], state_lru[i],
                              state_mlstm_c[i], state_mlstm_n[i], state_mlstm_m[i],
                              cache_sb_k[i], cache_sb_v[i], False)
        st_p.append(sp)
        st_s.append(ss)
    k_p, v_p, conv_p, lru_p, c_p, n_p, m_p = [jnp.stack(s) for s in zip(*st_p)]
    k_s, v_s, conv_s, lru_s, c_s, n_s, m_s = [jnp.stack(s) for s in zip(*st_s)]
    return (y_prompt, y_sample, k_p, v_p, conv_p, lru_p, c_p, n_p, m_p,
            k_s, v_s, conv_s, lru_s, c_s, n_s, m_s)
```

```python
import functools

import jax
import jax.numpy as jnp
from jax import lax
from jax.experimental import pallas as pl
from jax.experimental.pallas import tpu as pltpu

F32 = jnp.float32
BF16 = jnp.bfloat16

D_MODEL = 1024
D_RNN = 1024
N_LRU_BLOCKS = 16
LRU_BLOCK = D_RNN // N_LRU_BLOCKS
LRU_GROUP = 256
CONV_W = 4
LRU_C = 8.0
SB_HEADS = 4
SB_HEAD_DIM = 128
SB_WIDTH = SB_HEADS * SB_HEAD_DIM
ML_HEADS = 4
ML_QK_DIM = 64
ML_V_DIM = 128
ML_QK_WIDTH = ML_HEADS * ML_QK_DIM
ML_V_WIDTH = ML_HEADS * ML_V_DIM
N_BRANCH = 3
N_EXPERTS = 32
TOP_K = 4
D_FF = 512
SWIGLU_LIMIT = 7.0
SWIGLU_ALPHA = 1.702
D_PLE = 256
LN_EPS = 1e-5
LANES = 128
KEY_BLOCK = 128

OFF_AX, OFF_AG = 0, 1024
OFF_Q, OFF_K, OFF_V = 2048, 2560, 3072
OFF_MQ, OFF_MK, OFF_MV, OFF_MO = 3584, 3840, 4096, 4608
OFF_MG = 5120
OFF_IF = 8192
N_IN_PAD = 8320
ORIG_IF = 5120
SB_DEAD_MASS = 110.0
VMEM_LIMIT = 56 * 1024 * 1024


def _cp(sem, vmem=VMEM_LIMIT):
    return pltpu.CompilerParams(dimension_semantics=sem, vmem_limit_bytes=vmem)


def _dot(a, b):
    return jnp.dot(a, b, preferred_element_type=F32)


def _dot_nt(a, b):
    return lax.dot_general(a, b, (((1,), (1,)), ((), ())), preferred_element_type=F32)


def _dot_tn(a, b):
    return lax.dot_general(a, b, (((0,), (0,)), ((), ())), preferred_element_type=F32)


def _sigmoid(x):
    return 1.0 / (1.0 + jnp.exp(-x))


def _softplus(x):
    return jnp.maximum(x, 0.0) + jnp.log1p(jnp.exp(-jnp.abs(x)))


def _layer_norm(r, g, b):
    mu = jnp.mean(r, axis=-1, keepdims=True)
    d = r - mu
    var = jnp.mean(d * d, axis=-1, keepdims=True)
    return d * lax.rsqrt(var + LN_EPS) * g + b


def _inproj_kernel(x_ref, w_ref, b_ref, o_ref, xb_ref):
    @pl.when(pl.program_id(1) == 0)
    def _():
        xb_ref[...] = x_ref[...].astype(BF16)

    o_ref[...] = _dot(xb_ref[...], w_ref[...]) + b_ref[...]


def _inproj(x, w, b, tm, tn):
    T, D = x.shape
    N = w.shape[1]
    return pl.pallas_call(
        _inproj_kernel,
        grid=(T // tm, N // tn),
        in_specs=[pl.BlockSpec((tm, D), lambda i, j: (i, 0)),
                  pl.BlockSpec((D, tn), lambda i, j: (0, j)),
                  pl.BlockSpec((1, tn), lambda i, j: (0, j))],
        out_specs=pl.BlockSpec((tm, tn), lambda i, j: (i, j)),
        out_shape=jax.ShapeDtypeStruct((T, N), F32),
        scratch_shapes=[pltpu.VMEM((tm, D), BF16)],
        compiler_params=_cp(("parallel", "arbitrary")),
        name="inproj",
    )(x, w, b)


def _lru_kernel(ax_ref, ag_ref, cbuf_ref, h0_ref, cw_ref, cb_ref, wr_ref, br_ref, wi_ref, bi_ref,
                lam_ref, ya_ref, hl_ref, ext_ref, hc_ref, *, first_frame):
    tl = ax_ref.shape[1]
    li = pl.program_id(1)

    @pl.when(li == 0)
    def _():
        ext_ref[0:8, :] = cbuf_ref[0]
        hc_ref[...] = h0_ref[0]

    ext_ref[8:8 + tl, :] = ax_ref[0]
    ua = cb_ref[...] + ext_ref[5:5 + tl, :] * cw_ref[0:1, :]
    for j in range(1, CONV_W):
        ua = ua + ext_ref[5 + j:5 + j + tl, :] * cw_ref[j:j + 1, :]
    ext_ref[5:8, :] = ext_ref[5 + tl:8 + tl, :]

    uab = ua.astype(BF16)
    ng = D_RNN // LRU_GROUP
    r_pre = jnp.concatenate(
        [_dot(uab[:, g * LRU_GROUP:(g + 1) * LRU_GROUP], wr_ref[g]) for g in range(ng)], axis=1)
    i_pre = jnp.concatenate(
        [_dot(uab[:, g * LRU_GROUP:(g + 1) * LRU_GROUP], wi_ref[g]) for g in range(ng)], axis=1)
    r = _sigmoid(r_pre + br_ref[...])
    ig = _sigmoid(i_pre + bi_ref[...])
    log_a = (-LRU_C) * r * _softplus(-lam_ref[...])
    a = jnp.exp(log_a)
    a2 = jnp.exp(2.0 * log_a)
    near = jnp.where(a2 == 1.0, -2.0 * log_a, (1.0 - a2) * (2.0 * log_a) / jnp.log(a2))
    mult = jnp.sqrt(jnp.where(a2 < 0.5, 1.0 - a2, near))
    row = lax.broadcasted_iota(jnp.int32, (tl, D_RNN), 0)
    if first_frame:
        mult = jnp.where((row == 0) & (li == 0), 1.0, mult)
    bx = mult * (ig * ua)

    d = 1
    while d < tl:
        a_sh = jnp.where(row < d, 1.0, pltpu.roll(a, d, 0))
        b_sh = jnp.where(row < d, 0.0, pltpu.roll(bx, d, 0))
        bx = a * b_sh + bx
        a = a * a_sh
        d *= 2
    h = bx + a * hc_ref[...]
    hc_ref[...] = h[tl - 1:tl, :]
    hl_ref[0] = h[tl - 1:tl, :]

    gx = ag_ref[0]
    gelu = 0.5 * gx * (1.0 + jnp.tanh(0.7978845608028654 * (gx + 0.044715 * (gx * gx * gx))))
    ya_ref[0] = (h * gelu).astype(ya_ref.dtype)


def _lru(z3, cbuf8, h0, cw8, cb, wr, br, wi, bi, lam, tl, first_frame):
    B, S, _ = z3.shape
    full = lambda shape: pl.BlockSpec(shape, lambda b, l: (0,) * len(shape))
    return pl.pallas_call(
        functools.partial(_lru_kernel, first_frame=first_frame),
        grid=(B, S // tl),
        in_specs=[pl.BlockSpec((1, tl, D_RNN), lambda b, l: (b, l, OFF_AX // D_RNN)),
                  pl.BlockSpec((1, tl, D_RNN), lambda b, l: (b, l, OFF_AG // D_RNN)),
                  pl.BlockSpec((1, 8, D_RNN), lambda b, l: (b, 0, 0)),
                  pl.BlockSpec((1, 1, D_RNN), lambda b, l: (b, 0, 0)),
                  full((8, D_RNN)), full((1, D_RNN)),
                  full((D_RNN // LRU_GROUP, LRU_GROUP, LRU_GROUP)), full((1, D_RNN)),
                  full((D_RNN // LRU_GROUP, LRU_GROUP, LRU_GROUP)), full((1, D_RNN)),
                  full((1, D_RNN))],
        out_specs=[pl.BlockSpec((1, tl, D_RNN), lambda b, l: (b, l, 0)),
                   pl.BlockSpec((1, 1, D_RNN), lambda b, l: (b, 0, 0))],
        out_shape=[jax.ShapeDtypeStruct((B, S, D_RNN), BF16),
                   jax.ShapeDtypeStruct((B, 1, D_RNN), F32)],
        scratch_shapes=[pltpu.VMEM((tl + 8, D_RNN), F32), pltpu.VMEM((1, D_RNN), F32)],
        compiler_params=_cp(("parallel", "arbitrary")),
        name="conv_rglru",
    )(z3, z3, cbuf8, h0, cw8, cb, wr, br, wi, bi, lam)


def _sb_kernel(q_ref, k_ref, v_ref, o_ref, acc_ref, c_ref, *, tq, tk, q_off):
    qi = pl.program_id(2)
    q = (q_ref[0] * (SB_HEAD_DIM ** -0.5)).astype(BF16)
    qpos = q_off + qi * tq + lax.broadcasted_iota(jnp.int32, (tq, tk), 0)
    lane = lax.broadcasted_iota(jnp.int32, (tq, tk), 1)
    jj = lax.broadcasted_iota(jnp.int32, (tk, tk), 0)
    ll = lax.broadcasted_iota(jnp.int32, (tk, tk), 1)
    suffix = jnp.where(jj >= ll, 1.0, 0.0).astype(BF16)
    acc_ref[...] = jnp.zeros_like(acc_ref)
    c_ref[...] = jnp.zeros_like(c_ref)
    kb_hi = (q_off + (qi + 1) * tq - 2) // tk

    def cond(carry):
        kb, cmin = carry
        return (kb >= 0) & (cmin < SB_DEAD_MASS)

    def body(carry):
        kb, _ = carry
        start = pl.multiple_of(kb * tk, tk)
        k = k_ref[0, pl.ds(start, tk), :].astype(BF16)
        v = v_ref[0, pl.ds(start, tk), :].astype(BF16)
        z = _dot_nt(q, k)
        earlier = (start + lane) < qpos
        u = jnp.where(earlier, _softplus(z), 0.0)
        u_hi = u.astype(BF16)
        u_lo = (u - u_hi.astype(F32)).astype(BF16)
        suf = _dot(u_hi, suffix) + _dot(u_lo, suffix)
        c = c_ref[...]
        w = jnp.where(earlier, jnp.exp(z - suf - c), 0.0)
        acc_ref[...] += _dot(w.astype(BF16), v)
        c_new = c + suf[:, 0:1]
        c_ref[...] = c_new
        return kb - 1, jnp.min(c_new)

    lax.while_loop(cond, body, (kb_hi, jnp.float32(0.0)))
    o_ref[0] = acc_ref[...].astype(o_ref.dtype)


def _sb_attention(q_arr, k_arr, v_arr, q_cb, k_cb, v_cb, tq, q_off):
    B, Tq, _ = q_arr.shape
    Tk = k_arr.shape[1]
    return pl.pallas_call(
        functools.partial(_sb_kernel, tq=tq, tk=KEY_BLOCK, q_off=q_off),
        grid=(B, SB_HEADS, Tq // tq),
        in_specs=[pl.BlockSpec((1, tq, SB_HEAD_DIM), lambda b, h, i: (b, i, q_cb + h)),
                  pl.BlockSpec((1, Tk, SB_HEAD_DIM), lambda b, h, i: (b, 0, k_cb + h)),
                  pl.BlockSpec((1, Tk, SB_HEAD_DIM), lambda b, h, i: (b, 0, v_cb + h))],
        out_specs=pl.BlockSpec((1, tq, SB_HEAD_DIM), lambda b, h, i: (b, i, h)),
        out_shape=jax.ShapeDtypeStruct((B, Tq, SB_WIDTH), BF16),
        scratch_shapes=[pltpu.VMEM((tq, SB_HEAD_DIM), F32), pltpu.VMEM((tq, 1), F32)],
        compiler_params=_cp(("parallel", "parallel", "arbitrary")),
        name="stick_breaking",
    )(q_arr, k_arr, v_arr)


def _mlstm_kernel(mq_ref, mk_ref, mv_ref, mo_ref, if_ref, c0_ref, n0_ref, m0_ref, g_ref,
                  yc_ref, c_out, n_out, m_out, c_s, n_s, m_s, *, ck):
    tl = mq_ref.shape[1]
    li = pl.program_id(1)

    @pl.when(li == 0)
    def _():
        c_s[...] = c0_ref[0]
        n_s[...] = n0_ref[0]
        for h in range(ML_HEADS):
            m_s[h:h + 1, :] = jnp.broadcast_to(m0_ref[0, :, h:h + 1], (1, LANES))

    t_i = lax.broadcasted_iota(jnp.int32, (ck, ck), 0)
    s_i = lax.broadcasted_iota(jnp.int32, (ck, ck), 1)
    causal = s_i <= t_i
    diag = s_i == t_i

    def chunk(ci, carry):
        r0 = pl.multiple_of(ci * ck, ck)
        ifb = if_ref[0, pl.ds(r0, ck), :]
        for h in range(ML_HEADS):
            qh = mq_ref[0, pl.ds(r0, ck), h * ML_QK_DIM:(h + 1) * ML_QK_DIM]
            kh = mk_ref[0, pl.ds(r0, ck), h * ML_QK_DIM:(h + 1) * ML_QK_DIM] * (ML_QK_DIM ** -0.5)
            vh = mv_ref[0, pl.ds(r0, ck), h * ML_V_DIM:(h + 1) * ML_V_DIM]
            oh = mo_ref[0, pl.ds(r0, ck), h * ML_V_DIM:(h + 1) * ML_V_DIM]
            ig_col = ifb[:, h:h + 1]
            lf_col = -_softplus(-ifb[:, ML_HEADS + h:ML_HEADS + h + 1])
            ig_row = jnp.sum(jnp.where(diag, ig_col, 0.0), axis=0, keepdims=True)
            lf_row = jnp.sum(jnp.where(diag, lf_col, 0.0), axis=0, keepdims=True)
            b_col = jnp.sum(jnp.where(causal, lf_row, 0.0), axis=1, keepdims=True)
            b_row = jnp.sum(jnp.where(t_i <= s_i, lf_col, 0.0), axis=0, keepdims=True)
            m_prev = m_s[h:h + 1, 0:1]
            dmat = jnp.where(causal, b_col - b_row + ig_row, -jnp.inf)
            inter = b_col + m_prev
            m_t = jnp.maximum(inter, jnp.max(dmat, axis=1, keepdims=True))
            s_inter = jnp.exp(inter - m_t)
            qb = qh.astype(BF16)
            kb = kh.astype(BF16)
            vb = vh.astype(BF16)
            wqk = jnp.exp(dmat - m_t) * _dot_nt(qb, kb)
            c_prev = c_s[h]
            n_prev = n_s[h:h + 1, :]
            num = s_inter * _dot(qb, c_prev.astype(BF16)) + _dot(wqk.astype(BF16), vb)
            den = s_inter * jnp.sum(qh * n_prev, axis=1, keepdims=True) + jnp.sum(wqk, axis=1, keepdims=True)
            hh = num / jnp.maximum(jnp.abs(den), jnp.exp(-m_t))
            b_end = b_col[ck - 1:ck, :]
            g_col = b_end - b_col + ig_col
            m_new = jnp.maximum(b_end + m_prev, jnp.max(g_col, axis=0, keepdims=True))
            s_old = jnp.exp(b_end + m_prev - m_new)
            kw = kh * jnp.exp(g_col - m_new)
            c_s[h] = s_old * c_prev + _dot_tn(kw.astype(BF16), vb)
            n_s[h:h + 1, :] = s_old * n_prev + jnp.sum(kw, axis=0, keepdims=True)
            m_s[h:h + 1, :] = jnp.broadcast_to(m_new, (1, LANES))
            mu = jnp.mean(hh, axis=1, keepdims=True)
            dlt = hh - mu
            var = jnp.mean(dlt * dlt, axis=1, keepdims=True)
            hn = dlt * lax.rsqrt(var + LN_EPS) * g_ref[:, h * ML_V_DIM:(h + 1) * ML_V_DIM]
            yc_ref[0, pl.ds(r0, ck), h * ML_V_DIM:(h + 1) * ML_V_DIM] = (hn * _sigmoid(oh)).astype(yc_ref.dtype)
        return carry

    lax.fori_loop(0, tl // ck, chunk, 0)
    c_out[0] = c_s[...]
    n_out[0] = n_s[...]
    lane = lax.broadcasted_iota(jnp.int32, (1, LANES), 1)
    mrow = jnp.zeros((1, LANES), F32)
    for h in range(ML_HEADS):
        mrow = jnp.where(lane == h, m_s[h:h + 1, :], mrow)
    m_out[0] = mrow


def _mlstm(z3, c0, n0, m0p, g, tl, ck):
    B, S, _ = z3.shape
    return pl.pallas_call(
        functools.partial(_mlstm_kernel, ck=ck),
        grid=(B, S // tl),
        in_specs=[pl.BlockSpec((1, tl, ML_QK_WIDTH), lambda b, l: (b, l, OFF_MQ // ML_QK_WIDTH)),
                  pl.BlockSpec((1, tl, ML_QK_WIDTH), lambda b, l: (b, l, OFF_MK // ML_QK_WIDTH)),
                  pl.BlockSpec((1, tl, ML_V_WIDTH), lambda b, l: (b, l, OFF_MV // ML_V_WIDTH)),
                  pl.BlockSpec((1, tl, ML_V_WIDTH), lambda b, l: (b, l, OFF_MO // ML_V_WIDTH)),
                  pl.BlockSpec((1, tl, LANES), lambda b, l: (b, l, OFF_IF // LANES)),
                  pl.BlockSpec((1, ML_HEADS, ML_QK_DIM, ML_V_DIM), lambda b, l: (b, 0, 0, 0)),
                  pl.BlockSpec((1, ML_HEADS, ML_QK_DIM), lambda b, l: (b, 0, 0)),
                  pl.BlockSpec((1, 1, LANES), lambda b, l: (b, 0, 0)),
                  pl.BlockSpec((1, ML_V_WIDTH), lambda b, l: (0, 0))],
        out_specs=[pl.BlockSpec((1, tl, ML_V_WIDTH), lambda b, l: (b, l, 0)),
                   pl.BlockSpec((1, ML_HEADS, ML_QK_DIM, ML_V_DIM), lambda b, l: (b, 0, 0, 0)),
                   pl.BlockSpec((1, ML_HEADS, ML_QK_DIM), lambda b, l: (b, 0, 0)),
                   pl.BlockSpec((1, 1, LANES), lambda b, l: (b, 0, 0))],
        out_shape=[jax.ShapeDtypeStruct((B, S, ML_V_WIDTH), BF16),
                   jax.ShapeDtypeStruct((B, ML_HEADS, ML_QK_DIM, ML_V_DIM), F32),
                   jax.ShapeDtypeStruct((B, ML_HEADS, ML_QK_DIM), F32),
                   jax.ShapeDtypeStruct((B, 1, LANES), F32)],
        scratch_shapes=[pltpu.VMEM((ML_HEADS, ML_QK_DIM, ML_V_DIM), F32),
                        pltpu.VMEM((ML_HEADS, ML_QK_DIM), F32),
                        pltpu.VMEM((8, LANES), F32)],
        compiler_params=_cp(("parallel", "arbitrary")),
        name="mlstm",
    )(z3, z3, z3, z3, z3, c0, n0, m0p, g)


def _merge_kernel(ya_ref, yb_ref, yc_ref, g0_ref, g1_ref, g2_ref, x_ref, wpa_ref, wpb_ref, wpc_ref,
                  wo_ref, lg_ref, lb_ref, o_ref, *, alpha):
    mixed = _sigmoid(g0_ref[...]) * _dot(ya_ref[...], wpa_ref[...])
    mixed = mixed + _sigmoid(g1_ref[...]) * _dot(yb_ref[...], wpb_ref[...])
    mixed = mixed + _sigmoid(g2_ref[...]) * _dot(yc_ref[...], wpc_ref[...])
    r = alpha * x_ref[...] + _dot(mixed.astype(BF16), wo_ref[...])
    o_ref[...] = _layer_norm(r, lg_ref[...], lb_ref[...])


def _merge(ya, yb, yc, z, x, wpa, wpb, wpc, wo, lg, lb, tm, alpha):
    T = x.shape[0]
    full = lambda shape: pl.BlockSpec(shape, lambda i: (0,) * len(shape))
    mgb = OFF_MG // D_MODEL
    return pl.pallas_call(
        functools.partial(_merge_kernel, alpha=alpha),
        grid=(T // tm,),
        in_specs=[pl.BlockSpec((tm, D_RNN), lambda i: (i, 0)),
                  pl.BlockSpec((tm, SB_WIDTH), lambda i: (i, 0)),
                  pl.BlockSpec((tm, ML_V_WIDTH), lambda i: (i, 0)),
                  pl.BlockSpec((tm, D_MODEL), lambda i: (i, mgb)),
                  pl.BlockSpec((tm, D_MODEL), lambda i: (i, mgb + 1)),
                  pl.BlockSpec((tm, D_MODEL), lambda i: (i, mgb + 2)),
                  pl.BlockSpec((tm, D_MODEL), lambda i: (i, 0)),
                  full((D_RNN, D_MODEL)), full((SB_WIDTH, D_MODEL)), full((ML_V_WIDTH, D_MODEL)),
                  full((D_MODEL, D_MODEL)), full((1, D_MODEL)), full((1, D_MODEL))],
        out_specs=pl.BlockSpec((tm, D_MODEL), lambda i: (i, 0)),
        out_shape=jax.ShapeDtypeStruct((T, D_MODEL), F32),
        compiler_params=_cp(("parallel",)),
        name="merge_ln1",
    )(ya, yb, yc, z, z, z, x, wpa, wpb, wpc, wo, lg, lb)


def _router_kernel(x_ref, rw_ref, rb_ref, tri_ref, topi_ref, gate_ref, rank_ref, cnt_ref, run_ref):
    @pl.when(pl.program_id(0) == 0)
    def _():
        run_ref[...] = jnp.zeros_like(run_ref)

    tm = x_ref.shape[0]
    l = _dot(x_ref[...].astype(BF16), rw_ref[...]) + rb_ref[...]
    lane = lax.broadcasted_iota(jnp.int32, (tm, LANES), 1)
    lane_f = lane.astype(F32)
    vals, idxs, hots = [], [], []
    for _ in range(TOP_K):
        m = jnp.max(l, axis=1, keepdims=True)
        idx = jnp.min(jnp.where(l == m, lane_f, float(LANES)), axis=1, keepdims=True)
        hot = lane_f == idx
        vals.append(m)
        idxs.append(idx)
        hots.append(hot)
        l = jnp.where(hot, -jnp.inf, l)
    ex = [jnp.exp(v - vals[0]) for v in vals]
    den = ex[0] + ex[1] + ex[2] + ex[3]
    onehot = jnp.zeros((tm, LANES), F32)
    for hot in hots:
        onehot = onehot + jnp.where(hot, 1.0, 0.0)
    before = _dot(tri_ref[...], onehot.astype(BF16)) + run_ref[...]
    topi = jnp.zeros((tm, LANES), F32)
    gate = jnp.zeros((tm, LANES), F32)
    rank = jnp.zeros((tm, LANES), F32)
    for k in range(TOP_K):
        rk = jnp.sum(jnp.where(hots[k], before, 0.0), axis=1, keepdims=True)
        topi = jnp.where(lane == k, idxs[k], topi)
        gate = jnp.where(lane == k, ex[k] / den, gate)
        rank = jnp.where(lane == k, rk, rank)
    run_new = run_ref[...] + jnp.sum(onehot, axis=0, keepdims=True)
    run_ref[...] = run_new
    topi_ref[...] = topi.astype(jnp.int32)
    gate_ref[...] = gate
    rank_ref[...] = rank.astype(jnp.int32)
    cnt_ref[...] = run_new.astype(jnp.int32)


def _router(x, rw, rb, tri, tm):
    T = x.shape[0]
    full = lambda shape: pl.BlockSpec(shape, lambda i: (0,) * len(shape))
    return pl.pallas_call(
        _router_kernel,
        grid=(T // tm,),
        in_specs=[pl.BlockSpec((tm, D_MODEL), lambda i: (i, 0)),
                  full((D_MODEL, LANES)), full((1, LANES)), full((tm, tm))],
        out_specs=[pl.BlockSpec((tm, LANES), lambda i: (i, 0)),
                   pl.BlockSpec((tm, LANES), lambda i: (i, 0)),
                   pl.BlockSpec((tm, LANES), lambda i: (i, 0)),
                   full((1, LANES))],
        out_shape=[jax.ShapeDtypeStruct((T, LANES), jnp.int32),
                   jax.ShapeDtypeStruct((T, LANES), F32),
                   jax.ShapeDtypeStruct((T, LANES), jnp.int32),
                   jax.ShapeDtypeStruct((1, LANES), jnp.int32)],
        scratch_shapes=[pltpu.VMEM((1, LANES), F32)],
        compiler_params=_cp(("arbitrary",)),
        name="router",
    )(x, rw, rb, tri)


def _row_copy(src_hbm, src_row, dst_ref, dst_row, sem):
    return pltpu.make_async_copy(src_hbm.at[pl.ds(src_row, 1)], dst_ref.at[pl.ds(dst_row, 1)], sem)


def _dispatch_kernel(off_ref, topi_ref, rank_ref, x_hbm, xs_in_hbm, xs_hbm, sem, *, ts):
    del xs_in_hbm
    base = pl.program_id(0) * ts

    def issue(t, c):
        for k in range(TOP_K):
            dst = off_ref[topi_ref[TOP_K * t + k]] + rank_ref[TOP_K * t + k]
            _row_copy(x_hbm, base + t, xs_hbm, dst, sem).start()
        return c

    lax.fori_loop(0, ts, issue, 0)

    def drain(t, c):
        for k in range(TOP_K):
            _row_copy(x_hbm, 0, xs_hbm, 0, sem).wait()
        return c

    lax.fori_loop(0, ts, drain, 0)


def _dispatch(off, topi_flat, rank_flat, x, xs_zero, ts):
    T = x.shape[0]
    grid_spec = pltpu.PrefetchScalarGridSpec(
        num_scalar_prefetch=1,
        grid=(T // ts,),
        in_specs=[pl.BlockSpec((TOP_K * ts,), lambda i, off: (i,), memory_space=pltpu.SMEM),
                  pl.BlockSpec((TOP_K * ts,), lambda i, off: (i,), memory_space=pltpu.SMEM),
                  pl.BlockSpec(memory_space=pl.ANY),
                  pl.BlockSpec(memory_space=pl.ANY)],
        out_specs=pl.BlockSpec(memory_space=pl.ANY),
        scratch_shapes=[pltpu.SemaphoreType.DMA(())],
    )
    return pl.pallas_call(
        functools.partial(_dispatch_kernel, ts=ts),
        grid_spec=grid_spec,
        out_shape=jax.ShapeDtypeStruct(xs_zero.shape, xs_zero.dtype),
        input_output_aliases={4: 0},
        compiler_params=_cp(("arbitrary",)),
        name="moe_dispatch",
    )(off, topi_flat, rank_flat, x, xs_zero)


def _experts_kernel(be_ref, nu_ref, xs_ref, w1_ref, b1_ref, w2_ref, b2_ref, y_ref):
    @pl.when(pl.program_id(0) < nu_ref[0])
    def _():
        gu = _dot(xs_ref[...].astype(BF16), w1_ref[0]) + b1_ref[0]
        g_ = jnp.minimum(gu[:, :D_FF], SWIGLU_LIMIT)
        up = jnp.clip(gu[:, D_FF:], -SWIGLU_LIMIT, SWIGLU_LIMIT)
        act = (up + 1.0) * g_ * _sigmoid(SWIGLU_ALPHA * g_)
        y_ref[...] = _dot(act.astype(BF16), w2_ref[0]) + b2_ref[0]

    @pl.when(pl.program_id(0) >= nu_ref[0])
    def _():
        y_ref[...] = jnp.zeros_like(y_ref)


def _experts(block_e, n_used, xs, w1, b1, w2, b2, m):
    R = xs.shape[0]
    grid_spec = pltpu.PrefetchScalarGridSpec(
        num_scalar_prefetch=2,
        grid=(R // m,),
        in_specs=[pl.BlockSpec((m, D_MODEL), lambda j, be, nu: (j, 0)),
                  pl.BlockSpec((1, D_MODEL, 2 * D_FF), lambda j, be, nu: (be[j], 0, 0)),
                  pl.BlockSpec((1, 1, 2 * D_FF), lambda j, be, nu: (be[j], 0, 0)),
                  pl.BlockSpec((1, D_FF, D_MODEL), lambda j, be, nu: (be[j], 0, 0)),
                  pl.BlockSpec((1, 1, D_MODEL), lambda j, be, nu: (be[j], 0, 0))],
        out_specs=pl.BlockSpec((m, D_MODEL), lambda j, be, nu: (j, 0)),
    )
    return pl.pallas_call(
        _experts_kernel,
        grid_spec=grid_spec,
        out_shape=jax.ShapeDtypeStruct((R, D_MODEL), F32),
        compiler_params=_cp(("arbitrary",)),
        name="moe_experts",
    )(block_e, n_used, xs, w1, b1, w2, b2)


def _combine_kernel(off_ref, topi_ref, rank_ref, x_ref, p_ref, gate_ref, y_hbm, wp_ref, wg_ref, bg_ref,
                    lg_ref, lb_ref, o_ref, ybuf, sem, *, alpha):
    tm = x_ref.shape[0]

    def issue(t, c):
        for k in range(TOP_K):
            src = off_ref[topi_ref[TOP_K * t + k]] + rank_ref[TOP_K * t + k]
            _row_copy(y_hbm, src, ybuf.at[k], t, sem).start()
        return c

    lax.fori_loop(0, tm, issue, 0)

    x = x_ref[...]
    ple = _sigmoid(_dot(x.astype(BF16), wg_ref[...]) + bg_ref[...]) * _dot(p_ref[...].astype(BF16), wp_ref[...])
    r = alpha * x + ple

    def drain(t, c):
        for k in range(TOP_K):
            _row_copy(y_hbm, 0, ybuf.at[k], 0, sem).wait()
        return c

    lax.fori_loop(0, tm, drain, 0)
    for k in range(TOP_K):
        r = r + gate_ref[:, k:k + 1] * ybuf[k]
    o_ref[...] = _layer_norm(r, lg_ref[...], lb_ref[...])


def _combine(off, topi_flat, rank_flat, x, p, gate, y, wp, wg, bg, lg, lb, tm, alpha):
    T = x.shape[0]
    full = lambda shape: pl.BlockSpec(shape, lambda i, off: (0,) * len(shape))
    grid_spec = pltpu.PrefetchScalarGridSpec(
        num_scalar_prefetch=1,
        grid=(T // tm,),
        in_specs=[pl.BlockSpec((TOP_K * tm,), lambda i, off: (i,), memory_space=pltpu.SMEM),
                  pl.BlockSpec((TOP_K * tm,), lambda i, off: (i,), memory_space=pltpu.SMEM),
                  pl.BlockSpec((tm, D_MODEL), lambda i, off: (i, 0)),
                  pl.BlockSpec((tm, D_PLE), lambda i, off: (i, 0)),
                  pl.BlockSpec((tm, LANES), lambda i, off: (i, 0)),
                  pl.BlockSpec(memory_space=pl.ANY),
                  full((D_PLE, D_MODEL)), full((D_MODEL, D_MODEL)), full((1, D_MODEL)),
                  full((1, D_MODEL)), full((1, D_MODEL))],
        out_specs=pl.BlockSpec((tm, D_MODEL), lambda i, off: (i, 0)),
        scratch_shapes=[pltpu.VMEM((TOP_K, tm, D_MODEL), F32), pltpu.SemaphoreType.DMA(())],
    )
    return pl.pallas_call(
        functools.partial(_combine_kernel, alpha=alpha),
        grid_spec=grid_spec,
        out_shape=jax.ShapeDtypeStruct((T, D_MODEL), F32),
        compiler_params=_cp(("arbitrary",)),
        name="moe_combine_ln2",
    )(off, topi_flat, rank_flat, x, p, gate, y, wp, wg, bg, lg, lb)


def _pick(n, pref):
    t = min(n, pref)
    while n % t:
        t //= 2
    return t


def _layer(x, p, lw, conv_buf, lru_h, ml_c, ml_n, ml_m, k_past, v_past, prompt, alpha):
    B, L, _ = x.shape
    T = B * L
    xf = x.reshape(T, D_MODEL)

    z = _inproj(xf, lw["w_in"], lw["b_in"], _pick(T, 1024), 640)
    z3 = z.reshape(B, L, N_IN_PAD)
    k_new = z3[:, :, OFF_K:OFF_K + SB_WIDTH]
    v_new = z3[:, :, OFF_V:OFF_V + SB_WIDTH]
    conv_new = z3[:, L - (CONV_W - 1):, OFF_AX:OFF_AX + D_RNN]

    cbuf8 = jnp.concatenate([jnp.zeros((B, 8 - (CONV_W - 1), D_RNN), F32), conv_buf.astype(F32)], axis=1)
    ya, lru_new = _lru(z3, cbuf8, lru_h.reshape(B, 1, D_RNN).astype(F32), lw["conv_w8"], lw["conv_b"],
                       lw["wr"], lw["br"], lw["wi"], lw["bi"], lw["lam"], _pick(L, 256), prompt)

    if prompt:
        yb = _sb_attention(z3, z3, z3, OFF_Q // SB_HEAD_DIM, OFF_K // SB_HEAD_DIM, OFF_V // SB_HEAD_DIM,
                           _pick(L, 512), 0)
    else:
        P = k_past.shape[1]
        pad = (-(P + L)) % KEY_BLOCK
        zpad = jnp.zeros((B, pad, SB_WIDTH), F32)
        k_all = jnp.concatenate([k_past.reshape(B, P, SB_WIDTH), k_new, zpad], axis=1)
        v_all = jnp.concatenate([v_past.reshape(B, P, SB_WIDTH), v_new, zpad], axis=1)
        yb = _sb_attention(z3, k_all, v_all, OFF_Q // SB_HEAD_DIM, 0, 0, L, P)

    m0p = jnp.pad(ml_m.astype(F32), ((0, 0), (0, LANES - ML_HEADS))).reshape(B, 1, LANES)
    ck = _pick(L, 64)
    yc, c_new, n_new, m_new = _mlstm(z3, ml_c.astype(F32), ml_n.astype(F32), m0p, lw["ml_g"],
                                     _pick(L, 512), ck)
    m_new = m_new[:, 0, :ML_HEADS]

    x1 = _merge(ya.reshape(T, D_RNN), yb.reshape(T, SB_WIDTH), yc.reshape(T, ML_V_WIDTH), z, xf,
                lw["w_pa"], lw["w_pb"], lw["w_pc"], lw["w_out"], lw["ln1_g"], lw["ln1_b"],
                _pick(T, 512), alpha)

    tr = _pick(T, 512)
    tri = jnp.tril(jnp.ones((tr, tr), BF16), -1)
    topi, gate, rank, cnt = _router(x1, lw["router_w"], lw["router_b"], tri, tr)
    TK = T * TOP_K
    m_rows = max(16, min(512, TK // N_EXPERTS))
    nb = TK // m_rows + N_EXPERTS
    counts = cnt[0, :N_EXPERTS]
    padded = (counts + (m_rows - 1)) // m_rows * m_rows
    pend = jnp.cumsum(padded)
    off = (pend - padded).astype(jnp.int32)
    block_e = jnp.minimum(jnp.searchsorted(pend, jnp.arange(nb, dtype=jnp.int32) * m_rows, side="right"),
                          N_EXPERTS - 1).astype(jnp.int32)
    n_used = (pend[-1:] // m_rows).astype(jnp.int32)
    topi_flat = topi[:, :TOP_K].reshape(TK)
    rank_flat = rank[:, :TOP_K].reshape(TK)
    xs = _dispatch(off, topi_flat, rank_flat, x1, jnp.zeros((nb * m_rows, D_MODEL), F32), _pick(T, 256))
    y = _experts(block_e, n_used, xs, lw["exp_w1"], lw["exp_b1"], lw["exp_w2"], lw["exp_b2"], m_rows)
    x2 = _combine(off, topi_flat, rank_flat, x1, p.reshape(T, D_PLE), gate, y, lw["ple_w"], lw["ple_gate_w"],
                  lw["ple_gate_b"], lw["ln2_g"], lw["ln2_b"], _pick(T, 256), alpha)

    k_out = k_new.reshape(B, L, SB_HEADS, SB_HEAD_DIM)
    v_out = v_new.reshape(B, L, SB_HEADS, SB_HEAD_DIM)
    return x2.reshape(B, L, D_MODEL), (k_out, v_out, conv_new, lru_new.reshape(B, D_RNN), c_new, n_new, m_new)


def _block_diag_groups(w):
    per = LRU_GROUP // LRU_BLOCK
    w4 = w.reshape(D_RNN // LRU_GROUP, per, LRU_BLOCK, LRU_BLOCK)
    eye = jnp.eye(per, dtype=w.dtype)
    return jnp.einsum("gacd,ab->gacbd", w4, eye).reshape(D_RNN // LRU_GROUP, LRU_GROUP, LRU_GROUP)


def _reorder_in(w):
    pad = jnp.zeros(w.shape[:-1] + (N_IN_PAD - OFF_IF - 2 * ML_HEADS,), w.dtype)
    return jnp.concatenate([w[..., :ORIG_IF], w[..., ORIG_IF + 2 * ML_HEADS:],
                            w[..., ORIG_IF:ORIG_IF + 2 * ML_HEADS], pad], axis=-1)


def kernel(x_prompt, x_sample, cache_sb_k, cache_sb_v, state_conv, state_lru, state_mlstm_c, state_mlstm_n, state_mlstm_m, p_prompt, p_sample, w_in, b_in, conv_w, conv_b, lru_wr, lru_br, lru_wi, lru_bi, lru_lambda, ml_norm_g, w_pa, w_pb, w_pc, w_out, ln1_g, ln1_b, router_w, router_b, exp_w1, exp_b1, exp_w2, exp_b2, ple_w, ple_gate_w, ple_gate_b, ln2_g, ln2_b):
    depth = w_in.shape[0]
    alpha = (2 * depth) ** 0.25
    B = x_prompt.shape[0]
    y_prompt, y_sample = x_prompt, x_sample
    st_p, st_s = [], []
    for i in range(depth):
        row = lambda a: a[i].reshape(1, -1).astype(F32)
        lw = dict(
            w_in=_reorder_in(w_in[i]).astype(BF16), b_in=_reorder_in(b_in[i]).reshape(1, -1).astype(F32),
            conv_w8=jnp.pad(conv_w[i].astype(F32), ((0, 8 - CONV_W), (0, 0))), conv_b=row(conv_b),
            wr=_block_diag_groups(lru_wr[i]).astype(BF16), br=row(lru_br),
            wi=_block_diag_groups(lru_wi[i]).astype(BF16), bi=row(lru_bi), lam=row(lru_lambda),
            ml_g=row(ml_norm_g),
            w_pa=w_pa[i].astype(BF16), w_pb=w_pb[i].astype(BF16), w_pc=w_pc[i].astype(BF16),
            w_out=w_out[i].astype(BF16), ln1_g=row(ln1_g), ln1_b=row(ln1_b),
            router_w=jnp.pad(router_w[i], ((0, 0), (0, LANES - N_EXPERTS))).astype(BF16),
            router_b=jnp.pad(router_b[i].astype(F32), (0, LANES - N_EXPERTS),
                             constant_values=-1e30).reshape(1, LANES),
            exp_w1=exp_w1[i].astype(BF16), exp_b1=exp_b1[i].reshape(N_EXPERTS, 1, 2 * D_FF).astype(F32),
            exp_w2=exp_w2[i].astype(BF16), exp_b2=exp_b2[i].reshape(N_EXPERTS, 1, D_MODEL).astype(F32),
            ple_w=ple_w[i].astype(BF16), ple_gate_w=ple_gate_w[i].astype(BF16), ple_gate_b=row(ple_gate_b),
            ln2_g=row(ln2_g), ln2_b=row(ln2_b),
        )
        y_prompt, sp = _layer(y_prompt, p_prompt[i], lw,
                              jnp.zeros((B, CONV_W - 1, D_RNN), F32), jnp.zeros((B, D_RNN), F32),
                              jnp.zeros((B, ML_HEADS, ML_QK_DIM, ML_V_DIM), F32),
                              jnp.zeros((B, ML_HEADS, ML_QK_DIM), F32), jnp.zeros((B, ML_HEADS), F32),
                              None, None, True, alpha)
        y_sample, ss = _layer(y_sample, p_sample[i], lw, state_conv[i], state_lru[i], state_mlstm_c[i],
                              state_mlstm_n[i], state_mlstm_m[i], cache_sb_k[i], cache_sb_v[i], False, alpha)
        st_p.append(sp)
        st_s.append(ss)
    k_p, v_p, conv_p, lru_p, c_p, n_p, m_p = [jnp.stack(s) for s in zip(*st_p)]
    k_s, v_s, conv_s, lru_s, c_s, n_s, m_s = [jnp.stack(s) for s in zip(*st_s)]
    return (y_prompt, y_sample, k_p, v_p, conv_p, lru_p, c_p, n_p, m_p,
            k_s, v_s, conv_s, lru_s, c_s, n_s, m_s)
```

```python
import functools

import jax
import jax.numpy as jnp
from jax import lax
from jax.experimental import pallas as pl
from jax.experimental.pallas import tpu as pltpu

F32 = jnp.float32
BF16 = jnp.bfloat16

D_MODEL = 1024
D_RNN = 1024
N_LRU_BLOCKS = 16
LRU_BLOCK = D_RNN // N_LRU_BLOCKS
LRU_GROUP = 256
CONV_W = 4
LRU_C = 8.0
SB_HEADS = 4
SB_HEAD_DIM = 128
SB_WIDTH = SB_HEADS * SB_HEAD_DIM
ML_HEADS = 4
ML_QK_DIM = 64
ML_V_DIM = 128
ML_QK_WIDTH = ML_HEADS * ML_QK_DIM
ML_V_WIDTH = ML_HEADS * ML_V_DIM
N_BRANCH = 3
N_EXPERTS = 32
TOP_K = 4
D_FF = 512
SWIGLU_LIMIT = 7.0
SWIGLU_ALPHA = 1.702
D_PLE = 256
LN_EPS = 1e-5
LANES = 128
KEY_BLOCK = 128

OFF_AX, OFF_AG = 0, 1024
OFF_Q = 2048
OFF_MQ, OFF_MK, OFF_MV, OFF_MO = 2560, 2816, 3072, 3584
OFF_MG = 4096
N_MAIN = 7168
N_KVIF = 2 * SB_WIDTH + LANES
ORIG_K, ORIG_MQ, ORIG_IF, ORIG_MG = 2560, 3584, 5120, 5128
SB_DEAD_MASS = 88.0
VMEM_LIMIT = 56 * 1024 * 1024


def _cp(sem, vmem=VMEM_LIMIT):
    return pltpu.CompilerParams(dimension_semantics=sem, vmem_limit_bytes=vmem)


def _dot(a, b):
    return jnp.dot(a, b, preferred_element_type=F32)


def _dot_nt(a, b):
    return lax.dot_general(a, b, (((1,), (1,)), ((), ())), preferred_element_type=F32)


def _dot_tn(a, b):
    return lax.dot_general(a, b, (((0,), (0,)), ((), ())), preferred_element_type=F32)


def _sigmoid(x):
    return 0.5 * jnp.tanh(0.5 * x) + 0.5


def _softplus(x):
    return jnp.maximum(x, 0.0) + jnp.log1p(jnp.exp(-jnp.abs(x)))


def _layer_norm(r, g, b):
    mu = jnp.mean(r, axis=-1, keepdims=True)
    d = r - mu
    var = jnp.mean(d * d, axis=-1, keepdims=True)
    return d * lax.rsqrt(var + LN_EPS) * g + b


def _inproj_kernel(x_ref, w_ref, b_ref, o_ref, xb_ref):
    @pl.when(pl.program_id(1) == 0)
    def _():
        xb_ref[...] = x_ref[...].astype(BF16)

    o_ref[...] = _dot(xb_ref[...], w_ref[...]) + b_ref[...]


def _inproj(x, w, b, tm, tn):
    T, D = x.shape
    N = w.shape[1]
    return pl.pallas_call(
        _inproj_kernel,
        grid=(T // tm, N // tn),
        in_specs=[pl.BlockSpec((tm, D), lambda i, j: (i, 0)),
                  pl.BlockSpec((D, tn), lambda i, j: (0, j)),
                  pl.BlockSpec((1, tn), lambda i, j: (0, j))],
        out_specs=pl.BlockSpec((tm, tn), lambda i, j: (i, j)),
        out_shape=jax.ShapeDtypeStruct((T, N), F32),
        scratch_shapes=[pltpu.VMEM((tm, D), BF16)],
        compiler_params=_cp(("parallel", "arbitrary")),
        name="inproj",
    )(x, w, b)


def _kvif_kernel(x_ref, w_ref, b_ref, k_ref, v_ref, if_ref):
    r = _dot(x_ref[...].astype(BF16), w_ref[...]) + b_ref[...]
    k_ref[...] = r[:, :SB_WIDTH]
    v_ref[...] = r[:, SB_WIDTH:2 * SB_WIDTH]
    if_ref[...] = r[:, 2 * SB_WIDTH:]


def _kvif(x, w, b, tm):
    T, D = x.shape
    return pl.pallas_call(
        _kvif_kernel,
        grid=(T // tm,),
        in_specs=[pl.BlockSpec((tm, D), lambda i: (i, 0)),
                  pl.BlockSpec((D, N_KVIF), lambda i: (0, 0)),
                  pl.BlockSpec((1, N_KVIF), lambda i: (0, 0))],
        out_specs=[pl.BlockSpec((tm, SB_WIDTH), lambda i: (i, 0)),
                   pl.BlockSpec((tm, SB_WIDTH), lambda i: (i, 0)),
                   pl.BlockSpec((tm, LANES), lambda i: (i, 0))],
        out_shape=[jax.ShapeDtypeStruct((T, SB_WIDTH), F32),
                   jax.ShapeDtypeStruct((T, SB_WIDTH), F32),
                   jax.ShapeDtypeStruct((T, LANES), F32)],
        compiler_params=_cp(("parallel",)),
        name="inproj_kvif",
    )(x, w, b)


def _lru_kernel(ax_ref, ag_ref, cbuf_ref, h0_ref, cw_ref, cb_ref, wr_ref, br_ref, wi_ref, bi_ref,
                lam_ref, ya_ref, hl_ref, ext_ref, hc_ref, *, first_frame):
    tl = ax_ref.shape[1]
    li = pl.program_id(1)

    @pl.when(li == 0)
    def _():
        ext_ref[0:8, :] = cbuf_ref[0]
        hc_ref[...] = h0_ref[0]

    ext_ref[8:8 + tl, :] = ax_ref[0]
    ua = cb_ref[...] + ext_ref[5:5 + tl, :] * cw_ref[0:1, :]
    for j in range(1, CONV_W):
        ua = ua + ext_ref[5 + j:5 + j + tl, :] * cw_ref[j:j + 1, :]
    ext_ref[5:8, :] = ext_ref[5 + tl:8 + tl, :]

    uab = ua.astype(BF16)
    ng = D_RNN // LRU_GROUP
    r_pre = jnp.concatenate(
        [_dot(uab[:, g * LRU_GROUP:(g + 1) * LRU_GROUP], wr_ref[g]) for g in range(ng)], axis=1)
    i_pre = jnp.concatenate(
        [_dot(uab[:, g * LRU_GROUP:(g + 1) * LRU_GROUP], wi_ref[g]) for g in range(ng)], axis=1)
    r = _sigmoid(r_pre + br_ref[...])
    ig = _sigmoid(i_pre + bi_ref[...])
    log_a = (-LRU_C) * r * _softplus(-lam_ref[...])
    a = jnp.exp(log_a)
    th = jnp.tanh(log_a)
    mult = jnp.sqrt(-2.0 * th / (1.0 - th))
    row = lax.broadcasted_iota(jnp.int32, (tl, D_RNN), 0)
    if first_frame:
        mult = jnp.where((row == 0) & (li == 0), 1.0, mult)
    bx = mult * (ig * ua)

    a = a.reshape(tl // 8, 8, D_RNN)
    bx = bx.reshape(tl // 8, 8, D_RNN)
    sub = lax.broadcasted_iota(jnp.int32, (tl // 8, 8, D_RNN), 1)
    for d in (1, 2, 4):
        a_sh = jnp.where(sub < d, 1.0, pltpu.roll(a, d, 1))
        b_sh = jnp.where(sub < d, 0.0, pltpu.roll(bx, d, 1))
        bx = a * b_sh + bx
        a = a * a_sh
    carry = hc_ref[...]
    groups = []
    for g in range(tl // 8):
        hg = bx[g] + a[g] * carry
        carry = hg[7:8, :]
        groups.append(hg)
    h = jnp.concatenate(groups, axis=0)
    hc_ref[...] = carry
    hl_ref[0] = carry

    gx = ag_ref[0]
    gelu = 0.5 * gx * (1.0 + jnp.tanh(0.7978845608028654 * (gx + 0.044715 * (gx * gx * gx))))
    ya_ref[0] = (h * gelu).astype(ya_ref.dtype)


def _lru(z3, cbuf8, h0, cw8, cb, wr, br, wi, bi, lam, tl, first_frame):
    B, S, _ = z3.shape
    full = lambda shape: pl.BlockSpec(shape, lambda b, l: (0,) * len(shape))
    return pl.pallas_call(
        functools.partial(_lru_kernel, first_frame=first_frame),
        grid=(B, S // tl),
        in_specs=[pl.BlockSpec((1, tl, D_RNN), lambda b, l: (b, l, OFF_AX // D_RNN)),
                  pl.BlockSpec((1, tl, D_RNN), lambda b, l: (b, l, OFF_AG // D_RNN)),
                  pl.BlockSpec((1, 8, D_RNN), lambda b, l: (b, 0, 0)),
                  pl.BlockSpec((1, 1, D_RNN), lambda b, l: (b, 0, 0)),
                  full((8, D_RNN)), full((1, D_RNN)),
                  full((D_RNN // LRU_GROUP, LRU_GROUP, LRU_GROUP)), full((1, D_RNN)),
                  full((D_RNN // LRU_GROUP, LRU_GROUP, LRU_GROUP)), full((1, D_RNN)),
                  full((1, D_RNN))],
        out_specs=[pl.BlockSpec((1, tl, D_RNN), lambda b, l: (b, l, 0)),
                   pl.BlockSpec((1, 1, D_RNN), lambda b, l: (b, 0, 0))],
        out_shape=[jax.ShapeDtypeStruct((B, S, D_RNN), BF16),
                   jax.ShapeDtypeStruct((B, 1, D_RNN), F32)],
        scratch_shapes=[pltpu.VMEM((tl + 8, D_RNN), F32), pltpu.VMEM((1, D_RNN), F32)],
        compiler_params=_cp(("parallel", "arbitrary")),
        name="conv_rglru",
    )(z3, z3, cbuf8, h0, cw8, cb, wr, br, wi, bi, lam)


def _sb_kernel(q_ref, k_ref, v_ref, o_ref, acc_ref, c_ref, *, tq, q_off):
    tk = KEY_BLOCK
    rs = min(tq, tk)
    ns = tq // rs
    qi = pl.program_id(2)
    q = (q_ref[0] * (SB_HEAD_DIM ** -0.5)).astype(BF16)
    qpos = q_off + qi * tq + lax.broadcasted_iota(jnp.int32, (tq, tk), 0)
    lane = lax.broadcasted_iota(jnp.int32, (rs, tk), 1)
    jj = lax.broadcasted_iota(jnp.int32, (tk, tk), 0)
    ll = lax.broadcasted_iota(jnp.int32, (tk, tk), 1)
    suffix = jnp.where(jj >= ll, 1.0, 0.0).astype(BF16)
    acc_ref[...] = jnp.zeros_like(acc_ref)
    c_ref[...] = jnp.zeros_like(c_ref)
    diag = [(q_off + qi * tq + (i + 1) * rs - 2) // tk for i in range(ns)]
    never = jnp.int32(2 ** 30)

    def cond(carry):
        s, cmin = carry
        return (s <= diag[ns - 1]) & (cmin < SB_DEAD_MASS)

    def body(carry):
        s, _ = carry
        zs, kpos, vs = [], [], []
        for i in range(ns):
            kb = diag[i] - s
            start = pl.multiple_of(jnp.maximum(kb, 0) * tk, tk)
            k = k_ref[0, pl.ds(start, tk), :].astype(BF16)
            vs.append(v_ref[0, pl.ds(start, tk), :].astype(BF16))
            zs.append(_dot_nt(q[i * rs:(i + 1) * rs, :], k))
            kpos.append(jnp.where(kb >= 0, start, never) + lane)
        z = jnp.concatenate(zs, axis=0) if ns > 1 else zs[0]
        earlier = (jnp.concatenate(kpos, axis=0) if ns > 1 else kpos[0]) < qpos
        u = jnp.where(earlier, _softplus(z), 0.0)
        u_hi = u.astype(BF16)
        u_lo = (u - u_hi.astype(F32)).astype(BF16)
        suf = _dot(u_hi, suffix) + _dot(u_lo, suffix)
        c = c_ref[...]
        w = jnp.where(earlier, jnp.exp(z - suf - c), 0.0).astype(BF16)
        pv = [_dot(w[i * rs:(i + 1) * rs, :], vs[i]) for i in range(ns)]
        acc_ref[...] += jnp.concatenate(pv, axis=0) if ns > 1 else pv[0]
        c_new = c + suf[:, 0:1]
        c_ref[...] = c_new
        return s + 1, jnp.min(c_new)

    lax.while_loop(cond, body, (jnp.int32(0), jnp.float32(0.0)))
    o_ref[0] = acc_ref[...].astype(o_ref.dtype)


def _sb_attention(q_arr, k_arr, v_arr, q_cb, k_cb, v_cb, tq, q_off):
    B, Tq, _ = q_arr.shape
    Tk = k_arr.shape[1]
    return pl.pallas_call(
        functools.partial(_sb_kernel, tq=tq, q_off=q_off),
        grid=(B, SB_HEADS, Tq // tq),
        in_specs=[pl.BlockSpec((1, tq, SB_HEAD_DIM), lambda b, h, i: (b, i, q_cb + h)),
                  pl.BlockSpec((1, Tk, SB_HEAD_DIM), lambda b, h, i: (b, 0, k_cb + h)),
                  pl.BlockSpec((1, Tk, SB_HEAD_DIM), lambda b, h, i: (b, 0, v_cb + h))],
        out_specs=pl.BlockSpec((1, tq, SB_HEAD_DIM), lambda b, h, i: (b, i, h)),
        out_shape=jax.ShapeDtypeStruct((B, Tq, SB_WIDTH), BF16),
        scratch_shapes=[pltpu.VMEM((tq, SB_HEAD_DIM), F32), pltpu.VMEM((tq, 1), F32)],
        compiler_params=_cp(("parallel", "parallel", "arbitrary")),
        name="stick_breaking",
    )(q_arr, k_arr, v_arr)


def _mlstm_kernel(mq_ref, mk_ref, mv_ref, mo_ref, if_ref, c0_ref, n0_ref, m0_ref, g_ref,
                  yc_ref, c_out, n_out, m_out, c_s, n_s, m_s, *, ck):
    tl = mq_ref.shape[1]
    nh = ML_HEADS
    li = pl.program_id(1)

    @pl.when(li == 0)
    def _():
        c_s[...] = c0_ref[0]
        n_s[...] = n0_ref[0]
        m_s[...] = jnp.broadcast_to(m0_ref[0], (nh, 1, LANES))

    t_i = lax.broadcasted_iota(jnp.int32, (nh, ck, ck), 1)
    s_i = lax.broadcasted_iota(jnp.int32, (nh, ck, ck), 2)
    causal = s_i <= t_i
    diag = s_i == t_i
    gain = jnp.stack([g_ref[:, h * ML_V_DIM:(h + 1) * ML_V_DIM] for h in range(nh)])

    def heads(ref, r0, width):
        return jnp.stack([ref[0, pl.ds(r0, ck), h * width:(h + 1) * width] for h in range(nh)])

    def chunk(ci, carry):
        r0 = pl.multiple_of(ci * ck, ck)
        ifb = if_ref[0, pl.ds(r0, ck), :]
        ig_col = jnp.stack([ifb[:, h:h + 1] for h in range(nh)])
        lf_col = -_softplus(-jnp.stack([ifb[:, nh + h:nh + h + 1] for h in range(nh)]))
        q4 = heads(mq_ref, r0, ML_QK_DIM)
        k4 = heads(mk_ref, r0, ML_QK_DIM) * (ML_QK_DIM ** -0.5)
        v4 = heads(mv_ref, r0, ML_V_DIM)
        o4 = heads(mo_ref, r0, ML_V_DIM)
        ig_row = jnp.sum(jnp.where(diag, ig_col, 0.0), axis=1, keepdims=True)
        lf_row = jnp.sum(jnp.where(diag, lf_col, 0.0), axis=1, keepdims=True)
        b_col = jnp.sum(jnp.where(causal, lf_row, 0.0), axis=2, keepdims=True)
        b_row = jnp.sum(jnp.where(t_i <= s_i, lf_col, 0.0), axis=1, keepdims=True)
        m_prev = m_s[:, :, 0:1]
        dmat = jnp.where(causal, b_col - b_row + ig_row, -jnp.inf)
        inter = b_col + m_prev
        m_t = jnp.maximum(inter, jnp.max(dmat, axis=2, keepdims=True))
        s_inter = jnp.exp(inter - m_t)
        qb = q4.astype(BF16)
        kb = k4.astype(BF16)
        vb = v4.astype(BF16)
        wqk = jnp.exp(dmat - m_t) * jnp.stack([_dot_nt(qb[h], kb[h]) for h in range(nh)])
        c_prev = c_s[...]
        n_prev = n_s[...]
        cb = c_prev.astype(BF16)
        wb = wqk.astype(BF16)
        num = (s_inter * jnp.stack([_dot(qb[h], cb[h]) for h in range(nh)])
               + jnp.stack([_dot(wb[h], vb[h]) for h in range(nh)]))
        den = (s_inter * jnp.sum(q4 * n_prev, axis=2, keepdims=True)
               + jnp.sum(wqk, axis=2, keepdims=True))
        hh = num / jnp.maximum(jnp.abs(den), jnp.exp(-m_t))
        b_end = b_col[:, ck - 1:ck, :]
        g_col = b_end - b_col + ig_col
        m_new = jnp.maximum(b_end + m_prev, jnp.max(g_col, axis=1, keepdims=True))
        s_old = jnp.exp(b_end + m_prev - m_new)
        kw = k4 * jnp.exp(g_col - m_new)
        kwb = kw.astype(BF16)
        c_s[...] = s_old * c_prev + jnp.stack([_dot_tn(kwb[h], vb[h]) for h in range(nh)])
        n_s[...] = s_old * n_prev + jnp.sum(kw, axis=1, keepdims=True)
        m_s[...] = jnp.broadcast_to(m_new, (nh, 1, LANES))
        mu = jnp.mean(hh, axis=2, keepdims=True)
        dlt = hh - mu
        var = jnp.mean(dlt * dlt, axis=2, keepdims=True)
        out = (dlt * lax.rsqrt(var + LN_EPS) * gain * _sigmoid(o4)).astype(yc_ref.dtype)
        for h in range(nh):
            yc_ref[0, pl.ds(r0, ck), h * ML_V_DIM:(h + 1) * ML_V_DIM] = out[h]
        return carry

    lax.fori_loop(0, tl // ck, chunk, 0)
    c_out[0] = c_s[...]
    n_out[0] = n_s[...]
    m_out[0] = m_s[:, :, 0:1]


def _mlstm(z3, if3, c0, n0, m0p, g, tl, ck):
    B, S, _ = z3.shape
    return pl.pallas_call(
        functools.partial(_mlstm_kernel, ck=ck),
        grid=(B, S // tl),
        in_specs=[pl.BlockSpec((1, tl, ML_QK_WIDTH), lambda b, l: (b, l, OFF_MQ // ML_QK_WIDTH)),
                  pl.BlockSpec((1, tl, ML_QK_WIDTH), lambda b, l: (b, l, OFF_MK // ML_QK_WIDTH)),
                  pl.BlockSpec((1, tl, ML_V_WIDTH), lambda b, l: (b, l, OFF_MV // ML_V_WIDTH)),
                  pl.BlockSpec((1, tl, ML_V_WIDTH), lambda b, l: (b, l, OFF_MO // ML_V_WIDTH)),
                  pl.BlockSpec((1, tl, LANES), lambda b, l: (b, l, 0)),
                  pl.BlockSpec((1, ML_HEADS, ML_QK_DIM, ML_V_DIM), lambda b, l: (b, 0, 0, 0)),
                  pl.BlockSpec((1, ML_HEADS, 1, ML_QK_DIM), lambda b, l: (b, 0, 0, 0)),
                  pl.BlockSpec((1, ML_HEADS, 1, 1), lambda b, l: (b, 0, 0, 0)),
                  pl.BlockSpec((1, ML_V_WIDTH), lambda b, l: (0, 0))],
        out_specs=[pl.BlockSpec((1, tl, ML_V_WIDTH), lambda b, l: (b, l, 0)),
                   pl.BlockSpec((1, ML_HEADS, ML_QK_DIM, ML_V_DIM), lambda b, l: (b, 0, 0, 0)),
                   pl.BlockSpec((1, ML_HEADS, 1, ML_QK_DIM), lambda b, l: (b, 0, 0, 0)),
                   pl.BlockSpec((1, ML_HEADS, 1, 1), lambda b, l: (b, 0, 0, 0))],
        out_shape=[jax.ShapeDtypeStruct((B, S, ML_V_WIDTH), BF16),
                   jax.ShapeDtypeStruct((B, ML_HEADS, ML_QK_DIM, ML_V_DIM), F32),
                   jax.ShapeDtypeStruct((B, ML_HEADS, 1, ML_QK_DIM), F32),
                   jax.ShapeDtypeStruct((B, ML_HEADS, 1, 1), F32)],
        scratch_shapes=[pltpu.VMEM((ML_HEADS, ML_QK_DIM, ML_V_DIM), F32),
                        pltpu.VMEM((ML_HEADS, 1, ML_QK_DIM), F32),
                        pltpu.VMEM((ML_HEADS, 1, LANES), F32)],
        compiler_params=_cp(("parallel", "arbitrary")),
        name="mlstm",
    )(z3, z3, z3, z3, if3, c0, n0, m0p, g)


def _merge_kernel(ya_ref, yb_ref, yc_ref, g0_ref, g1_ref, g2_ref, x_ref, wpa_ref, wpb_ref, wpc_ref,
                  wo_ref, lg_ref, lb_ref, o_ref, *, alpha):
    mixed = _sigmoid(g0_ref[...]) * _dot(ya_ref[...], wpa_ref[...])
    mixed = mixed + _sigmoid(g1_ref[...]) * _dot(yb_ref[...], wpb_ref[...])
    mixed = mixed + _sigmoid(g2_ref[...]) * _dot(yc_ref[...], wpc_ref[...])
    r = alpha * x_ref[...] + _dot(mixed.astype(BF16), wo_ref[...])
    o_ref[...] = _layer_norm(r, lg_ref[...], lb_ref[...])


def _merge(ya, yb, yc, z, x, wpa, wpb, wpc, wo, lg, lb, tm, alpha):
    T = x.shape[0]
    full = lambda shape: pl.BlockSpec(shape, lambda i: (0,) * len(shape))
    mgb = OFF_MG // D_MODEL
    return pl.pallas_call(
        functools.partial(_merge_kernel, alpha=alpha),
        grid=(T // tm,),
        in_specs=[pl.BlockSpec((tm, D_RNN), lambda i: (i, 0)),
                  pl.BlockSpec((tm, SB_WIDTH), lambda i: (i, 0)),
                  pl.BlockSpec((tm, ML_V_WIDTH), lambda i: (i, 0)),
                  pl.BlockSpec((tm, D_MODEL), lambda i: (i, mgb)),
                  pl.BlockSpec((tm, D_MODEL), lambda i: (i, mgb + 1)),
                  pl.BlockSpec((tm, D_MODEL), lambda i: (i, mgb + 2)),
                  pl.BlockSpec((tm, D_MODEL), lambda i: (i, 0)),
                  full((D_RNN, D_MODEL)), full((SB_WIDTH, D_MODEL)), full((ML_V_WIDTH, D_MODEL)),
                  full((D_MODEL, D_MODEL)), full((1, D_MODEL)), full((1, D_MODEL))],
        out_specs=pl.BlockSpec((tm, D_MODEL), lambda i: (i, 0)),
        out_shape=jax.ShapeDtypeStruct((T, D_MODEL), F32),
        compiler_params=_cp(("parallel",)),
        name="merge_ln1",
    )(ya, yb, yc, z, z, z, x, wpa, wpb, wpc, wo, lg, lb)


def _router_kernel(x_ref, rw_ref, rb_ref, tri_ref, topi_ref, gate_ref, rank_ref, cnt_ref, run_ref):
    @pl.when(pl.program_id(0) == 0)
    def _():
        run_ref[...] = jnp.zeros_like(run_ref)

    tm = x_ref.shape[0]
    l = _dot(x_ref[...].astype(BF16), rw_ref[...]) + rb_ref[...]
    lane = lax.broadcasted_iota(jnp.int32, (tm, LANES), 1)
    lane_f = lane.astype(F32)
    vals, idxs, hots = [], [], []
    for _ in range(TOP_K):
        m = jnp.max(l, axis=1, keepdims=True)
        idx = jnp.min(jnp.where(l == m, lane_f, float(LANES)), axis=1, keepdims=True)
        hot = lane_f == idx
        vals.append(m)
        idxs.append(idx)
        hots.append(hot)
        l = jnp.where(hot, -jnp.inf, l)
    ex = [jnp.exp(v - vals[0]) for v in vals]
    den = ex[0] + ex[1] + ex[2] + ex[3]
    onehot = jnp.zeros((tm, LANES), F32)
    for hot in hots:
        onehot = onehot + jnp.where(hot, 1.0, 0.0)
    before = _dot(tri_ref[...], onehot.astype(BF16)) + run_ref[...]
    topi = jnp.zeros((tm, LANES), F32)
    gate = jnp.zeros((tm, LANES), F32)
    rank = jnp.zeros((tm, LANES), F32)
    for k in range(TOP_K):
        rk = jnp.sum(jnp.where(hots[k], before, 0.0), axis=1, keepdims=True)
        topi = jnp.where(lane == k, idxs[k], topi)
        gate = jnp.where(lane == k, ex[k] / den, gate)
        rank = jnp.where(lane == k, rk, rank)
    run_new = run_ref[...] + jnp.sum(onehot, axis=0, keepdims=True)
    run_ref[...] = run_new
    topi_ref[...] = topi.astype(jnp.int32)
    gate_ref[...] = gate
    rank_ref[...] = rank.astype(jnp.int32)
    cnt_ref[...] = run_new.astype(jnp.int32)


def _router(x, rw, rb, tri, tm):
    T = x.shape[0]
    full = lambda shape: pl.BlockSpec(shape, lambda i: (0,) * len(shape))
    return pl.pallas_call(
        _router_kernel,
        grid=(T // tm,),
        in_specs=[pl.BlockSpec((tm, D_MODEL), lambda i: (i, 0)),
                  full((D_MODEL, LANES)), full((1, LANES)), full((tm, tm))],
        out_specs=[pl.BlockSpec((tm, LANES), lambda i: (i, 0)),
                   pl.BlockSpec((tm, LANES), lambda i: (i, 0)),
                   pl.BlockSpec((tm, LANES), lambda i: (i, 0)),
                   full((1, LANES))],
        out_shape=[jax.ShapeDtypeStruct((T, LANES), jnp.int32),
                   jax.ShapeDtypeStruct((T, LANES), F32),
                   jax.ShapeDtypeStruct((T, LANES), jnp.int32),
                   jax.ShapeDtypeStruct((1, LANES), jnp.int32)],
        scratch_shapes=[pltpu.VMEM((1, LANES), F32)],
        compiler_params=_cp(("arbitrary",)),
        name="router",
    )(x, rw, rb, tri)


def _row_copy(src_ref, src_row, dst_ref, dst_row, sem):
    return pltpu.make_async_copy(src_ref.at[pl.ds(src_row, 1)], dst_ref.at[pl.ds(dst_row, 1)], sem)


def _dispatch_kernel(padlo_ref, padn_ref, dest_ref, x_ref, xs_hbm, zrow, sem, *, ts):
    @pl.when(pl.program_id(0) == 0)
    def _():
        zrow[...] = jnp.zeros_like(zrow)

        def fill(e, c):
            lax.fori_loop(0, padn_ref[e],
                          lambda r, c2: (_row_copy(zrow, 0, xs_hbm, padlo_ref[e] + r, sem).start(), c2)[1], 0)
            return c

        def fill_done(e, c):
            lax.fori_loop(0, padn_ref[e],
                          lambda r, c2: (_row_copy(zrow, 0, xs_hbm, 0, sem).wait(), c2)[1], 0)
            return c

        lax.fori_loop(0, N_EXPERTS, fill, 0)
        lax.fori_loop(0, N_EXPERTS, fill_done, 0)

    def issue(t, c):
        for k in range(TOP_K):
            _row_copy(x_ref, t, xs_hbm, dest_ref[TOP_K * t + k], sem).start()
        return c

    lax.fori_loop(0, ts, issue, 0)

    def drain(t, c):
        for k in range(TOP_K):
            _row_copy(x_ref, 0, xs_hbm, 0, sem).wait()
        return c

    lax.fori_loop(0, ts, drain, 0)


def _dispatch(pad_lo, pad_n, dest_flat, x, rows, ts):
    T = x.shape[0]
    grid_spec = pltpu.PrefetchScalarGridSpec(
        num_scalar_prefetch=2,
        grid=(T // ts,),
        in_specs=[pl.BlockSpec((TOP_K * ts,), lambda i, lo, n: (i,), memory_space=pltpu.SMEM),
                  pl.BlockSpec((ts, D_MODEL), lambda i, lo, n: (i, 0))],
        out_specs=pl.BlockSpec(memory_space=pl.ANY),
        scratch_shapes=[pltpu.VMEM((8, D_MODEL), F32), pltpu.SemaphoreType.DMA(())],
    )
    return pl.pallas_call(
        functools.partial(_dispatch_kernel, ts=ts),
        grid_spec=grid_spec,
        out_shape=jax.ShapeDtypeStruct((rows, D_MODEL), F32),
        compiler_params=_cp(("arbitrary",)),
        name="moe_dispatch",
    )(pad_lo, pad_n, dest_flat, x)


def _experts_kernel(be_ref, nu_ref, xs_ref, w1_ref, b1_ref, w2_ref, b2_ref, y_ref):
    @pl.when(pl.program_id(0) < nu_ref[0])
    def _():
        gu = _dot(xs_ref[...].astype(BF16), w1_ref[0]) + b1_ref[0]
        g_ = jnp.minimum(gu[:, :D_FF], SWIGLU_LIMIT)
        up = jnp.clip(gu[:, D_FF:], -SWIGLU_LIMIT, SWIGLU_LIMIT)
        act = (up + 1.0) * g_ * _sigmoid(SWIGLU_ALPHA * g_)
        y_ref[...] = _dot(act.astype(BF16), w2_ref[0]) + b2_ref[0]

    @pl.when(pl.program_id(0) >= nu_ref[0])
    def _():
        y_ref[...] = jnp.zeros_like(y_ref)


def _experts(block_e, n_used, xs, w1, b1, w2, b2, m):
    R = xs.shape[0]
    grid_spec = pltpu.PrefetchScalarGridSpec(
        num_scalar_prefetch=2,
        grid=(R // m,),
        in_specs=[pl.BlockSpec((m, D_MODEL), lambda j, be, nu: (jnp.minimum(j, nu[0] - 1), 0)),
                  pl.BlockSpec((1, D_MODEL, 2 * D_FF), lambda j, be, nu: (be[j], 0, 0)),
                  pl.BlockSpec((1, 1, 2 * D_FF), lambda j, be, nu: (be[j], 0, 0)),
                  pl.BlockSpec((1, D_FF, D_MODEL), lambda j, be, nu: (be[j], 0, 0)),
                  pl.BlockSpec((1, 1, D_MODEL), lambda j, be, nu: (be[j], 0, 0))],
        out_specs=pl.BlockSpec((m, D_MODEL), lambda j, be, nu: (j, 0)),
    )
    return pl.pallas_call(
        _experts_kernel,
        grid_spec=grid_spec,
        out_shape=jax.ShapeDtypeStruct((R, D_MODEL), F32),
        compiler_params=_cp(("arbitrary",)),
        name="moe_experts",
    )(block_e, n_used, xs, w1, b1, w2, b2)


def _combine_kernel(dest_ref, x_ref, p_ref, gate_ref, y_hbm, wp_ref, wg_ref, bg_ref,
                    lg_ref, lb_ref, o_ref, ybuf, sem, *, alpha):
    tm = x_ref.shape[0]

    def issue(t, c):
        for k in range(TOP_K):
            _row_copy(y_hbm, dest_ref[TOP_K * t + k], ybuf.at[k], t, sem).start()
        return c

    lax.fori_loop(0, tm, issue, 0)

    x = x_ref[...]
    ple = _sigmoid(_dot(x.astype(BF16), wg_ref[...]) + bg_ref[...]) * _dot(p_ref[...].astype(BF16), wp_ref[...])
    r = alpha * x + ple

    def drain(t, c):
        for k in range(TOP_K):
            _row_copy(y_hbm, 0, ybuf.at[k], 0, sem).wait()
        return c

    lax.fori_loop(0, tm, drain, 0)
    for k in range(TOP_K):
        r = r + gate_ref[:, k:k + 1] * ybuf[k]
    o_ref[...] = _layer_norm(r, lg_ref[...], lb_ref[...])


def _combine(dest_flat, x, p, gate, y, wp, wg, bg, lg, lb, tm, alpha):
    T = x.shape[0]
    full = lambda shape: pl.BlockSpec(shape, lambda i: (0,) * len(shape))
    return pl.pallas_call(
        functools.partial(_combine_kernel, alpha=alpha),
        grid=(T // tm,),
        in_specs=[pl.BlockSpec((TOP_K * tm,), lambda i: (i,), memory_space=pltpu.SMEM),
                  pl.BlockSpec((tm, D_MODEL), lambda i: (i, 0)),
                  pl.BlockSpec((tm, D_PLE), lambda i: (i, 0)),
                  pl.BlockSpec((tm, LANES), lambda i: (i, 0)),
                  pl.BlockSpec(memory_space=pl.ANY),
                  full((D_PLE, D_MODEL)), full((D_MODEL, D_MODEL)), full((1, D_MODEL)),
                  full((1, D_MODEL)), full((1, D_MODEL))],
        out_specs=pl.BlockSpec((tm, D_MODEL), lambda i: (i, 0)),
        out_shape=jax.ShapeDtypeStruct((T, D_MODEL), F32),
        scratch_shapes=[pltpu.VMEM((TOP_K, tm, D_MODEL), F32), pltpu.SemaphoreType.DMA(())],
        compiler_params=_cp(("arbitrary",)),
        name="moe_combine_ln2",
    )(dest_flat, x, p, gate, y, wp, wg, bg, lg, lb)


def _pick(n, pref):
    t = min(n, pref)
    while n % t:
        t //= 2
    return t


def _layer(x, p, lw, conv_buf, lru_h, ml_c, ml_n, ml_m, k_past, v_past, prompt, alpha):
    B, L, _ = x.shape
    T = B * L
    xf = x.reshape(T, D_MODEL)

    z = _inproj(xf, lw["w_in"], lw["b_in"], _pick(T, 1024), 1024)
    k_new, v_new, ifg = _kvif(xf, lw["w_kvif"], lw["b_kvif"], _pick(T, 1024))
    z3 = z.reshape(B, L, N_MAIN)
    k_new = k_new.reshape(B, L, SB_WIDTH)
    v_new = v_new.reshape(B, L, SB_WIDTH)
    conv_new = z3[:, L - (CONV_W - 1):, OFF_AX:OFF_AX + D_RNN]

    cbuf8 = jnp.concatenate([jnp.zeros((B, 8 - (CONV_W - 1), D_RNN), F32), conv_buf.astype(F32)], axis=1)
    ya, lru_new = _lru(z3, cbuf8, lru_h.reshape(B, 1, D_RNN).astype(F32), lw["conv_w8"], lw["conv_b"],
                       lw["wr"], lw["br"], lw["wi"], lw["bi"], lw["lam"], _pick(L, 256), prompt)

    if prompt:
        yb = _sb_attention(z3, k_new, v_new, OFF_Q // SB_HEAD_DIM, 0, 0, _pick(L, 1024), 0)
    else:
        P = k_past.shape[1]
        pad = (-(P + L)) % KEY_BLOCK
        zpad = jnp.zeros((B, pad, SB_WIDTH), F32)
        k_all = jnp.concatenate([k_past.reshape(B, P, SB_WIDTH), k_new, zpad], axis=1)
        v_all = jnp.concatenate([v_past.reshape(B, P, SB_WIDTH), v_new, zpad], axis=1)
        yb = _sb_attention(z3, k_all, v_all, OFF_Q // SB_HEAD_DIM, 0, 0, L, P)

    ck = _pick(L, 128)
    yc, c_new, n_new, m_new = _mlstm(z3, ifg.reshape(B, L, LANES), ml_c.astype(F32),
                                     ml_n.astype(F32).reshape(B, ML_HEADS, 1, ML_QK_DIM),
                                     ml_m.astype(F32).reshape(B, ML_HEADS, 1, 1),
                                     lw["ml_g"], _pick(L, 512), ck)
    n_new = n_new.reshape(B, ML_HEADS, ML_QK_DIM)
    m_new = m_new.reshape(B, ML_HEADS)

    x1 = _merge(ya.reshape(T, D_RNN), yb.reshape(T, SB_WIDTH), yc.reshape(T, ML_V_WIDTH), z, xf,
                lw["w_pa"], lw["w_pb"], lw["w_pc"], lw["w_out"], lw["ln1_g"], lw["ln1_b"],
                _pick(T, 512), alpha)

    tr = _pick(T, 512)
    tri = jnp.tril(jnp.ones((tr, tr), BF16), -1)
    topi, gate, rank, cnt = _router(x1, lw["router_w"], lw["router_b"], tri, tr)
    TK = T * TOP_K
    m_rows = max(16, min(512, TK // N_EXPERTS))
    nb = TK // m_rows + N_EXPERTS
    counts = cnt[0, :N_EXPERTS]
    padded = (counts + (m_rows - 1)) // m_rows * m_rows
    pend = jnp.cumsum(padded)
    off = (pend - padded).astype(jnp.int32)
    blk_start = jnp.arange(nb, dtype=jnp.int32) * m_rows
    block_e = jnp.minimum(jnp.sum((pend[None, :] <= blk_start[:, None]).astype(jnp.int32), axis=1),
                          N_EXPERTS - 1).astype(jnp.int32)
    n_used = (pend[-1:] // m_rows).astype(jnp.int32)
    experts = jnp.arange(N_EXPERTS, dtype=jnp.int32)
    top4 = topi[:, :TOP_K]
    dest = rank[:, :TOP_K] + jnp.sum(jnp.where(top4[:, :, None] == experts, off, 0), axis=-1)
    dest_flat = dest.reshape(TK).astype(jnp.int32)
    xs = _dispatch(off + counts, (padded - counts).astype(jnp.int32), dest_flat, x1, nb * m_rows,
                   _pick(T, 256))
    y = _experts(block_e, n_used, xs, lw["exp_w1"], lw["exp_b1"], lw["exp_w2"], lw["exp_b2"], m_rows)
    x2 = _combine(dest_flat, x1, p.reshape(T, D_PLE), gate, y, lw["ple_w"], lw["ple_gate_w"],
                  lw["ple_gate_b"], lw["ln2_g"], lw["ln2_b"], _pick(T, 256), alpha)

    k_out = k_new.reshape(B, L, SB_HEADS, SB_HEAD_DIM)
    v_out = v_new.reshape(B, L, SB_HEADS, SB_HEAD_DIM)
    return x2.reshape(B, L, D_MODEL), (k_out, v_out, conv_new, lru_new.reshape(B, D_RNN), c_new, n_new, m_new)


def _block_diag_groups(w):
    per = LRU_GROUP // LRU_BLOCK
    w4 = w.reshape(D_RNN // LRU_GROUP, per, LRU_BLOCK, LRU_BLOCK)
    eye = jnp.eye(per, dtype=w.dtype)
    return jnp.einsum("gacd,ab->gacbd", w4, eye).reshape(D_RNN // LRU_GROUP, LRU_GROUP, LRU_GROUP)


def _split_in(w):
    main = jnp.concatenate([w[..., :ORIG_K], w[..., ORIG_MQ:ORIG_IF], w[..., ORIG_MG:]], axis=-1)
    pad = jnp.zeros(w.shape[:-1] + (LANES - 2 * ML_HEADS,), w.dtype)
    kvif = jnp.concatenate([w[..., ORIG_K:ORIG_MQ], w[..., ORIG_IF:ORIG_MG], pad], axis=-1)
    return main, kvif


def kernel(x_prompt, x_sample, cache_sb_k, cache_sb_v, state_conv, state_lru, state_mlstm_c, state_mlstm_n, state_mlstm_m, p_prompt, p_sample, w_in, b_in, conv_w, conv_b, lru_wr, lru_br, lru_wi, lru_bi, lru_lambda, ml_norm_g, w_pa, w_pb, w_pc, w_out, ln1_g, ln1_b, router_w, router_b, exp_w1, exp_b1, exp_w2, exp_b2, ple_w, ple_gate_w, ple_gate_b, ln2_g, ln2_b):
    depth = w_in.shape[0]
    alpha = (2 * depth) ** 0.25
    B = x_prompt.shape[0]
    y_prompt, y_sample = x_prompt, x_sample
    st_p, st_s = [], []
    for i in range(depth):
        row = lambda a: a[i].reshape(1, -1).astype(F32)
        w_main, w_kvif = _split_in(w_in[i])
        b_main, b_kvif = _split_in(b_in[i])
        lw = dict(
            w_in=w_main.astype(BF16), b_in=b_main.reshape(1, -1).astype(F32),
            w_kvif=w_kvif.astype(BF16), b_kvif=b_kvif.reshape(1, -1).astype(F32),
            conv_w8=jnp.pad(conv_w[i].astype(F32), ((0, 8 - CONV_W), (0, 0))), conv_b=row(conv_b),
            wr=_block_diag_groups(lru_wr[i]).astype(BF16), br=row(lru_br),
            wi=_block_diag_groups(lru_wi[i]).astype(BF16), bi=row(lru_bi), lam=row(lru_lambda),
            ml_g=row(ml_norm_g),
            w_pa=w_pa[i].astype(BF16), w_pb=w_pb[i].astype(BF16), w_pc=w_pc[i].astype(BF16),
            w_out=w_out[i].astype(BF16), ln1_g=row(ln1_g), ln1_b=row(ln1_b),
            router_w=jnp.pad(router_w[i], ((0, 0), (0, LANES - N_EXPERTS))).astype(BF16),
            router_b=jnp.pad(router_b[i].astype(F32), (0, LANES - N_EXPERTS),
                             constant_values=-1e30).reshape(1, LANES),
            exp_w1=exp_w1[i].astype(BF16), exp_b1=exp_b1[i].reshape(N_EXPERTS, 1, 2 * D_FF).astype(F32),
            exp_w2=exp_w2[i].astype(BF16), exp_b2=exp_b2[i].reshape(N_EXPERTS, 1, D_MODEL).astype(F32),
            ple_w=ple_w[i].astype(BF16), ple_gate_w=ple_gate_w[i].astype(BF16), ple_gate_b=row(ple_gate_b),
            ln2_g=row(ln2_g), ln2_b=row(ln2_b),
        )
        y_prompt, sp = _layer(y_prompt, p_prompt[i], lw,
                              jnp.zeros((B, CONV_W - 1, D_RNN), F32), jnp.zeros((B, D_RNN), F32),
                              jnp.zeros((B, ML_HEADS, ML_QK_DIM, ML_V_DIM), F32),
                              jnp.zeros((B, ML_HEADS, ML_QK_DIM), F32), jnp.zeros((B, ML_HEADS), F32),
                              None, None, True, alpha)
        y_sample, ss = _layer(y_sample, p_sample[i], lw, state_conv[i], state_lru[i], state_mlstm_c[i],
                              state_mlstm_n[i], state_mlstm_m[i], cache_sb_k[i], cache_sb_v[i], False, alpha)
        st_p.append(sp)
        st_s.append(ss)
    k_p, v_p, conv_p, lru_p, c_p, n_p, m_p = [jnp.stack(s) for s in zip(*st_p)]
    k_s, v_s, conv_s, lru_s, c_s, n_s, m_s = [jnp.stack(s) for s in zip(*st_s)]
    return (y_prompt, y_sample, k_p, v_p, conv_p, lru_p, c_p, n_p, m_p,
            k_s, v_s, conv_s, lru_s, c_s, n_s, m_s)
```

```python
import functools

import jax
import jax.numpy as jnp
from jax import lax
from jax.experimental import pallas as pl
from jax.experimental.pallas import tpu as pltpu

F32 = jnp.float32
BF16 = jnp.bfloat16

D_MODEL = 1024
D_RNN = 1024
N_LRU_BLOCKS = 16
LRU_BLOCK = D_RNN // N_LRU_BLOCKS
LRU_GROUP = 256
CONV_W = 4
LRU_C = 8.0
SB_HEADS = 4
SB_HEAD_DIM = 128
SB_WIDTH = SB_HEADS * SB_HEAD_DIM
ML_HEADS = 4
ML_QK_DIM = 64
ML_V_DIM = 128
ML_QK_WIDTH = ML_HEADS * ML_QK_DIM
ML_V_WIDTH = ML_HEADS * ML_V_DIM
N_BRANCH = 3
N_EXPERTS = 32
TOP_K = 4
D_FF = 512
SWIGLU_LIMIT = 7.0
SWIGLU_ALPHA = 1.702
D_PLE = 256
LN_EPS = 1e-5
LANES = 128
KEY_BLOCK = 128

OFF_AX, OFF_AG = 0, 1024
OFF_Q = 2048
OFF_MQ, OFF_MK, OFF_MV, OFF_MO = 2560, 2816, 3072, 3584
OFF_MG = 4096
N_MAIN = 7168
N_KVIF = 2 * SB_WIDTH + LANES
ORIG_K, ORIG_MQ, ORIG_IF, ORIG_MG = 2560, 3584, 5120, 5128
SB_DEAD_MASS = 88.0
VMEM_LIMIT = 56 * 1024 * 1024


def _cp(sem, vmem=VMEM_LIMIT):
    return pltpu.CompilerParams(dimension_semantics=sem, vmem_limit_bytes=vmem)


def _dot(a, b):
    return jnp.dot(a, b, preferred_element_type=F32)


def _dot_nt(a, b):
    return lax.dot_general(a, b, (((1,), (1,)), ((), ())), preferred_element_type=F32)


def _dot_tn(a, b):
    return lax.dot_general(a, b, (((0,), (0,)), ((), ())), preferred_element_type=F32)


def _sigmoid(x):
    return 0.5 * jnp.tanh(0.5 * x) + 0.5


def _softplus(x):
    return jnp.maximum(x, 0.0) + jnp.log1p(jnp.exp(-jnp.abs(x)))


def _layer_norm(r, g, b):
    mu = jnp.mean(r, axis=-1, keepdims=True)
    d = r - mu
    var = jnp.mean(d * d, axis=-1, keepdims=True)
    return d * lax.rsqrt(var + LN_EPS) * g + b


def _inproj_kernel(x_ref, w_ref, b_ref, o_ref, xb_ref):
    @pl.when(pl.program_id(1) == 0)
    def _():
        xb_ref[...] = x_ref[...].astype(BF16)

    o_ref[...] = _dot(xb_ref[...], w_ref[...]) + b_ref[...]


def _inproj(x, w, b, tm, tn):
    T, D = x.shape
    N = w.shape[1]
    return pl.pallas_call(
        _inproj_kernel,
        grid=(T // tm, N // tn),
        in_specs=[pl.BlockSpec((tm, D), lambda i, j: (i, 0)),
                  pl.BlockSpec((D, tn), lambda i, j: (0, j)),
                  pl.BlockSpec((1, tn), lambda i, j: (0, j))],
        out_specs=pl.BlockSpec((tm, tn), lambda i, j: (i, j)),
        out_shape=jax.ShapeDtypeStruct((T, N), F32),
        scratch_shapes=[pltpu.VMEM((tm, D), BF16)],
        compiler_params=_cp(("parallel", "arbitrary")),
        name="inproj",
    )(x, w, b)


def _kvif_kernel(x_ref, w_ref, b_ref, k_ref, v_ref, if_ref):
    r = _dot(x_ref[...].astype(BF16), w_ref[...]) + b_ref[...]
    k_ref[...] = r[:, :SB_WIDTH]
    v_ref[...] = r[:, SB_WIDTH:2 * SB_WIDTH]
    if_ref[...] = r[:, 2 * SB_WIDTH:]


def _kvif(x, w, b, tm):
    T, D = x.shape
    return pl.pallas_call(
        _kvif_kernel,
        grid=(T // tm,),
        in_specs=[pl.BlockSpec((tm, D), lambda i: (i, 0)),
                  pl.BlockSpec((D, N_KVIF), lambda i: (0, 0)),
                  pl.BlockSpec((1, N_KVIF), lambda i: (0, 0))],
        out_specs=[pl.BlockSpec((tm, SB_WIDTH), lambda i: (i, 0)),
                   pl.BlockSpec((tm, SB_WIDTH), lambda i: (i, 0)),
                   pl.BlockSpec((tm, LANES), lambda i: (i, 0))],
        out_shape=[jax.ShapeDtypeStruct((T, SB_WIDTH), F32),
                   jax.ShapeDtypeStruct((T, SB_WIDTH), F32),
                   jax.ShapeDtypeStruct((T, LANES), F32)],
        compiler_params=_cp(("parallel",)),
        name="inproj_kvif",
    )(x, w, b)


def _lru_kernel(ax_ref, ag_ref, cbuf_ref, h0_ref, cw_ref, cb_ref, wr_ref, br_ref, wi_ref, bi_ref,
                lam_ref, ya_ref, hl_ref, ext_ref, hc_ref, *, first_frame):
    tl = ax_ref.shape[1]
    li = pl.program_id(1)

    @pl.when(li == 0)
    def _():
        ext_ref[0:8, :] = cbuf_ref[0]
        hc_ref[...] = h0_ref[0]

    ext_ref[8:8 + tl, :] = ax_ref[0]
    ua = cb_ref[...] + ext_ref[5:5 + tl, :] * cw_ref[0:1, :]
    for j in range(1, CONV_W):
        ua = ua + ext_ref[5 + j:5 + j + tl, :] * cw_ref[j:j + 1, :]
    ext_ref[5:8, :] = ext_ref[5 + tl:8 + tl, :]

    uab = ua.astype(BF16)
    ng = D_RNN // LRU_GROUP
    r_pre = jnp.concatenate(
        [_dot(uab[:, g * LRU_GROUP:(g + 1) * LRU_GROUP], wr_ref[g]) for g in range(ng)], axis=1)
    i_pre = jnp.concatenate(
        [_dot(uab[:, g * LRU_GROUP:(g + 1) * LRU_GROUP], wi_ref[g]) for g in range(ng)], axis=1)
    r = _sigmoid(r_pre + br_ref[...])
    ig = _sigmoid(i_pre + bi_ref[...])
    log_a = (-LRU_C) * r * _softplus(-lam_ref[...])
    a = jnp.exp(log_a)
    th = jnp.tanh(log_a)
    mult = jnp.sqrt(-2.0 * th / (1.0 - th))
    row = lax.broadcasted_iota(jnp.int32, (tl, D_RNN), 0)
    if first_frame:
        mult = jnp.where((row == 0) & (li == 0), 1.0, mult)
    bx = mult * (ig * ua)

    a = a.reshape(tl // 8, 8, D_RNN)
    bx = bx.reshape(tl // 8, 8, D_RNN)
    sub = lax.broadcasted_iota(jnp.int32, (tl // 8, 8, D_RNN), 1)
    for d in (1, 2, 4):
        a_sh = jnp.where(sub < d, 1.0, pltpu.roll(a, d, 1))
        b_sh = jnp.where(sub < d, 0.0, pltpu.roll(bx, d, 1))
        bx = a * b_sh + bx
        a = a * a_sh
    carry = hc_ref[...]
    groups = []
    for g in range(tl // 8):
        hg = bx[g] + a[g] * carry
        carry = hg[7:8, :]
        groups.append(hg)
    h = jnp.concatenate(groups, axis=0)
    hc_ref[...] = carry
    hl_ref[0] = carry

    gx = ag_ref[0]
    gelu = 0.5 * gx * (1.0 + jnp.tanh(0.7978845608028654 * (gx + 0.044715 * (gx * gx * gx))))
    ya_ref[0] = (h * gelu).astype(ya_ref.dtype)


def _lru(z3, cbuf8, h0, cw8, cb, wr, br, wi, bi, lam, tl, first_frame):
    B, S, _ = z3.shape
    full = lambda shape: pl.BlockSpec(shape, lambda b, l: (0,) * len(shape))
    return pl.pallas_call(
        functools.partial(_lru_kernel, first_frame=first_frame),
        grid=(B, S // tl),
        in_specs=[pl.BlockSpec((1, tl, D_RNN), lambda b, l: (b, l, OFF_AX // D_RNN)),
                  pl.BlockSpec((1, tl, D_RNN), lambda b, l: (b, l, OFF_AG // D_RNN)),
                  pl.BlockSpec((1, 8, D_RNN), lambda b, l: (b, 0, 0)),
                  pl.BlockSpec((1, 1, D_RNN), lambda b, l: (b, 0, 0)),
                  full((8, D_RNN)), full((1, D_RNN)),
                  full((D_RNN // LRU_GROUP, LRU_GROUP, LRU_GROUP)), full((1, D_RNN)),
                  full((D_RNN // LRU_GROUP, LRU_GROUP, LRU_GROUP)), full((1, D_RNN)),
                  full((1, D_RNN))],
        out_specs=[pl.BlockSpec((1, tl, D_RNN), lambda b, l: (b, l, 0)),
                   pl.BlockSpec((1, 1, D_RNN), lambda b, l: (b, 0, 0))],
        out_shape=[jax.ShapeDtypeStruct((B, S, D_RNN), BF16),
                   jax.ShapeDtypeStruct((B, 1, D_RNN), F32)],
        scratch_shapes=[pltpu.VMEM((tl + 8, D_RNN), F32), pltpu.VMEM((1, D_RNN), F32)],
        compiler_params=_cp(("parallel", "arbitrary")),
        name="conv_rglru",
    )(z3, z3, cbuf8, h0, cw8, cb, wr, br, wi, bi, lam)


def _sb_kernel(q_ref, k_ref, v_ref, o_ref, acc_ref, c_ref, *, tq, q_off):
    tk = KEY_BLOCK
    rs = min(tq, tk)
    ns = tq // rs
    qi = pl.program_id(2)
    q = (q_ref[0] * (SB_HEAD_DIM ** -0.5)).astype(BF16)
    qpos = q_off + qi * tq + lax.broadcasted_iota(jnp.int32, (tq, tk), 0)
    lane = lax.broadcasted_iota(jnp.int32, (rs, tk), 1)
    jj = lax.broadcasted_iota(jnp.int32, (tk, tk), 0)
    ll = lax.broadcasted_iota(jnp.int32, (tk, tk), 1)
    suffix = jnp.where(jj >= ll, 1.0, 0.0).astype(BF16)
    acc_ref[...] = jnp.zeros_like(acc_ref)
    c_ref[...] = jnp.zeros_like(c_ref)
    diag = [(q_off + qi * tq + (i + 1) * rs - 2) // tk for i in range(ns)]
    never = jnp.int32(2 ** 30)
    group = lax.broadcasted_iota(jnp.int32, (tq, 1), 0) // rs
    diag_row = jnp.zeros((tq, 1), jnp.int32)
    for i in range(ns):
        diag_row = jnp.where(group == i, diag[i], diag_row)

    def cond(carry):
        s, cmin = carry
        return (s <= diag[ns - 1]) & (cmin < SB_DEAD_MASS)

    def body(carry):
        s, _ = carry
        zs, kpos, vs = [], [], []
        for i in range(ns):
            kb = diag[i] - s
            start = pl.multiple_of(jnp.maximum(kb, 0) * tk, tk)
            k = k_ref[0, pl.ds(start, tk), :].astype(BF16)
            vs.append(v_ref[0, pl.ds(start, tk), :].astype(BF16))
            zs.append(_dot_nt(q[i * rs:(i + 1) * rs, :], k))
            kpos.append(jnp.where(kb >= 0, start, never) + lane)
        z = jnp.concatenate(zs, axis=0) if ns > 1 else zs[0]
        earlier = (jnp.concatenate(kpos, axis=0) if ns > 1 else kpos[0]) < qpos
        u = jnp.where(earlier, _softplus(z), 0.0)
        u_hi = u.astype(BF16)
        u_lo = (u - u_hi.astype(F32)).astype(BF16)
        suf = _dot(u_hi, suffix) + _dot(u_lo, suffix)
        c = c_ref[...]
        w = jnp.where(earlier, jnp.exp(z - suf - c), 0.0).astype(BF16)
        pv = [_dot(w[i * rs:(i + 1) * rs, :], vs[i]) for i in range(ns)]
        acc_ref[...] += jnp.concatenate(pv, axis=0) if ns > 1 else pv[0]
        c_new = c + suf[:, 0:1]
        c_ref[...] = c_new
        return s + 1, jnp.min(jnp.where(diag_row > s, c_new, jnp.inf))

    lax.while_loop(cond, body, (jnp.int32(0), jnp.float32(0.0)))
    o_ref[0] = acc_ref[...].astype(o_ref.dtype)


def _sb_attention(q_arr, k_arr, v_arr, q_cb, k_cb, v_cb, tq, q_off):
    B, Tq, _ = q_arr.shape
    Tk = k_arr.shape[1]
    return pl.pallas_call(
        functools.partial(_sb_kernel, tq=tq, q_off=q_off),
        grid=(B, SB_HEADS, Tq // tq),
        in_specs=[pl.BlockSpec((1, tq, SB_HEAD_DIM), lambda b, h, i: (b, i, q_cb + h)),
                  pl.BlockSpec((1, Tk, SB_HEAD_DIM), lambda b, h, i: (b, 0, k_cb + h)),
                  pl.BlockSpec((1, Tk, SB_HEAD_DIM), lambda b, h, i: (b, 0, v_cb + h))],
        out_specs=pl.BlockSpec((1, tq, SB_HEAD_DIM), lambda b, h, i: (b, i, h)),
        out_shape=jax.ShapeDtypeStruct((B, Tq, SB_WIDTH), BF16),
        scratch_shapes=[pltpu.VMEM((tq, SB_HEAD_DIM), F32), pltpu.VMEM((tq, 1), F32)],
        compiler_params=_cp(("parallel", "parallel", "arbitrary")),
        name="stick_breaking",
    )(q_arr, k_arr, v_arr)


def _mlstm_kernel(mq_ref, mk_ref, mv_ref, mo_ref, if_ref, c0_ref, n0_ref, m0_ref, g_ref,
                  yc_ref, c_out, n_out, m_out, c_s, n_s, m_s, *, ck):
    tl = mq_ref.shape[1]
    nh = ML_HEADS
    li = pl.program_id(1)

    @pl.when(li == 0)
    def _():
        c_s[...] = c0_ref[0]
        n_s[...] = n0_ref[0]
        m_s[...] = jnp.broadcast_to(m0_ref[0], (nh, 1, LANES))

    t_i = lax.broadcasted_iota(jnp.int32, (nh, ck, ck), 1)
    s_i = lax.broadcasted_iota(jnp.int32, (nh, ck, ck), 2)
    causal = s_i <= t_i
    diag = s_i == t_i
    gain = jnp.stack([g_ref[:, h * ML_V_DIM:(h + 1) * ML_V_DIM] for h in range(nh)])

    def heads(ref, r0, width):
        return jnp.stack([ref[0, pl.ds(r0, ck), h * width:(h + 1) * width] for h in range(nh)])

    def chunk(ci, carry):
        r0 = pl.multiple_of(ci * ck, ck)
        ifb = if_ref[0, pl.ds(r0, ck), :]
        ig_col = jnp.stack([ifb[:, h:h + 1] for h in range(nh)])
        lf_col = -_softplus(-jnp.stack([ifb[:, nh + h:nh + h + 1] for h in range(nh)]))
        q4 = heads(mq_ref, r0, ML_QK_DIM)
        k4 = heads(mk_ref, r0, ML_QK_DIM) * (ML_QK_DIM ** -0.5)
        v4 = heads(mv_ref, r0, ML_V_DIM)
        o4 = heads(mo_ref, r0, ML_V_DIM)
        ig_row = jnp.sum(jnp.where(diag, ig_col, 0.0), axis=1, keepdims=True)
        lf_row = jnp.sum(jnp.where(diag, lf_col, 0.0), axis=1, keepdims=True)
        b_col = jnp.sum(jnp.where(causal, lf_row, 0.0), axis=2, keepdims=True)
        b_row = jnp.sum(jnp.where(t_i <= s_i, lf_col, 0.0), axis=1, keepdims=True)
        m_prev = m_s[:, :, 0:1]
        dmat = jnp.where(causal, b_col - b_row + ig_row, -jnp.inf)
        inter = b_col + m_prev
        m_t = jnp.maximum(inter, jnp.max(dmat, axis=2, keepdims=True))
        s_inter = jnp.exp(inter - m_t)
        qb = q4.astype(BF16)
        kb = k4.astype(BF16)
        vb = v4.astype(BF16)
        wqk = jnp.exp(dmat - m_t) * jnp.stack([_dot_nt(qb[h], kb[h]) for h in range(nh)])
        c_prev = c_s[...]
        n_prev = n_s[...]
        cb = c_prev.astype(BF16)
        wb = wqk.astype(BF16)
        num = (s_inter * jnp.stack([_dot(qb[h], cb[h]) for h in range(nh)])
               + jnp.stack([_dot(wb[h], vb[h]) for h in range(nh)]))
        den = (s_inter * jnp.sum(q4 * n_prev, axis=2, keepdims=True)
               + jnp.sum(wqk, axis=2, keepdims=True))
        hh = num / jnp.maximum(jnp.abs(den), jnp.exp(-m_t))
        b_end = b_col[:, ck - 1:ck, :]
        g_col = b_end - b_col + ig_col
        m_new = jnp.maximum(b_end + m_prev, jnp.max(g_col, axis=1, keepdims=True))
        s_old = jnp.exp(b_end + m_prev - m_new)
        kw = k4 * jnp.exp(g_col - m_new)
        kwb = kw.astype(BF16)
        c_s[...] = s_old * c_prev + jnp.stack([_dot_tn(kwb[h], vb[h]) for h in range(nh)])
        n_s[...] = s_old * n_prev + jnp.sum(kw, axis=1, keepdims=True)
        m_s[...] = jnp.broadcast_to(m_new, (nh, 1, LANES))
        mu = jnp.mean(hh, axis=2, keepdims=True)
        dlt = hh - mu
        var = jnp.mean(dlt * dlt, axis=2, keepdims=True)
        out = (dlt * lax.rsqrt(var + LN_EPS) * gain * _sigmoid(o4)).astype(yc_ref.dtype)
        for h in range(nh):
            yc_ref[0, pl.ds(r0, ck), h * ML_V_DIM:(h + 1) * ML_V_DIM] = out[h]
        return carry

    lax.fori_loop(0, tl // ck, chunk, 0)
    c_out[0] = c_s[...]
    n_out[0] = n_s[...]
    m_out[0] = m_s[:, :, 0:1]


def _mlstm(z3, if3, c0, n0, m0p, g, tl, ck):
    B, S, _ = z3.shape
    return pl.pallas_call(
        functools.partial(_mlstm_kernel, ck=ck),
        grid=(B, S // tl),
        in_specs=[pl.BlockSpec((1, tl, ML_QK_WIDTH), lambda b, l: (b, l, OFF_MQ // ML_QK_WIDTH)),
                  pl.BlockSpec((1, tl, ML_QK_WIDTH), lambda b, l: (b, l, OFF_MK // ML_QK_WIDTH)),
                  pl.BlockSpec((1, tl, ML_V_WIDTH), lambda b, l: (b, l, OFF_MV // ML_V_WIDTH)),
                  pl.BlockSpec((1, tl, ML_V_WIDTH), lambda b, l: (b, l, OFF_MO // ML_V_WIDTH)),
                  pl.BlockSpec((1, tl, LANES), lambda b, l: (b, l, 0)),
                  pl.BlockSpec((1, ML_HEADS, ML_QK_DIM, ML_V_DIM), lambda b, l: (b, 0, 0, 0)),
                  pl.BlockSpec((1, ML_HEADS, 1, ML_QK_DIM), lambda b, l: (b, 0, 0, 0)),
                  pl.BlockSpec((1, ML_HEADS, 1, 1), lambda b, l: (b, 0, 0, 0)),
                  pl.BlockSpec((1, ML_V_WIDTH), lambda b, l: (0, 0))],
        out_specs=[pl.BlockSpec((1, tl, ML_V_WIDTH), lambda b, l: (b, l, 0)),
                   pl.BlockSpec((1, ML_HEADS, ML_QK_DIM, ML_V_DIM), lambda b, l: (b, 0, 0, 0)),
                   pl.BlockSpec((1, ML_HEADS, 1, ML_QK_DIM), lambda b, l: (b, 0, 0, 0)),
                   pl.BlockSpec((1, ML_HEADS, 1, 1), lambda b, l: (b, 0, 0, 0))],
        out_shape=[jax.ShapeDtypeStruct((B, S, ML_V_WIDTH), BF16),
                   jax.ShapeDtypeStruct((B, ML_HEADS, ML_QK_DIM, ML_V_DIM), F32),
                   jax.ShapeDtypeStruct((B, ML_HEADS, 1, ML_QK_DIM), F32),
                   jax.ShapeDtypeStruct((B, ML_HEADS, 1, 1), F32)],
        scratch_shapes=[pltpu.VMEM((ML_HEADS, ML_QK_DIM, ML_V_DIM), F32),
                        pltpu.VMEM((ML_HEADS, 1, ML_QK_DIM), F32),
                        pltpu.VMEM((ML_HEADS, 1, LANES), F32)],
        compiler_params=_cp(("parallel", "arbitrary")),
        name="mlstm",
    )(z3, z3, z3, z3, if3, c0, n0, m0p, g)


def _merge_kernel(ya_ref, yb_ref, yc_ref, g0_ref, g1_ref, g2_ref, x_ref, wpa_ref, wpb_ref, wpc_ref,
                  wo_ref, lg_ref, lb_ref, o_ref, *, alpha):
    mixed = _sigmoid(g0_ref[...]) * _dot(ya_ref[...], wpa_ref[...])
    mixed = mixed + _sigmoid(g1_ref[...]) * _dot(yb_ref[...], wpb_ref[...])
    mixed = mixed + _sigmoid(g2_ref[...]) * _dot(yc_ref[...], wpc_ref[...])
    r = alpha * x_ref[...] + _dot(mixed.astype(BF16), wo_ref[...])
    o_ref[...] = _layer_norm(r, lg_ref[...], lb_ref[...])


def _merge(ya, yb, yc, z, x, wpa, wpb, wpc, wo, lg, lb, tm, alpha):
    T = x.shape[0]
    full = lambda shape: pl.BlockSpec(shape, lambda i: (0,) * len(shape))
    mgb = OFF_MG // D_MODEL
    return pl.pallas_call(
        functools.partial(_merge_kernel, alpha=alpha),
        grid=(T // tm,),
        in_specs=[pl.BlockSpec((tm, D_RNN), lambda i: (i, 0)),
                  pl.BlockSpec((tm, SB_WIDTH), lambda i: (i, 0)),
                  pl.BlockSpec((tm, ML_V_WIDTH), lambda i: (i, 0)),
                  pl.BlockSpec((tm, D_MODEL), lambda i: (i, mgb)),
                  pl.BlockSpec((tm, D_MODEL), lambda i: (i, mgb + 1)),
                  pl.BlockSpec((tm, D_MODEL), lambda i: (i, mgb + 2)),
                  pl.BlockSpec((tm, D_MODEL), lambda i: (i, 0)),
                  full((D_RNN, D_MODEL)), full((SB_WIDTH, D_MODEL)), full((ML_V_WIDTH, D_MODEL)),
                  full((D_MODEL, D_MODEL)), full((1, D_MODEL)), full((1, D_MODEL))],
        out_specs=pl.BlockSpec((tm, D_MODEL), lambda i: (i, 0)),
        out_shape=jax.ShapeDtypeStruct((T, D_MODEL), F32),
        compiler_params=_cp(("parallel",)),
        name="merge_ln1",
    )(ya, yb, yc, z, z, z, x, wpa, wpb, wpc, wo, lg, lb)


def _router_kernel(x_ref, rw_ref, rb_ref, tri_ref, topi_ref, gate_ref, rank_ref, cnt_ref, run_ref):
    @pl.when(pl.program_id(0) == 0)
    def _():
        run_ref[...] = jnp.zeros_like(run_ref)

    tm = x_ref.shape[0]
    l = _dot(x_ref[...].astype(BF16), rw_ref[...]) + rb_ref[...]
    lane = lax.broadcasted_iota(jnp.int32, (tm, LANES), 1)
    lane_f = lane.astype(F32)
    vals, idxs, hots = [], [], []
    for _ in range(TOP_K):
        m = jnp.max(l, axis=1, keepdims=True)
        idx = jnp.min(jnp.where(l == m, lane_f, float(LANES)), axis=1, keepdims=True)
        hot = lane_f == idx
        vals.append(m)
        idxs.append(idx)
        hots.append(hot)
        l = jnp.where(hot, -jnp.inf, l)
    ex = [jnp.exp(v - vals[0]) for v in vals]
    den = ex[0] + ex[1] + ex[2] + ex[3]
    onehot = jnp.zeros((tm, LANES), F32)
    for hot in hots:
        onehot = onehot + jnp.where(hot, 1.0, 0.0)
    before = _dot(tri_ref[...], onehot.astype(BF16)) + run_ref[...]
    topi = jnp.zeros((tm, LANES), F32)
    gate = jnp.zeros((tm, LANES), F32)
    rank = jnp.zeros((tm, LANES), F32)
    for k in range(TOP_K):
        rk = jnp.sum(jnp.where(hots[k], before, 0.0), axis=1, keepdims=True)
        topi = jnp.where(lane == k, idxs[k], topi)
        gate = jnp.where(lane == k, ex[k] / den, gate)
        rank = jnp.where(lane == k, rk, rank)
    run_new = run_ref[...] + jnp.sum(onehot, axis=0, keepdims=True)
    run_ref[...] = run_new
    topi_ref[...] = topi.astype(jnp.int32)
    gate_ref[...] = gate
    rank_ref[...] = rank.astype(jnp.int32)
    cnt_ref[...] = run_new.astype(jnp.int32)


def _router(x, rw, rb, tri, tm):
    T = x.shape[0]
    full = lambda shape: pl.BlockSpec(shape, lambda i: (0,) * len(shape))
    return pl.pallas_call(
        _router_kernel,
        grid=(T // tm,),
        in_specs=[pl.BlockSpec((tm, D_MODEL), lambda i: (i, 0)),
                  full((D_MODEL, LANES)), full((1, LANES)), full((tm, tm))],
        out_specs=[pl.BlockSpec((tm, LANES), lambda i: (i, 0)),
                   pl.BlockSpec((tm, LANES), lambda i: (i, 0)),
                   pl.BlockSpec((tm, LANES), lambda i: (i, 0)),
                   full((1, LANES))],
        out_shape=[jax.ShapeDtypeStruct((T, LANES), jnp.int32),
                   jax.ShapeDtypeStruct((T, LANES), F32),
                   jax.ShapeDtypeStruct((T, LANES), jnp.int32),
                   jax.ShapeDtypeStruct((1, LANES), jnp.int32)],
        scratch_shapes=[pltpu.VMEM((1, LANES), F32)],
        compiler_params=_cp(("arbitrary",)),
        name="router",
    )(x, rw, rb, tri)


ROW_CHUNKS = D_MODEL // LANES


def _tile_copy(src_ref, src_row, dst_ref, dst_row, sem):
    src = src_ref.at[pl.ds(pl.multiple_of(src_row * ROW_CHUNKS, ROW_CHUNKS), ROW_CHUNKS)]
    dst = dst_ref.at[pl.ds(pl.multiple_of(dst_row * ROW_CHUNKS, ROW_CHUNKS), ROW_CHUNKS)]
    return pltpu.make_async_copy(src, dst, sem)


def _interleave_store(ref, row0, value):
    n = value.shape[0]
    for c in range(ROW_CHUNKS):
        ref[pl.ds(row0 * ROW_CHUNKS + c, n, stride=ROW_CHUNKS), :] = value[:, c * LANES:(c + 1) * LANES]


def _interleave_load(ref, row0, n):
    return jnp.concatenate(
        [ref[pl.ds(row0 * ROW_CHUNKS + c, n, stride=ROW_CHUNKS), :] for c in range(ROW_CHUNKS)], axis=1)


def _dispatch_kernel(padlo_ref, padn_ref, dest_ref, x_ref, xs_hbm, xi, ztile, sem, *, ts):
    @pl.when(pl.program_id(0) == 0)
    def _():
        ztile[...] = jnp.zeros_like(ztile)

        def fill(e, c):
            lax.fori_loop(0, padn_ref[e],
                          lambda r, c2: (_tile_copy(ztile, 0, xs_hbm, padlo_ref[e] + r, sem).start(), c2)[1], 0)
            return c

        def fill_done(e, c):
            lax.fori_loop(0, padn_ref[e],
                          lambda r, c2: (_tile_copy(ztile, 0, xs_hbm, 0, sem).wait(), c2)[1], 0)
            return c

        lax.fori_loop(0, N_EXPERTS, fill, 0)
        lax.fori_loop(0, N_EXPERTS, fill_done, 0)

    _interleave_store(xi, 0, x_ref[...])

    def issue(t, c):
        for k in range(TOP_K):
            _tile_copy(xi, t, xs_hbm, dest_ref[TOP_K * t + k], sem).start()
        return c

    lax.fori_loop(0, ts, issue, 0)

    def drain(t, c):
        for k in range(TOP_K):
            _tile_copy(xi, 0, xs_hbm, 0, sem).wait()
        return c

    lax.fori_loop(0, ts, drain, 0)


def _dispatch(pad_lo, pad_n, dest_flat, x, rows, ts):
    T = x.shape[0]
    grid_spec = pltpu.PrefetchScalarGridSpec(
        num_scalar_prefetch=2,
        grid=(T // ts,),
        in_specs=[pl.BlockSpec((TOP_K * ts,), lambda i, lo, n: (i,), memory_space=pltpu.SMEM),
                  pl.BlockSpec((ts, D_MODEL), lambda i, lo, n: (i, 0))],
        out_specs=pl.BlockSpec(memory_space=pl.ANY),
        scratch_shapes=[pltpu.VMEM((ts * ROW_CHUNKS, LANES), F32), pltpu.VMEM((ROW_CHUNKS, LANES), F32),
                        pltpu.SemaphoreType.DMA(())],
    )
    return pl.pallas_call(
        functools.partial(_dispatch_kernel, ts=ts),
        grid_spec=grid_spec,
        out_shape=jax.ShapeDtypeStruct((rows * ROW_CHUNKS, LANES), F32),
        compiler_params=_cp(("arbitrary",)),
        name="moe_dispatch",
    )(pad_lo, pad_n, dest_flat, x)


def _experts_kernel(be_ref, nu_ref, xs_ref, w1_ref, b1_ref, w2_ref, b2_ref, y_ref, *, m):
    @pl.when(pl.program_id(0) < nu_ref[0])
    def _():
        gu = _dot(_interleave_load(xs_ref, 0, m).astype(BF16), w1_ref[0]) + b1_ref[0]
        g_ = jnp.minimum(gu[:, :D_FF], SWIGLU_LIMIT)
        up = jnp.clip(gu[:, D_FF:], -SWIGLU_LIMIT, SWIGLU_LIMIT)
        act = (up + 1.0) * g_ * _sigmoid(SWIGLU_ALPHA * g_)
        _interleave_store(y_ref, 0, _dot(act.astype(BF16), w2_ref[0]) + b2_ref[0])

    @pl.when(pl.program_id(0) >= nu_ref[0])
    def _():
        y_ref[...] = jnp.zeros_like(y_ref)


def _experts(block_e, n_used, xs, w1, b1, w2, b2, m):
    nblk = xs.shape[0] // (m * ROW_CHUNKS)
    grid_spec = pltpu.PrefetchScalarGridSpec(
        num_scalar_prefetch=2,
        grid=(nblk,),
        in_specs=[pl.BlockSpec((m * ROW_CHUNKS, LANES), lambda j, be, nu: (jnp.minimum(j, nu[0] - 1), 0)),
                  pl.BlockSpec((1, D_MODEL, 2 * D_FF), lambda j, be, nu: (be[j], 0, 0)),
                  pl.BlockSpec((1, 1, 2 * D_FF), lambda j, be, nu: (be[j], 0, 0)),
                  pl.BlockSpec((1, D_FF, D_MODEL), lambda j, be, nu: (be[j], 0, 0)),
                  pl.BlockSpec((1, 1, D_MODEL), lambda j, be, nu: (be[j], 0, 0))],
        out_specs=pl.BlockSpec((m * ROW_CHUNKS, LANES), lambda j, be, nu: (j, 0)),
    )
    return pl.pallas_call(
        functools.partial(_experts_kernel, m=m),
        grid_spec=grid_spec,
        out_shape=jax.ShapeDtypeStruct(xs.shape, F32),
        compiler_params=_cp(("arbitrary",)),
        name="moe_experts",
    )(block_e, n_used, xs, w1, b1, w2, b2)


def _combine_kernel(dest_ref, x_ref, p_ref, gate_ref, y_hbm, wp_ref, wg_ref, bg_ref,
                    lg_ref, lb_ref, o_ref, ybuf, sem, *, alpha):
    tm = x_ref.shape[0]

    def issue(t, c):
        for k in range(TOP_K):
            _tile_copy(y_hbm, dest_ref[TOP_K * t + k], ybuf, k * tm + t, sem).start(priority=k % 2)
        return c

    lax.fori_loop(0, tm, issue, 0)

    x = x_ref[...]
    ple = _sigmoid(_dot(x.astype(BF16), wg_ref[...]) + bg_ref[...]) * _dot(p_ref[...].astype(BF16), wp_ref[...])
    r = alpha * x + ple

    def drain(t, c):
        for k in range(TOP_K):
            _tile_copy(y_hbm, 0, ybuf, 0, sem).wait()
        return c

    lax.fori_loop(0, tm, drain, 0)
    for k in range(TOP_K):
        r = r + gate_ref[:, k:k + 1] * _interleave_load(ybuf, k * tm, tm)
    o_ref[...] = _layer_norm(r, lg_ref[...], lb_ref[...])


def _combine(dest_flat, x, p, gate, y, wp, wg, bg, lg, lb, tm, alpha):
    T = x.shape[0]
    full = lambda shape: pl.BlockSpec(shape, lambda i: (0,) * len(shape))
    return pl.pallas_call(
        functools.partial(_combine_kernel, alpha=alpha),
        grid=(T // tm,),
        in_specs=[pl.BlockSpec((TOP_K * tm,), lambda i: (i,), memory_space=pltpu.SMEM),
                  pl.BlockSpec((tm, D_MODEL), lambda i: (i, 0)),
                  pl.BlockSpec((tm, D_PLE), lambda i: (i, 0)),
                  pl.BlockSpec((tm, LANES), lambda i: (i, 0)),
                  pl.BlockSpec(memory_space=pl.ANY),
                  full((D_PLE, D_MODEL)), full((D_MODEL, D_MODEL)), full((1, D_MODEL)),
                  full((1, D_MODEL)), full((1, D_MODEL))],
        out_specs=pl.BlockSpec((tm, D_MODEL), lambda i: (i, 0)),
        out_shape=jax.ShapeDtypeStruct((T, D_MODEL), F32),
        scratch_shapes=[pltpu.VMEM((TOP_K * tm * ROW_CHUNKS, LANES), F32), pltpu.SemaphoreType.DMA(())],
        compiler_params=_cp(("arbitrary",)),
        name="moe_combine_ln2",
    )(dest_flat, x, p, gate, y, wp, wg, bg, lg, lb)


def _pick(n, pref):
    t = min(n, pref)
    while n % t:
        t //= 2
    return t


def _layer(x, p, lw, conv_buf, lru_h, ml_c, ml_n, ml_m, k_past, v_past, prompt, alpha):
    B, L, _ = x.shape
    T = B * L
    xf = x.reshape(T, D_MODEL)

    z = _inproj(xf, lw["w_in"], lw["b_in"], _pick(T, 1024), 1024)
    k_new, v_new, ifg = _kvif(xf, lw["w_kvif"], lw["b_kvif"], _pick(T, 1024))
    z3 = z.reshape(B, L, N_MAIN)
    k_new = k_new.reshape(B, L, SB_WIDTH)
    v_new = v_new.reshape(B, L, SB_WIDTH)
    conv_new = z3[:, L - (CONV_W - 1):, OFF_AX:OFF_AX + D_RNN]

    cbuf8 = jnp.concatenate([jnp.zeros((B, 8 - (CONV_W - 1), D_RNN), F32), conv_buf.astype(F32)], axis=1)
    ya, lru_new = _lru(z3, cbuf8, lru_h.reshape(B, 1, D_RNN).astype(F32), lw["conv_w8"], lw["conv_b"],
                       lw["wr"], lw["br"], lw["wi"], lw["bi"], lw["lam"], _pick(L, 256), prompt)

    if prompt:
        yb = _sb_attention(z3, k_new, v_new, OFF_Q // SB_HEAD_DIM, 0, 0, _pick(L, 1024), 0)
    else:
        P = k_past.shape[1]
        pad = (-(P + L)) % KEY_BLOCK
        zpad = jnp.zeros((B, pad, SB_WIDTH), F32)
        k_all = jnp.concatenate([k_past.reshape(B, P, SB_WIDTH), k_new, zpad], axis=1)
        v_all = jnp.concatenate([v_past.reshape(B, P, SB_WIDTH), v_new, zpad], axis=1)
        yb = _sb_attention(z3, k_all, v_all, OFF_Q // SB_HEAD_DIM, 0, 0, L, P)

    ck = _pick(L, 128)
    yc, c_new, n_new, m_new = _mlstm(z3, ifg.reshape(B, L, LANES), ml_c.astype(F32),
                                     ml_n.astype(F32).reshape(B, ML_HEADS, 1, ML_QK_DIM),
                                     ml_m.astype(F32).reshape(B, ML_HEADS, 1, 1),
                                     lw["ml_g"], _pick(L, 512), ck)
    n_new = n_new.reshape(B, ML_HEADS, ML_QK_DIM)
    m_new = m_new.reshape(B, ML_HEADS)

    x1 = _merge(ya.reshape(T, D_RNN), yb.reshape(T, SB_WIDTH), yc.reshape(T, ML_V_WIDTH), z, xf,
                lw["w_pa"], lw["w_pb"], lw["w_pc"], lw["w_out"], lw["ln1_g"], lw["ln1_b"],
                _pick(T, 512), alpha)

    tr = _pick(T, 512)
    tri = jnp.tril(jnp.ones((tr, tr), BF16), -1)
    topi, gate, rank, cnt = _router(x1, lw["router_w"], lw["router_b"], tri, tr)
    TK = T * TOP_K
    m_rows = max(16, min(512, TK // N_EXPERTS))
    nb = TK // m_rows + N_EXPERTS
    counts = cnt[0, :N_EXPERTS]
    padded = (counts + (m_rows - 1)) // m_rows * m_rows
    pend = jnp.cumsum(padded)
    off = (pend - padded).astype(jnp.int32)
    blk_start = jnp.arange(nb, dtype=jnp.int32) * m_rows
    block_e = jnp.minimum(jnp.sum((pend[None, :] <= blk_start[:, None]).astype(jnp.int32), axis=1),
                          N_EXPERTS - 1).astype(jnp.int32)
    n_used = (pend[-1:] // m_rows).astype(jnp.int32)
    experts = jnp.arange(N_EXPERTS, dtype=jnp.int32)
    top4 = topi[:, :TOP_K]
    dest = rank[:, :TOP_K] + jnp.sum(jnp.where(top4[:, :, None] == experts, off, 0), axis=-1)
    dest_flat = dest.reshape(TK).astype(jnp.int32)
    xs = _dispatch(off + counts, (padded - counts).astype(jnp.int32), dest_flat, x1, nb * m_rows,
                   _pick(T, 256))
    y = _experts(block_e, n_used, xs, lw["exp_w1"], lw["exp_b1"], lw["exp_w2"], lw["exp_b2"], m_rows)
    x2 = _combine(dest_flat, x1, p.reshape(T, D_PLE), gate, y, lw["ple_w"], lw["ple_gate_w"],
                  lw["ple_gate_b"], lw["ln2_g"], lw["ln2_b"], _pick(T, 256), alpha)

    k_out = k_new.reshape(B, L, SB_HEADS, SB_HEAD_DIM)
    v_out = v_new.reshape(B, L, SB_HEADS, SB_HEAD_DIM)
    return x2.reshape(B, L, D_MODEL), (k_out, v_out, conv_new, lru_new.reshape(B, D_RNN), c_new, n_new, m_new)


def _block_diag_groups(w):
    per = LRU_GROUP // LRU_BLOCK
    w4 = w.reshape(D_RNN // LRU_GROUP, per, LRU_BLOCK, LRU_BLOCK)
    eye = jnp.eye(per, dtype=w.dtype)
    return jnp.einsum("gacd,ab->gacbd", w4, eye).reshape(D_RNN // LRU_GROUP, LRU_GROUP, LRU_GROUP)


def _split_in(w):
    main = jnp.concatenate([w[..., :ORIG_K], w[..., ORIG_MQ:ORIG_IF], w[..., ORIG_MG:]], axis=-1)
    pad = jnp.zeros(w.shape[:-1] + (LANES - 2 * ML_HEADS,), w.dtype)
    kvif = jnp.concatenate([w[..., ORIG_K:ORIG_MQ], w[..., ORIG_IF:ORIG_MG], pad], axis=-1)
    return main, kvif


def kernel(x_prompt, x_sample, cache_sb_k, cache_sb_v, state_conv, state_lru, state_mlstm_c, state_mlstm_n, state_mlstm_m, p_prompt, p_sample, w_in, b_in, conv_w, conv_b, lru_wr, lru_br, lru_wi, lru_bi, lru_lambda, ml_norm_g, w_pa, w_pb, w_pc, w_out, ln1_g, ln1_b, router_w, router_b, exp_w1, exp_b1, exp_w2, exp_b2, ple_w, ple_gate_w, ple_gate_b, ln2_g, ln2_b):
    depth = w_in.shape[0]
    alpha = (2 * depth) ** 0.25
    B = x_prompt.shape[0]
    y_prompt, y_sample = x_prompt, x_sample
    st_p, st_s = [], []
    for i in range(depth):
        row = lambda a: a[i].reshape(1, -1).astype(F32)
        w_main, w_kvif = _split_in(w_in[i])
        b_main, b_kvif = _split_in(b_in[i])
        lw = dict(
            w_in=w_main.astype(BF16), b_in=b_main.reshape(1, -1).astype(F32),
            w_kvif=w_kvif.astype(BF16), b_kvif=b_kvif.reshape(1, -1).astype(F32),
            conv_w8=jnp.pad(conv_w[i].astype(F32), ((0, 8 - CONV_W), (0, 0))), conv_b=row(conv_b),
            wr=_block_diag_groups(lru_wr[i]).astype(BF16), br=row(lru_br),
            wi=_block_diag_groups(lru_wi[i]).astype(BF16), bi=row(lru_bi), lam=row(lru_lambda),
            ml_g=row(ml_norm_g),
            w_pa=w_pa[i].astype(BF16), w_pb=w_pb[i].astype(BF16), w_pc=w_pc[i].astype(BF16),
            w_out=w_out[i].astype(BF16), ln1_g=row(ln1_g), ln1_b=row(ln1_b),
            router_w=jnp.pad(router_w[i], ((0, 0), (0, LANES - N_EXPERTS))).astype(BF16),
            router_b=jnp.pad(router_b[i].astype(F32), (0, LANES - N_EXPERTS),
                             constant_values=-1e30).reshape(1, LANES),
            exp_w1=exp_w1[i].astype(BF16), exp_b1=exp_b1[i].reshape(N_EXPERTS, 1, 2 * D_FF).astype(F32),
            exp_w2=exp_w2[i].astype(BF16), exp_b2=exp_b2[i].reshape(N_EXPERTS, 1, D_MODEL).astype(F32),
            ple_w=ple_w[i].astype(BF16), ple_gate_w=ple_gate_w[i].astype(BF16), ple_gate_b=row(ple_gate_b),
            ln2_g=row(ln2_g), ln2_b=row(ln2_b),
        )
        y_prompt, sp = _layer(y_prompt, p_prompt[i], lw,
                              jnp.zeros((B, CONV_W - 1, D_RNN), F32), jnp.zeros((B, D_RNN), F32),
                              jnp.zeros((B, ML_HEADS, ML_QK_DIM, ML_V_DIM), F32),
                              jnp.zeros((B, ML_HEADS, ML_QK_DIM), F32), jnp.zeros((B, ML_HEADS), F32),
                              None, None, True, alpha)
        y_sample, ss = _layer(y_sample, p_sample[i], lw, state_conv[i], state_lru[i], state_mlstm_c[i],
                              state_mlstm_n[i], state_mlstm_m[i], cache_sb_k[i], cache_sb_v[i], False, alpha)
        st_p.append(sp)
        st_s.append(ss)
    k_p, v_p, conv_p, lru_p, c_p, n_p, m_p = [jnp.stack(s) for s in zip(*st_p)]
    k_s, v_s, conv_s, lru_s, c_s, n_s, m_s = [jnp.stack(s) for s in zip(*st_s)]
    return (y_prompt, y_sample, k_p, v_p, conv_p, lru_p, c_p, n_p, m_p,
            k_s, v_s, conv_s, lru_s, c_s, n_s, m_s)
```

```python
import functools

import jax
import jax.numpy as jnp
from jax import lax
from jax.experimental import pallas as pl
from jax.experimental.pallas import tpu as pltpu

F32 = jnp.float32
BF16 = jnp.bfloat16

D_MODEL = 1024
D_RNN = 1024
N_LRU_BLOCKS = 16
LRU_BLOCK = D_RNN // N_LRU_BLOCKS
LRU_GROUP = 256
CONV_W = 4
LRU_C = 8.0
SB_HEADS = 4
SB_HEAD_DIM = 128
SB_WIDTH = SB_HEADS * SB_HEAD_DIM
ML_HEADS = 4
ML_QK_DIM = 64
ML_V_DIM = 128
ML_QK_WIDTH = ML_HEADS * ML_QK_DIM
ML_V_WIDTH = ML_HEADS * ML_V_DIM
N_BRANCH = 3
N_EXPERTS = 32
TOP_K = 4
D_FF = 512
SWIGLU_LIMIT = 7.0
SWIGLU_ALPHA = 1.702
D_PLE = 256
LN_EPS = 1e-5
LANES = 128
KEY_BLOCK = 128

OFF_AX, OFF_AG = 0, 1024
OFF_Q = 2048
OFF_MQ, OFF_MK, OFF_MV, OFF_MO = 2560, 2816, 3072, 3584
OFF_MG = 4096
N_MAIN = 7168
N_KVIF = 2 * SB_WIDTH + LANES
ORIG_K, ORIG_MQ, ORIG_IF, ORIG_MG = 2560, 3584, 5120, 5128
SB_DEAD_MASS = 88.0
VMEM_LIMIT = 56 * 1024 * 1024


def _cp(sem, vmem=VMEM_LIMIT):
    return pltpu.CompilerParams(dimension_semantics=sem, vmem_limit_bytes=vmem)


def _dot(a, b):
    return jnp.dot(a, b, preferred_element_type=F32)


def _dot_nt(a, b):
    return lax.dot_general(a, b, (((1,), (1,)), ((), ())), preferred_element_type=F32)


def _dot_tn(a, b):
    return lax.dot_general(a, b, (((0,), (0,)), ((), ())), preferred_element_type=F32)


def _sigmoid(x):
    return 0.5 * jnp.tanh(0.5 * x) + 0.5


def _softplus(x):
    return jnp.maximum(x, 0.0) + jnp.log1p(jnp.exp(-jnp.abs(x)))


def _layer_norm(r, g, b):
    mu = jnp.mean(r, axis=-1, keepdims=True)
    d = r - mu
    var = jnp.mean(d * d, axis=-1, keepdims=True)
    return d * lax.rsqrt(var + LN_EPS) * g + b


def _inproj_kernel(x_ref, w_ref, b_ref, o_ref, xb_ref):
    @pl.when(pl.program_id(1) == 0)
    def _():
        xb_ref[...] = x_ref[...].astype(BF16)

    o_ref[...] = (_dot(xb_ref[...], w_ref[...]) + b_ref[...]).astype(o_ref.dtype)


def _inproj(x, w, b, tm, tn):
    T, D = x.shape
    N = w.shape[1]
    return pl.pallas_call(
        _inproj_kernel,
        grid=(T // tm, N // tn),
        in_specs=[pl.BlockSpec((tm, D), lambda i, j: (i, 0)),
                  pl.BlockSpec((D, tn), lambda i, j: (0, j)),
                  pl.BlockSpec((1, tn), lambda i, j: (0, j))],
        out_specs=pl.BlockSpec((tm, tn), lambda i, j: (i, j)),
        out_shape=jax.ShapeDtypeStruct((T, N), BF16),
        scratch_shapes=[pltpu.VMEM((tm, D), BF16)],
        compiler_params=_cp(("parallel", "arbitrary")),
        name="inproj",
    )(x, w, b)


def _kvif_kernel(x_ref, w_ref, b_ref, k_ref, v_ref, if_ref):
    r = _dot(x_ref[...].astype(BF16), w_ref[...]) + b_ref[...]
    k_ref[...] = r[:, :SB_WIDTH]
    v_ref[...] = r[:, SB_WIDTH:2 * SB_WIDTH]
    if_ref[...] = r[:, 2 * SB_WIDTH:]


def _kvif(x, w, b, tm):
    T, D = x.shape
    return pl.pallas_call(
        _kvif_kernel,
        grid=(T // tm,),
        in_specs=[pl.BlockSpec((tm, D), lambda i: (i, 0)),
                  pl.BlockSpec((D, N_KVIF), lambda i: (0, 0)),
                  pl.BlockSpec((1, N_KVIF), lambda i: (0, 0))],
        out_specs=[pl.BlockSpec((tm, SB_WIDTH), lambda i: (i, 0)),
                   pl.BlockSpec((tm, SB_WIDTH), lambda i: (i, 0)),
                   pl.BlockSpec((tm, LANES), lambda i: (i, 0))],
        out_shape=[jax.ShapeDtypeStruct((T, SB_WIDTH), F32),
                   jax.ShapeDtypeStruct((T, SB_WIDTH), F32),
                   jax.ShapeDtypeStruct((T, LANES), F32)],
        compiler_params=_cp(("parallel",)),
        name="inproj_kvif",
    )(x, w, b)


def _lru_kernel(ax_ref, ag_ref, cbuf_ref, h0_ref, cw_ref, cb_ref, wr_ref, br_ref, wi_ref, bi_ref,
                lam_ref, ya_ref, hl_ref, ext_ref, hc_ref, *, first_frame):
    tl = ax_ref.shape[1]
    li = pl.program_id(1)

    @pl.when(li == 0)
    def _():
        ext_ref[0:8, :] = cbuf_ref[0]
        hc_ref[...] = h0_ref[0]

    ext_ref[8:8 + tl, :] = ax_ref[0].astype(F32)
    ua = cb_ref[...] + ext_ref[5:5 + tl, :] * cw_ref[0:1, :]
    for j in range(1, CONV_W):
        ua = ua + ext_ref[5 + j:5 + j + tl, :] * cw_ref[j:j + 1, :]
    ext_ref[5:8, :] = ext_ref[5 + tl:8 + tl, :]

    uab = ua.astype(BF16)
    ng = D_RNN // LRU_GROUP
    r_pre = jnp.concatenate(
        [_dot(uab[:, g * LRU_GROUP:(g + 1) * LRU_GROUP], wr_ref[g]) for g in range(ng)], axis=1)
    i_pre = jnp.concatenate(
        [_dot(uab[:, g * LRU_GROUP:(g + 1) * LRU_GROUP], wi_ref[g]) for g in range(ng)], axis=1)
    r = _sigmoid(r_pre + br_ref[...])
    ig = _sigmoid(i_pre + bi_ref[...])
    log_a = (-LRU_C) * r * _softplus(-lam_ref[...])
    a = jnp.exp(log_a)
    th = jnp.tanh(log_a)
    mult = jnp.sqrt(-2.0 * th / (1.0 - th))
    row = lax.broadcasted_iota(jnp.int32, (tl, D_RNN), 0)
    if first_frame:
        mult = jnp.where((row == 0) & (li == 0), 1.0, mult)
    bx = mult * (ig * ua)

    a = a.reshape(tl // 8, 8, D_RNN)
    bx = bx.reshape(tl // 8, 8, D_RNN)
    sub = lax.broadcasted_iota(jnp.int32, (tl // 8, 8, D_RNN), 1)
    for d in (1, 2, 4):
        a_sh = jnp.where(sub < d, 1.0, pltpu.roll(a, d, 1))
        b_sh = jnp.where(sub < d, 0.0, pltpu.roll(bx, d, 1))
        bx = a * b_sh + bx
        a = a * a_sh
    carry = hc_ref[...]
    groups = []
    for g in range(tl // 8):
        hg = bx[g] + a[g] * carry
        carry = hg[7:8, :]
        groups.append(hg)
    h = jnp.concatenate(groups, axis=0)
    hc_ref[...] = carry
    hl_ref[0] = carry

    gx = ag_ref[0].astype(F32)
    gelu = 0.5 * gx * (1.0 + jnp.tanh(0.7978845608028654 * (gx + 0.044715 * (gx * gx * gx))))
    ya_ref[0] = (h * gelu).astype(ya_ref.dtype)


def _lru(z3, cbuf8, h0, cw8, cb, wr, br, wi, bi, lam, tl, first_frame):
    B, S, _ = z3.shape
    full = lambda shape: pl.BlockSpec(shape, lambda b, l: (0,) * len(shape))
    return pl.pallas_call(
        functools.partial(_lru_kernel, first_frame=first_frame),
        grid=(B, S // tl),
        in_specs=[pl.BlockSpec((1, tl, D_RNN), lambda b, l: (b, l, OFF_AX // D_RNN)),
                  pl.BlockSpec((1, tl, D_RNN), lambda b, l: (b, l, OFF_AG // D_RNN)),
                  pl.BlockSpec((1, 8, D_RNN), lambda b, l: (b, 0, 0)),
                  pl.BlockSpec((1, 1, D_RNN), lambda b, l: (b, 0, 0)),
                  full((8, D_RNN)), full((1, D_RNN)),
                  full((D_RNN // LRU_GROUP, LRU_GROUP, LRU_GROUP)), full((1, D_RNN)),
                  full((D_RNN // LRU_GROUP, LRU_GROUP, LRU_GROUP)), full((1, D_RNN)),
                  full((1, D_RNN))],
        out_specs=[pl.BlockSpec((1, tl, D_RNN), lambda b, l: (b, l, 0)),
                   pl.BlockSpec((1, 1, D_RNN), lambda b, l: (b, 0, 0))],
        out_shape=[jax.ShapeDtypeStruct((B, S, D_RNN), BF16),
                   jax.ShapeDtypeStruct((B, 1, D_RNN), F32)],
        scratch_shapes=[pltpu.VMEM((tl + 8, D_RNN), F32), pltpu.VMEM((1, D_RNN), F32)],
        compiler_params=_cp(("parallel", "arbitrary")),
        name="conv_rglru",
    )(z3, z3, cbuf8, h0, cw8, cb, wr, br, wi, bi, lam)


def _sb_kernel(q_ref, k_ref, v_ref, kn_ref, vn_ref, o_ref, acc_ref, c_ref, *, tq, q_off, n_new):
    tk = KEY_BLOCK
    rs = min(tq, tk)
    ns = tq // rs
    qi = pl.program_id(2)
    q = (q_ref[0].astype(F32) * (SB_HEAD_DIM ** -0.5)).astype(BF16)
    qpos = q_off + qi * tq + lax.broadcasted_iota(jnp.int32, (tq, tk), 0)
    lane = lax.broadcasted_iota(jnp.int32, (rs, tk), 1)
    jj = lax.broadcasted_iota(jnp.int32, (tk, tk), 0)
    ll = lax.broadcasted_iota(jnp.int32, (tk, tk), 1)
    suffix = jnp.where(jj >= ll, 1.0, 0.0).astype(BF16)
    acc_ref[...] = jnp.zeros_like(acc_ref)
    c_ref[...] = jnp.zeros_like(c_ref)
    diag = [(q_off + qi * tq + (i + 1) * rs - 2) // tk for i in range(ns)]
    never = jnp.int32(2 ** 30)
    group = lax.broadcasted_iota(jnp.int32, (tq, 1), 0) // rs
    diag_row = jnp.zeros((tq, 1), jnp.int32)
    for i in range(ns):
        diag_row = jnp.where(group == i, diag[i], diag_row)

    def cond(carry):
        s, cmin = carry
        return (s <= diag[ns - 1]) & (cmin < SB_DEAD_MASS)

    def step(s, ks, vs, kpos):
        zs = [_dot_nt(q[i * rs:(i + 1) * rs, :], ks[i]) for i in range(ns)]
        z = jnp.concatenate(zs, axis=0) if ns > 1 else zs[0]
        earlier = (jnp.concatenate(kpos, axis=0) if ns > 1 else kpos[0]) < qpos
        u = jnp.where(earlier, _softplus(z), 0.0)
        u_hi = u.astype(BF16)
        u_lo = (u - u_hi.astype(F32)).astype(BF16)
        suf = _dot(u_hi, suffix) + _dot(u_lo, suffix)
        c = c_ref[...]
        w = jnp.where(earlier, jnp.exp(z - suf - c), 0.0).astype(BF16)
        pv = [_dot(w[i * rs:(i + 1) * rs, :], vs[i]) for i in range(ns)]
        acc_ref[...] += jnp.concatenate(pv, axis=0) if ns > 1 else pv[0]
        c_new = c + suf[:, 0:1]
        c_ref[...] = c_new
        return s + 1, jnp.min(jnp.where(diag_row > s, c_new, jnp.inf))

    def body(carry):
        s, _ = carry
        ks, vs, kpos = [], [], []
        for i in range(ns):
            kb = diag[i] - s
            start = pl.multiple_of(jnp.maximum(kb, 0) * tk, tk)
            ks.append(k_ref[0, 0, pl.ds(start, tk), :].astype(BF16))
            vs.append(v_ref[0, 0, pl.ds(start, tk), :].astype(BF16))
            kpos.append(jnp.where(kb >= 0, start, never) + lane)
        return step(s, ks, vs, kpos)

    first = (jnp.int32(0), jnp.float32(0.0))
    if n_new:
        pad = jnp.zeros((tk - n_new, SB_HEAD_DIM), BF16)
        first = step(jnp.int32(0), [jnp.concatenate([kn_ref[0].astype(BF16), pad], axis=0)],
                     [jnp.concatenate([vn_ref[0].astype(BF16), pad], axis=0)], [diag[0] * tk + lane])
    lax.while_loop(cond, body, first)
    o_ref[0] = acc_ref[...].astype(o_ref.dtype)


def _sb_attention(q_arr, k_arr, v_arr, kn_arr, vn_arr, q_cb, layer, tq, q_off, n_new):
    B, Tq, _ = q_arr.shape
    Tk = k_arr.shape[2]
    nrows = kn_arr.shape[1] if n_new else 8
    assert n_new == 0 or (Tq == tq == n_new <= KEY_BLOCK and q_off % KEY_BLOCK == 0 and q_off == Tk)
    return pl.pallas_call(
        functools.partial(_sb_kernel, tq=tq, q_off=q_off, n_new=n_new),
        grid=(B, SB_HEADS, Tq // tq),
        in_specs=[pl.BlockSpec((1, tq, SB_HEAD_DIM), lambda b, h, i: (b, i, q_cb + h)),
                  pl.BlockSpec((1, 1, Tk, SB_HEAD_DIM), lambda b, h, i: (layer, b, 0, h)),
                  pl.BlockSpec((1, 1, Tk, SB_HEAD_DIM), lambda b, h, i: (layer, b, 0, h)),
                  pl.BlockSpec((1, nrows, SB_HEAD_DIM), lambda b, h, i: (b, 0, h)),
                  pl.BlockSpec((1, nrows, SB_HEAD_DIM), lambda b, h, i: (b, 0, h))],
        out_specs=pl.BlockSpec((1, tq, SB_HEAD_DIM), lambda b, h, i: (b, i, h)),
        out_shape=jax.ShapeDtypeStruct((B, Tq, SB_WIDTH), BF16),
        scratch_shapes=[pltpu.VMEM((tq, SB_HEAD_DIM), F32), pltpu.VMEM((tq, 1), F32)],
        compiler_params=_cp(("parallel", "parallel", "arbitrary")),
        name="stick_breaking",
    )(q_arr, k_arr, v_arr, kn_arr, vn_arr)


def _mlstm_kernel(mq_ref, mk_ref, mv_ref, mo_ref, if_ref, c0_ref, n0_ref, m0_ref, g_ref,
                  yc_ref, c_out, n_out, m_out, c_s, n_s, m_s, *, ck):
    tl = mq_ref.shape[1]
    nh = ML_HEADS
    li = pl.program_id(1)

    @pl.when(li == 0)
    def _():
        c_s[...] = c0_ref[0]
        n_s[...] = n0_ref[0]
        m_s[...] = jnp.broadcast_to(m0_ref[0], (nh, 1, LANES))

    t_i = lax.broadcasted_iota(jnp.int32, (nh, ck, ck), 1)
    s_i = lax.broadcasted_iota(jnp.int32, (nh, ck, ck), 2)
    causal = s_i <= t_i
    diag = s_i == t_i
    gain = jnp.stack([g_ref[:, h * ML_V_DIM:(h + 1) * ML_V_DIM] for h in range(nh)])

    def heads(ref, r0, width):
        return jnp.stack([ref[0, pl.ds(r0, ck), h * width:(h + 1) * width] for h in range(nh)]).astype(F32)

    def chunk(ci, carry):
        r0 = pl.multiple_of(ci * ck, ck)
        ifb = if_ref[0, pl.ds(r0, ck), :]
        ig_col = jnp.stack([ifb[:, h:h + 1] for h in range(nh)])
        lf_col = -_softplus(-jnp.stack([ifb[:, nh + h:nh + h + 1] for h in range(nh)]))
        q4 = heads(mq_ref, r0, ML_QK_DIM)
        k4 = heads(mk_ref, r0, ML_QK_DIM) * (ML_QK_DIM ** -0.5)
        v4 = heads(mv_ref, r0, ML_V_DIM)
        o4 = heads(mo_ref, r0, ML_V_DIM)
        ig_row = jnp.sum(jnp.where(diag, ig_col, 0.0), axis=1, keepdims=True)
        lf_row = jnp.sum(jnp.where(diag, lf_col, 0.0), axis=1, keepdims=True)
        b_col = jnp.sum(jnp.where(causal, lf_row, 0.0), axis=2, keepdims=True)
        b_row = jnp.sum(jnp.where(t_i <= s_i, lf_col, 0.0), axis=1, keepdims=True)
        m_prev = m_s[:, :, 0:1]
        dmat = jnp.where(causal, b_col - b_row + ig_row, -jnp.inf)
        inter = b_col + m_prev
        m_t = jnp.maximum(inter, jnp.max(dmat, axis=2, keepdims=True))
        s_inter = jnp.exp(inter - m_t)
        qb = q4.astype(BF16)
        kb = k4.astype(BF16)
        vb = v4.astype(BF16)
        wqk = jnp.exp(dmat - m_t) * jnp.stack([_dot_nt(qb[h], kb[h]) for h in range(nh)])
        c_prev = c_s[...]
        n_prev = n_s[...]
        cb = c_prev.astype(BF16)
        wb = wqk.astype(BF16)
        num = (s_inter * jnp.stack([_dot(qb[h], cb[h]) for h in range(nh)])
               + jnp.stack([_dot(wb[h], vb[h]) for h in range(nh)]))
        den = (s_inter * jnp.sum(q4 * n_prev, axis=2, keepdims=True)
               + jnp.sum(wqk, axis=2, keepdims=True))
        hh = num / jnp.maximum(jnp.abs(den), jnp.exp(-m_t))
        b_end = b_col[:, ck - 1:ck, :]
        g_col = b_end - b_col + ig_col
        m_new = jnp.maximum(b_end + m_prev, jnp.max(g_col, axis=1, keepdims=True))
        s_old = jnp.exp(b_end + m_prev - m_new)
        kw = k4 * jnp.exp(g_col - m_new)
        kwb = kw.astype(BF16)
        c_s[...] = s_old * c_prev + jnp.stack([_dot_tn(kwb[h], vb[h]) for h in range(nh)])
        n_s[...] = s_old * n_prev + jnp.sum(kw, axis=1, keepdims=True)
        m_s[...] = jnp.broadcast_to(m_new, (nh, 1, LANES))
        mu = jnp.mean(hh, axis=2, keepdims=True)
        dlt = hh - mu
        var = jnp.mean(dlt * dlt, axis=2, keepdims=True)
        out = (dlt * lax.rsqrt(var + LN_EPS) * gain * _sigmoid(o4)).astype(yc_ref.dtype)
        for h in range(nh):
            yc_ref[0, pl.ds(r0, ck), h * ML_V_DIM:(h + 1) * ML_V_DIM] = out[h]
        return carry

    lax.fori_loop(0, tl // ck, chunk, 0)
    c_out[0] = c_s[...]
    n_out[0] = n_s[...]
    m_out[0] = m_s[:, :, 0:1]


def _mlstm(z3, if3, c0, n0, m0p, g, tl, ck):
    B, S, _ = z3.shape
    return pl.pallas_call(
        functools.partial(_mlstm_kernel, ck=ck),
        grid=(B, S // tl),
        in_specs=[pl.BlockSpec((1, tl, ML_QK_WIDTH), lambda b, l: (b, l, OFF_MQ // ML_QK_WIDTH)),
                  pl.BlockSpec((1, tl, ML_QK_WIDTH), lambda b, l: (b, l, OFF_MK // ML_QK_WIDTH)),
                  pl.BlockSpec((1, tl, ML_V_WIDTH), lambda b, l: (b, l, OFF_MV // ML_V_WIDTH)),
                  pl.BlockSpec((1, tl, ML_V_WIDTH), lambda b, l: (b, l, OFF_MO // ML_V_WIDTH)),
                  pl.BlockSpec((1, tl, LANES), lambda b, l: (b, l, 0)),
                  pl.BlockSpec((1, ML_HEADS, ML_QK_DIM, ML_V_DIM), lambda b, l: (b, 0, 0, 0)),
                  pl.BlockSpec((1, ML_HEADS, 1, ML_QK_DIM), lambda b, l: (b, 0, 0, 0)),
                  pl.BlockSpec((1, ML_HEADS, 1, 1), lambda b, l: (b, 0, 0, 0)),
                  pl.BlockSpec((1, ML_V_WIDTH), lambda b, l: (0, 0))],
        out_specs=[pl.BlockSpec((1, tl, ML_V_WIDTH), lambda b, l: (b, l, 0)),
                   pl.BlockSpec((1, ML_HEADS, ML_QK_DIM, ML_V_DIM), lambda b, l: (b, 0, 0, 0)),
                   pl.BlockSpec((1, ML_HEADS, 1, ML_QK_DIM), lambda b, l: (b, 0, 0, 0)),
                   pl.BlockSpec((1, ML_HEADS, 1, 1), lambda b, l: (b, 0, 0, 0))],
        out_shape=[jax.ShapeDtypeStruct((B, S, ML_V_WIDTH), BF16),
                   jax.ShapeDtypeStruct((B, ML_HEADS, ML_QK_DIM, ML_V_DIM), F32),
                   jax.ShapeDtypeStruct((B, ML_HEADS, 1, ML_QK_DIM), F32),
                   jax.ShapeDtypeStruct((B, ML_HEADS, 1, 1), F32)],
        scratch_shapes=[pltpu.VMEM((ML_HEADS, ML_QK_DIM, ML_V_DIM), F32),
                        pltpu.VMEM((ML_HEADS, 1, ML_QK_DIM), F32),
                        pltpu.VMEM((ML_HEADS, 1, LANES), F32)],
        compiler_params=_cp(("parallel", "arbitrary")),
        name="mlstm",
    )(z3, z3, z3, z3, if3, c0, n0, m0p, g)


def _merge_kernel(ya_ref, yb_ref, yc_ref, g0_ref, g1_ref, g2_ref, x_ref, wpa_ref, wpb_ref, wpc_ref,
                  wo_ref, lg_ref, lb_ref, o_ref, *, alpha):
    mixed = _sigmoid(g0_ref[...].astype(F32)) * _dot(ya_ref[...], wpa_ref[...])
    mixed = mixed + _sigmoid(g1_ref[...].astype(F32)) * _dot(yb_ref[...], wpb_ref[...])
    mixed = mixed + _sigmoid(g2_ref[...].astype(F32)) * _dot(yc_ref[...], wpc_ref[...])
    r = alpha * x_ref[...] + _dot(mixed.astype(BF16), wo_ref[...])
    o_ref[...] = _layer_norm(r, lg_ref[...], lb_ref[...])


def _merge(ya, yb, yc, z, x, wpa, wpb, wpc, wo, lg, lb, tm, alpha):
    T = x.shape[0]
    full = lambda shape: pl.BlockSpec(shape, lambda i: (0,) * len(shape))
    mgb = OFF_MG // D_MODEL
    return pl.pallas_call(
        functools.partial(_merge_kernel, alpha=alpha),
        grid=(T // tm,),
        in_specs=[pl.BlockSpec((tm, D_RNN), lambda i: (i, 0)),
                  pl.BlockSpec((tm, SB_WIDTH), lambda i: (i, 0)),
                  pl.BlockSpec((tm, ML_V_WIDTH), lambda i: (i, 0)),
                  pl.BlockSpec((tm, D_MODEL), lambda i: (i, mgb)),
                  pl.BlockSpec((tm, D_MODEL), lambda i: (i, mgb + 1)),
                  pl.BlockSpec((tm, D_MODEL), lambda i: (i, mgb + 2)),
                  pl.BlockSpec((tm, D_MODEL), lambda i: (i, 0)),
                  full((D_RNN, D_MODEL)), full((SB_WIDTH, D_MODEL)), full((ML_V_WIDTH, D_MODEL)),
                  full((D_MODEL, D_MODEL)), full((1, D_MODEL)), full((1, D_MODEL))],
        out_specs=pl.BlockSpec((tm, D_MODEL), lambda i: (i, 0)),
        out_shape=jax.ShapeDtypeStruct((T, D_MODEL), F32),
        compiler_params=_cp(("parallel",)),
        name="merge_ln1",
    )(ya, yb, yc, z, z, z, x, wpa, wpb, wpc, wo, lg, lb)


def _router_kernel(x_ref, rw_ref, rb_ref, tri_ref, topi_ref, gate_ref, rank_ref, cnt_ref, run_ref):
    @pl.when(pl.program_id(0) == 0)
    def _():
        run_ref[...] = jnp.zeros_like(run_ref)

    tm = x_ref.shape[0]
    l = _dot(x_ref[...].astype(BF16), rw_ref[...]) + rb_ref[...]
    lane = lax.broadcasted_iota(jnp.int32, (tm, LANES), 1)
    lane_f = lane.astype(F32)
    vals, idxs, hots = [], [], []
    for _ in range(TOP_K):
        m = jnp.max(l, axis=1, keepdims=True)
        idx = jnp.min(jnp.where(l == m, lane_f, float(LANES)), axis=1, keepdims=True)
        hot = lane_f == idx
        vals.append(m)
        idxs.append(idx)
        hots.append(hot)
        l = jnp.where(hot, -jnp.inf, l)
    ex = [jnp.exp(v - vals[0]) for v in vals]
    den = ex[0] + ex[1] + ex[2] + ex[3]
    onehot = jnp.zeros((tm, LANES), F32)
    for hot in hots:
        onehot = onehot + jnp.where(hot, 1.0, 0.0)
    before = _dot(tri_ref[...], onehot.astype(BF16)) + run_ref[...]
    topi = jnp.zeros((tm, LANES), F32)
    gate = jnp.zeros((tm, LANES), F32)
    rank = jnp.zeros((tm, LANES), F32)
    for k in range(TOP_K):
        rk = jnp.sum(jnp.where(hots[k], before, 0.0), axis=1, keepdims=True)
        topi = jnp.where(lane == k, idxs[k], topi)
        gate = jnp.where(lane == k, ex[k] / den, gate)
        rank = jnp.where(lane == k, rk, rank)
    run_new = run_ref[...] + jnp.sum(onehot, axis=0, keepdims=True)
    run_ref[...] = run_new
    topi_ref[...] = topi.astype(jnp.int32)
    gate_ref[...] = gate
    rank_ref[...] = rank.astype(jnp.int32)
    cnt_ref[...] = run_new.astype(jnp.int32)


def _router(x, rw, rb, tri, tm):
    T = x.shape[0]
    full = lambda shape: pl.BlockSpec(shape, lambda i: (0,) * len(shape))
    return pl.pallas_call(
        _router_kernel,
        grid=(T // tm,),
        in_specs=[pl.BlockSpec((tm, D_MODEL), lambda i: (i, 0)),
                  full((D_MODEL, LANES)), full((1, LANES)), full((tm, tm))],
        out_specs=[pl.BlockSpec((tm, LANES), lambda i: (i, 0)),
                   pl.BlockSpec((tm, LANES), lambda i: (i, 0)),
                   pl.BlockSpec((tm, LANES), lambda i: (i, 0)),
                   full((1, LANES))],
        out_shape=[jax.ShapeDtypeStruct((T, LANES), jnp.int32),
                   jax.ShapeDtypeStruct((T, LANES), F32),
                   jax.ShapeDtypeStruct((T, LANES), jnp.int32),
                   jax.ShapeDtypeStruct((1, LANES), jnp.int32)],
        scratch_shapes=[pltpu.VMEM((1, LANES), F32)],
        compiler_params=_cp(("arbitrary",)),
        name="router",
    )(x, rw, rb, tri)


ROW_CHUNKS = D_MODEL // LANES


def _tile_copy(src_ref, src_row, dst_ref, dst_row, sem):
    src = src_ref.at[pl.ds(pl.multiple_of(src_row * ROW_CHUNKS, ROW_CHUNKS), ROW_CHUNKS)]
    dst = dst_ref.at[pl.ds(pl.multiple_of(dst_row * ROW_CHUNKS, ROW_CHUNKS), ROW_CHUNKS)]
    return pltpu.make_async_copy(src, dst, sem)


def _interleave_store(ref, row0, value):
    n = value.shape[0]
    for c in range(ROW_CHUNKS):
        ref[pl.ds(row0 * ROW_CHUNKS + c, n, stride=ROW_CHUNKS), :] = value[:, c * LANES:(c + 1) * LANES]


def _interleave_load(ref, row0, n):
    return jnp.concatenate(
        [ref[pl.ds(row0 * ROW_CHUNKS + c, n, stride=ROW_CHUNKS), :] for c in range(ROW_CHUNKS)], axis=1)


def _dispatch_kernel(padlo_ref, padn_ref, dest_ref, x_ref, xs_hbm, xi, ztile, sem, *, ts):
    @pl.when(pl.program_id(0) == 0)
    def _():
        ztile[...] = jnp.zeros_like(ztile)

        def fill(e, c):
            lax.fori_loop(0, padn_ref[e],
                          lambda r, c2: (_tile_copy(ztile, 0, xs_hbm, padlo_ref[e] + r, sem).start(), c2)[1], 0)
            return c

        def fill_done(e, c):
            lax.fori_loop(0, padn_ref[e],
                          lambda r, c2: (_tile_copy(ztile, 0, xs_hbm, 0, sem).wait(), c2)[1], 0)
            return c

        lax.fori_loop(0, N_EXPERTS, fill, 0)
        lax.fori_loop(0, N_EXPERTS, fill_done, 0)

    _interleave_store(xi, 0, x_ref[...])

    def issue(t, c):
        for k in range(TOP_K):
            _tile_copy(xi, t, xs_hbm, dest_ref[TOP_K * t + k], sem).start(priority=k % 2)
        return c

    lax.fori_loop(0, ts, issue, 0)

    def drain(t, c):
        for k in range(TOP_K):
            _tile_copy(xi, 0, xs_hbm, 0, sem).wait()
        return c

    lax.fori_loop(0, ts, drain, 0)


def _dispatch(pad_lo, pad_n, dest_flat, x, rows, ts):
    T = x.shape[0]
    grid_spec = pltpu.PrefetchScalarGridSpec(
        num_scalar_prefetch=2,
        grid=(T // ts,),
        in_specs=[pl.BlockSpec((TOP_K * ts,), lambda i, lo, n: (i,), memory_space=pltpu.SMEM),
                  pl.BlockSpec((ts, D_MODEL), lambda i, lo, n: (i, 0))],
        out_specs=pl.BlockSpec(memory_space=pl.ANY),
        scratch_shapes=[pltpu.VMEM((ts * ROW_CHUNKS, LANES), F32), pltpu.VMEM((ROW_CHUNKS, LANES), F32),
                        pltpu.SemaphoreType.DMA(())],
    )
    return pl.pallas_call(
        functools.partial(_dispatch_kernel, ts=ts),
        grid_spec=grid_spec,
        out_shape=jax.ShapeDtypeStruct((rows * ROW_CHUNKS, LANES), F32),
        compiler_params=_cp(("arbitrary",)),
        name="moe_dispatch",
    )(pad_lo, pad_n, dest_flat, x)


def _experts_kernel(be_ref, nu_ref, xs_ref, w1_ref, b1_ref, w2_ref, b2_ref, y_ref, *, m):
    @pl.when(pl.program_id(0) < nu_ref[0])
    def _():
        gu = _dot(_interleave_load(xs_ref, 0, m).astype(BF16), w1_ref[0]) + b1_ref[0]
        g_ = jnp.minimum(gu[:, :D_FF], SWIGLU_LIMIT)
        up = jnp.clip(gu[:, D_FF:], -SWIGLU_LIMIT, SWIGLU_LIMIT)
        act = (up + 1.0) * g_ * _sigmoid(SWIGLU_ALPHA * g_)
        _interleave_store(y_ref, 0, _dot(act.astype(BF16), w2_ref[0]) + b2_ref[0])

    @pl.when(pl.program_id(0) >= nu_ref[0])
    def _():
        y_ref[...] = jnp.zeros_like(y_ref)


def _experts(block_e, n_used, xs, w1, b1, w2, b2, m):
    nblk = xs.shape[0] // (m * ROW_CHUNKS)
    grid_spec = pltpu.PrefetchScalarGridSpec(
        num_scalar_prefetch=2,
        grid=(nblk,),
        in_specs=[pl.BlockSpec((m * ROW_CHUNKS, LANES), lambda j, be, nu: (jnp.minimum(j, nu[0] - 1), 0)),
                  pl.BlockSpec((1, D_MODEL, 2 * D_FF), lambda j, be, nu: (be[j], 0, 0)),
                  pl.BlockSpec((1, 1, 2 * D_FF), lambda j, be, nu: (be[j], 0, 0)),
                  pl.BlockSpec((1, D_FF, D_MODEL), lambda j, be, nu: (be[j], 0, 0)),
                  pl.BlockSpec((1, 1, D_MODEL), lambda j, be, nu: (be[j], 0, 0))],
        out_specs=pl.BlockSpec((m * ROW_CHUNKS, LANES), lambda j, be, nu: (j, 0)),
    )
    return pl.pallas_call(
        functools.partial(_experts_kernel, m=m),
        grid_spec=grid_spec,
        out_shape=jax.ShapeDtypeStruct(xs.shape, F32),
        compiler_params=_cp(("arbitrary",)),
        name="moe_experts",
    )(block_e, n_used, xs, w1, b1, w2, b2)


def _combine_kernel(dest_a, dest_b, dest_next, x_ref, p_ref, gate_ref, y_hbm, wp_ref, wg_ref, bg_ref,
                    lg_ref, lb_ref, o_ref, ybuf_a, ybuf_b, sem_a, sem_b, *, alpha, tm, halves):
    i = pl.program_id(0)
    bufs = [(ybuf_a, sem_a), (ybuf_b, sem_b)][:halves]
    dests = [dest_a, dest_b][:halves]

    def issue(dest_ref, buf, sem):
        def one(t, c):
            for k in range(TOP_K):
                _tile_copy(y_hbm, dest_ref[TOP_K * t + k], buf, k * tm + t, sem).start(priority=k % 2)
            return c

        lax.fori_loop(0, tm, one, 0)

    def drain(buf, sem):
        def one(t, c):
            for k in range(TOP_K):
                _tile_copy(y_hbm, 0, buf, 0, sem).wait()
            return c

        lax.fori_loop(0, tm, one, 0)

    @pl.when(i == 0)
    def _():
        issue(dests[0], *bufs[0])

    for h in range(halves):
        if h + 1 < halves:
            issue(dests[h + 1], *bufs[h + 1])
        rows = pl.ds(h * tm, tm)
        x = x_ref[rows, :]
        ple = (_sigmoid(_dot(x.astype(BF16), wg_ref[...]) + bg_ref[...])
               * _dot(p_ref[rows, :].astype(BF16), wp_ref[...]))
        r = alpha * x + ple
        drain(*bufs[h])
        for k in range(TOP_K):
            r = r + gate_ref[rows, k:k + 1] * _interleave_load(bufs[h][0], k * tm, tm)
        o_ref[rows, :] = _layer_norm(r, lg_ref[...], lb_ref[...])
        if h == 0:
            @pl.when(i + 1 < pl.num_programs(0))
            def _():
                issue(dest_next, *bufs[0])


def _combine(dest_flat, x, p, gate, y, wp, wg, bg, lg, lb, tm, alpha):
    T = x.shape[0]
    ntiles = T // tm
    halves = 2 if ntiles % 2 == 0 else 1
    tb = halves * tm
    full = lambda shape: pl.BlockSpec(shape, lambda i: (0,) * len(shape))
    dest_spec = lambda fn: pl.BlockSpec((TOP_K * tm,), fn, memory_space=pltpu.SMEM)
    return pl.pallas_call(
        functools.partial(_combine_kernel, alpha=alpha, tm=tm, halves=halves),
        grid=(ntiles // halves,),
        in_specs=[dest_spec(lambda i: (halves * i,)),
                  dest_spec(lambda i: (halves * i + halves - 1,)),
                  dest_spec(lambda i: (jnp.minimum(halves * (i + 1), ntiles - 1),)),
                  pl.BlockSpec((tb, D_MODEL), lambda i: (i, 0)),
                  pl.BlockSpec((tb, D_PLE), lambda i: (i, 0)),
                  pl.BlockSpec((tb, LANES), lambda i: (i, 0)),
                  pl.BlockSpec(memory_space=pl.ANY),
                  full((D_PLE, D_MODEL)), full((D_MODEL, D_MODEL)), full((1, D_MODEL)),
                  full((1, D_MODEL)), full((1, D_MODEL))],
        out_specs=pl.BlockSpec((tb, D_MODEL), lambda i: (i, 0)),
        out_shape=jax.ShapeDtypeStruct((T, D_MODEL), F32),
        scratch_shapes=[pltpu.VMEM((TOP_K * tm * ROW_CHUNKS, LANES), F32),
                        pltpu.VMEM((TOP_K * tm * ROW_CHUNKS, LANES), F32),
                        pltpu.SemaphoreType.DMA(()), pltpu.SemaphoreType.DMA(())],
        compiler_params=_cp(("arbitrary",)),
        name="moe_combine_ln2",
    )(dest_flat, dest_flat, dest_flat, x, p, gate, y, wp, wg, bg, lg, lb)


def _pick(n, pref):
    t = min(n, pref)
    while n % t:
        t //= 2
    return t


def _layer(x, p, lw, conv_buf, lru_h, ml_c, ml_n, ml_m, k_past, v_past, prompt, alpha, layer):
    B, L, _ = x.shape
    T = B * L
    xf = x.reshape(T, D_MODEL)

    z = _inproj(xf, lw["w_in"], lw["b_in"], _pick(T, 1024), 1024)
    k_new, v_new, ifg = _kvif(xf, lw["w_kvif"], lw["b_kvif"], _pick(T, 1024))
    z3 = z.reshape(B, L, N_MAIN)
    k_new = k_new.reshape(B, L, SB_WIDTH)
    v_new = v_new.reshape(B, L, SB_WIDTH)
    conv_new = z3[:, L - (CONV_W - 1):, OFF_AX:OFF_AX + D_RNN].astype(F32)

    cbuf8 = jnp.concatenate([jnp.zeros((B, 8 - (CONV_W - 1), D_RNN), F32), conv_buf.astype(F32)], axis=1)
    ya, lru_new = _lru(z3, cbuf8, lru_h.reshape(B, 1, D_RNN).astype(F32), lw["conv_w8"], lw["conv_b"],
                       lw["wr"], lw["br"], lw["wi"], lw["bi"], lw["lam"], _pick(L, 256), prompt)

    if prompt:
        yb = _sb_attention(z3, k_new[None], v_new[None], k_new, v_new, OFF_Q // SB_HEAD_DIM, 0,
                           _pick(L, 1024), 0, 0)
    else:
        yb = _sb_attention(z3, k_past, v_past, k_new, v_new, OFF_Q // SB_HEAD_DIM, layer,
                           L, k_past.shape[2], L)

    ck = _pick(L, 128)
    yc, c_new, n_new, m_new = _mlstm(z3, ifg.reshape(B, L, LANES), ml_c.astype(F32),
                                     ml_n.astype(F32).reshape(B, ML_HEADS, 1, ML_QK_DIM),
                                     ml_m.astype(F32).reshape(B, ML_HEADS, 1, 1),
                                     lw["ml_g"], _pick(L, 512), ck)
    n_new = n_new.reshape(B, ML_HEADS, ML_QK_DIM)
    m_new = m_new.reshape(B, ML_HEADS)

    x1 = _merge(ya.reshape(T, D_RNN), yb.reshape(T, SB_WIDTH), yc.reshape(T, ML_V_WIDTH), z, xf,
                lw["w_pa"], lw["w_pb"], lw["w_pc"], lw["w_out"], lw["ln1_g"], lw["ln1_b"],
                _pick(T, 512), alpha)

    tr = _pick(T, 512)
    tri = jnp.tril(jnp.ones((tr, tr), BF16), -1)
    topi, gate, rank, cnt = _router(x1, lw["router_w"], lw["router_b"], tri, tr)
    TK = T * TOP_K
    m_rows = max(16, min(512, TK // N_EXPERTS))
    nb = TK // m_rows + N_EXPERTS
    counts = cnt[0, :N_EXPERTS]
    padded = (counts + (m_rows - 1)) // m_rows * m_rows
    pend = jnp.cumsum(padded)
    off = (pend - padded).astype(jnp.int32)
    blk_start = jnp.arange(nb, dtype=jnp.int32) * m_rows
    block_e = jnp.minimum(jnp.sum((pend[None, :] <= blk_start[:, None]).astype(jnp.int32), axis=1),
                          N_EXPERTS - 1).astype(jnp.int32)
    n_used = (pend[-1:] // m_rows).astype(jnp.int32)
    experts = jnp.arange(N_EXPERTS, dtype=jnp.int32)
    top4 = topi[:, :TOP_K]
    dest = rank[:, :TOP_K] + jnp.sum(jnp.where(top4[:, :, None] == experts, off, 0), axis=-1)
    dest_flat = dest.reshape(TK).astype(jnp.int32)
    xs = _dispatch(off + counts, (padded - counts).astype(jnp.int32), dest_flat, x1, nb * m_rows,
                   _pick(T, 256))
    y = _experts(block_e, n_used, xs, lw["exp_w1"], lw["exp_b1"], lw["exp_w2"], lw["exp_b2"], m_rows)
    x2 = _combine(dest_flat, x1, p.reshape(T, D_PLE), gate, y, lw["ple_w"], lw["ple_gate_w"],
                  lw["ple_gate_b"], lw["ln2_g"], lw["ln2_b"], _pick(T, 256), alpha)

    k_out = k_new.reshape(B, L, SB_HEADS, SB_HEAD_DIM)
    v_out = v_new.reshape(B, L, SB_HEADS, SB_HEAD_DIM)
    return x2.reshape(B, L, D_MODEL), (k_out, v_out, conv_new, lru_new.reshape(B, D_RNN), c_new, n_new, m_new)


def _block_diag_groups(w):
    per = LRU_GROUP // LRU_BLOCK
    w4 = w.reshape(D_RNN // LRU_GROUP, per, LRU_BLOCK, LRU_BLOCK)
    eye = jnp.eye(per, dtype=w.dtype)
    return jnp.einsum("gacd,ab->gacbd", w4, eye).reshape(D_RNN // LRU_GROUP, LRU_GROUP, LRU_GROUP)


def _split_in(w):
    main = jnp.concatenate([w[..., :ORIG_K], w[..., ORIG_MQ:ORIG_IF], w[..., ORIG_MG:]], axis=-1)
    pad = jnp.zeros(w.shape[:-1] + (LANES - 2 * ML_HEADS,), w.dtype)
    kvif = jnp.concatenate([w[..., ORIG_K:ORIG_MQ], w[..., ORIG_IF:ORIG_MG], pad], axis=-1)
    return main, kvif


def kernel(x_prompt, x_sample, cache_sb_k, cache_sb_v, state_conv, state_lru, state_mlstm_c, state_mlstm_n, state_mlstm_m, p_prompt, p_sample, w_in, b_in, conv_w, conv_b, lru_wr, lru_br, lru_wi, lru_bi, lru_lambda, ml_norm_g, w_pa, w_pb, w_pc, w_out, ln1_g, ln1_b, router_w, router_b, exp_w1, exp_b1, exp_w2, exp_b2, ple_w, ple_gate_w, ple_gate_b, ln2_g, ln2_b):
    depth = w_in.shape[0]
    alpha = (2 * depth) ** 0.25
    B = x_prompt.shape[0]
    y_prompt, y_sample = x_prompt, x_sample
    k_cache = cache_sb_k.reshape(cache_sb_k.shape[:3] + (SB_WIDTH,)).astype(F32)
    v_cache = cache_sb_v.reshape(cache_sb_v.shape[:3] + (SB_WIDTH,)).astype(F32)
    st_p, st_s = [], []
    for i in range(depth):
        row = lambda a: a[i].reshape(1, -1).astype(F32)
        w_main, w_kvif = _split_in(w_in[i])
        b_main, b_kvif = _split_in(b_in[i])
        lw = dict(
            w_in=w_main.astype(BF16), b_in=b_main.reshape(1, -1).astype(F32),
            w_kvif=w_kvif.astype(BF16), b_kvif=b_kvif.reshape(1, -1).astype(F32),
            conv_w8=jnp.pad(conv_w[i].astype(F32), ((0, 8 - CONV_W), (0, 0))), conv_b=row(conv_b),
            wr=_block_diag_groups(lru_wr[i]).astype(BF16), br=row(lru_br),
            wi=_block_diag_groups(lru_wi[i]).astype(BF16), bi=row(lru_bi), lam=row(lru_lambda),
            ml_g=row(ml_norm_g),
            w_pa=w_pa[i].astype(BF16), w_pb=w_pb[i].astype(BF16), w_pc=w_pc[i].astype(BF16),
            w_out=w_out[i].astype(BF16), ln1_g=row(ln1_g), ln1_b=row(ln1_b),
            router_w=jnp.pad(router_w[i], ((0, 0), (0, LANES - N_EXPERTS))).astype(BF16),
            router_b=jnp.pad(router_b[i].astype(F32), (0, LANES - N_EXPERTS),
                             constant_values=-1e30).reshape(1, LANES),
            exp_w1=exp_w1[i].astype(BF16), exp_b1=exp_b1[i].reshape(N_EXPERTS, 1, 2 * D_FF).astype(F32),
            exp_w2=exp_w2[i].astype(BF16), exp_b2=exp_b2[i].reshape(N_EXPERTS, 1, D_MODEL).astype(F32),
            ple_w=ple_w[i].astype(BF16), ple_gate_w=ple_gate_w[i].astype(BF16), ple_gate_b=row(ple_gate_b),
            ln2_g=row(ln2_g), ln2_b=row(ln2_b),
        )
        y_prompt, sp = _layer(y_prompt, p_prompt[i], lw,
                              jnp.zeros((B, CONV_W - 1, D_RNN), F32), jnp.zeros((B, D_RNN), F32),
                              jnp.zeros((B, ML_HEADS, ML_QK_DIM, ML_V_DIM), F32),
                              jnp.zeros((B, ML_HEADS, ML_QK_DIM), F32), jnp.zeros((B, ML_HEADS), F32),
                              None, None, True, alpha, i)
        y_sample, ss = _layer(y_sample, p_sample[i], lw, state_conv[i], state_lru[i], state_mlstm_c[i],
                              state_mlstm_n[i], state_mlstm_m[i], k_cache, v_cache, False, alpha, i)
        st_p.append(sp)
        st_s.append(ss)
    k_p, v_p, conv_p, lru_p, c_p, n_p, m_p = [jnp.stack(s) for s in zip(*st_p)]
    k_s, v_s, conv_s, lru_s, c_s, n_s, m_s = [jnp.stack(s) for s in zip(*st_s)]
    return (y_prompt, y_sample, k_p, v_p, conv_p, lru_p, c_p, n_p, m_p,
            k_s, v_s, conv_s, lru_s, c_s, n_s, m_s)
```

```python
import functools

import jax
import jax.numpy as jnp
from jax import lax
from jax.experimental import pallas as pl
from jax.experimental.pallas import tpu as pltpu

F32 = jnp.float32
BF16 = jnp.bfloat16

D_MODEL = 1024
D_RNN = 1024
N_LRU_BLOCKS = 16
LRU_BLOCK = D_RNN // N_LRU_BLOCKS
LRU_GROUP = 256
CONV_W = 4
LRU_C = 8.0
SB_HEADS = 4
SB_HEAD_DIM = 128
SB_WIDTH = SB_HEADS * SB_HEAD_DIM
ML_HEADS = 4
ML_QK_DIM = 64
ML_V_DIM = 128
ML_QK_WIDTH = ML_HEADS * ML_QK_DIM
ML_V_WIDTH = ML_HEADS * ML_V_DIM
N_BRANCH = 3
N_EXPERTS = 32
TOP_K = 4
D_FF = 512
SWIGLU_LIMIT = 7.0
SWIGLU_ALPHA = 1.702
D_PLE = 256
LN_EPS = 1e-5
LANES = 128
KEY_BLOCK = 128

OFF_AX, OFF_AG = 0, 1024
OFF_Q = 2048
OFF_MQ, OFF_MK, OFF_MV, OFF_MO = 2560, 2816, 3072, 3584
OFF_MG = 4096
N_MAIN = 7168
N_KVIF = 2 * SB_WIDTH + LANES
ORIG_K, ORIG_MQ, ORIG_IF, ORIG_MG = 2560, 3584, 5120, 5128
SB_DEAD_MASS = 88.0
VMEM_LIMIT = 56 * 1024 * 1024


def _cp(sem, vmem=VMEM_LIMIT):
    return pltpu.CompilerParams(dimension_semantics=sem, vmem_limit_bytes=vmem)


def _dot(a, b):
    return jnp.dot(a, b, preferred_element_type=F32)


def _dot_nt(a, b):
    return lax.dot_general(a, b, (((1,), (1,)), ((), ())), preferred_element_type=F32)


def _dot_tn(a, b):
    return lax.dot_general(a, b, (((0,), (0,)), ((), ())), preferred_element_type=F32)


def _sigmoid(x):
    return 0.5 * jnp.tanh(0.5 * x) + 0.5


def _softplus(x):
    return jnp.maximum(x, 0.0) + jnp.log1p(jnp.exp(-jnp.abs(x)))


def _layer_norm(r, g, b):
    mu = jnp.mean(r, axis=-1, keepdims=True)
    d = r - mu
    var = jnp.mean(d * d, axis=-1, keepdims=True)
    return d * lax.rsqrt(var + LN_EPS) * g + b


def _inproj_kernel(x_ref, w_ref, b_ref, o_ref, xb_ref):
    @pl.when(pl.program_id(1) == 0)
    def _():
        xb_ref[...] = x_ref[...].astype(BF16)

    o_ref[...] = (_dot(xb_ref[...], w_ref[...]) + b_ref[...]).astype(o_ref.dtype)


def _inproj(x, w, b, tm, tn):
    T, D = x.shape
    N = w.shape[1]
    return pl.pallas_call(
        _inproj_kernel,
        grid=(T // tm, N // tn),
        in_specs=[pl.BlockSpec((tm, D), lambda i, j: (i, 0)),
                  pl.BlockSpec((D, tn), lambda i, j: (0, j)),
                  pl.BlockSpec((1, tn), lambda i, j: (0, j))],
        out_specs=pl.BlockSpec((tm, tn), lambda i, j: (i, j)),
        out_shape=jax.ShapeDtypeStruct((T, N), BF16),
        scratch_shapes=[pltpu.VMEM((tm, D), BF16)],
        compiler_params=_cp(("parallel", "arbitrary")),
        name="inproj",
    )(x, w, b)


def _kvif_kernel(x_ref, w_ref, b_ref, kall_ref, vall_ref, k_ref, v_ref, if_ref):
    del kall_ref, vall_ref
    tm = x_ref.shape[0]
    r = _dot(x_ref[...].astype(BF16), w_ref[...]) + b_ref[...]
    for h in range(SB_HEADS):
        k_ref[pl.ds(h, tm, stride=SB_HEADS), :] = r[:, h * SB_HEAD_DIM:(h + 1) * SB_HEAD_DIM]
        v_ref[pl.ds(h, tm, stride=SB_HEADS), :] = r[:, SB_WIDTH + h * SB_HEAD_DIM:SB_WIDTH + (h + 1) * SB_HEAD_DIM]
    if_ref[...] = r[:, 2 * SB_WIDTH:]


def _kvif(x, w, b, k_all, v_all, layer, tm):
    T, D = x.shape
    nt = T // tm
    kv_spec = pl.BlockSpec((SB_HEADS * tm, SB_HEAD_DIM), lambda i: (layer * nt + i, 0))
    return pl.pallas_call(
        _kvif_kernel,
        grid=(nt,),
        in_specs=[pl.BlockSpec((tm, D), lambda i: (i, 0)),
                  pl.BlockSpec((D, N_KVIF), lambda i: (0, 0)),
                  pl.BlockSpec((1, N_KVIF), lambda i: (0, 0)),
                  pl.BlockSpec(memory_space=pl.ANY),
                  pl.BlockSpec(memory_space=pl.ANY)],
        out_specs=[kv_spec, kv_spec, pl.BlockSpec((tm, LANES), lambda i: (i, 0))],
        out_shape=[jax.ShapeDtypeStruct(k_all.shape, F32),
                   jax.ShapeDtypeStruct(v_all.shape, F32),
                   jax.ShapeDtypeStruct((T, LANES), F32)],
        input_output_aliases={3: 0, 4: 1},
        compiler_params=_cp(("parallel",)),
        name="inproj_kvif",
    )(x, w, b, k_all, v_all)


def _unwritten(shape):
    return pl.pallas_call(lambda o_ref: None, out_specs=pl.BlockSpec(memory_space=pl.ANY),
                          out_shape=jax.ShapeDtypeStruct(shape, F32), name="kv_state_buffer")()


def _lru_kernel(ax_ref, ag_ref, cbuf_ref, h0_ref, cw_ref, cb_ref, wr_ref, br_ref, wi_ref, bi_ref,
                lam_ref, ya_ref, hl_ref, ext_ref, hc_ref, *, first_frame):
    tl = ax_ref.shape[1]
    li = pl.program_id(1)

    @pl.when(li == 0)
    def _():
        ext_ref[0:8, :] = cbuf_ref[0]
        hc_ref[...] = h0_ref[0]

    ext_ref[8:8 + tl, :] = ax_ref[0].astype(F32)
    ua = cb_ref[...] + ext_ref[5:5 + tl, :] * cw_ref[0:1, :]
    for j in range(1, CONV_W):
        ua = ua + ext_ref[5 + j:5 + j + tl, :] * cw_ref[j:j + 1, :]
    ext_ref[5:8, :] = ext_ref[5 + tl:8 + tl, :]

    uab = ua.astype(BF16)
    ng = D_RNN // LRU_GROUP
    r_pre = jnp.concatenate(
        [_dot(uab[:, g * LRU_GROUP:(g + 1) * LRU_GROUP], wr_ref[g]) for g in range(ng)], axis=1)
    i_pre = jnp.concatenate(
        [_dot(uab[:, g * LRU_GROUP:(g + 1) * LRU_GROUP], wi_ref[g]) for g in range(ng)], axis=1)
    r = _sigmoid(r_pre + br_ref[...])
    ig = _sigmoid(i_pre + bi_ref[...])
    log_a = (-LRU_C) * r * _softplus(-lam_ref[...])
    a = jnp.exp(log_a)
    th = jnp.tanh(log_a)
    mult = jnp.sqrt(-2.0 * th / (1.0 - th))
    row = lax.broadcasted_iota(jnp.int32, (tl, D_RNN), 0)
    if first_frame:
        mult = jnp.where((row == 0) & (li == 0), 1.0, mult)
    bx = mult * (ig * ua)

    a = a.reshape(tl // 8, 8, D_RNN)
    bx = bx.reshape(tl // 8, 8, D_RNN)
    sub = lax.broadcasted_iota(jnp.int32, (tl // 8, 8, D_RNN), 1)
    for d in (1, 2, 4):
        a_sh = jnp.where(sub < d, 1.0, pltpu.roll(a, d, 1))
        b_sh = jnp.where(sub < d, 0.0, pltpu.roll(bx, d, 1))
        bx = a * b_sh + bx
        a = a * a_sh
    carry = hc_ref[...]
    groups = []
    for g in range(tl // 8):
        hg = bx[g] + a[g] * carry
        carry = hg[7:8, :]
        groups.append(hg)
    h = jnp.concatenate(groups, axis=0)
    hc_ref[...] = carry
    hl_ref[0] = carry

    gx = ag_ref[0].astype(F32)
    gelu = 0.5 * gx * (1.0 + jnp.tanh(0.7978845608028654 * (gx + 0.044715 * (gx * gx * gx))))
    ya_ref[0] = (h * gelu).astype(ya_ref.dtype)


def _lru(z3, cbuf8, h0, cw8, cb, wr, br, wi, bi, lam, tl, first_frame):
    B, S, _ = z3.shape
    full = lambda shape: pl.BlockSpec(shape, lambda b, l: (0,) * len(shape))
    return pl.pallas_call(
        functools.partial(_lru_kernel, first_frame=first_frame),
        grid=(B, S // tl),
        in_specs=[pl.BlockSpec((1, tl, D_RNN), lambda b, l: (b, l, OFF_AX // D_RNN)),
                  pl.BlockSpec((1, tl, D_RNN), lambda b, l: (b, l, OFF_AG // D_RNN)),
                  pl.BlockSpec((1, 8, D_RNN), lambda b, l: (b, 0, 0)),
                  pl.BlockSpec((1, 1, D_RNN), lambda b, l: (b, 0, 0)),
                  full((8, D_RNN)), full((1, D_RNN)),
                  full((D_RNN // LRU_GROUP, LRU_GROUP, LRU_GROUP)), full((1, D_RNN)),
                  full((D_RNN // LRU_GROUP, LRU_GROUP, LRU_GROUP)), full((1, D_RNN)),
                  full((1, D_RNN))],
        out_specs=[pl.BlockSpec((1, tl, D_RNN), lambda b, l: (b, l, 0)),
                   pl.BlockSpec((1, 1, D_RNN), lambda b, l: (b, 0, 0))],
        out_shape=[jax.ShapeDtypeStruct((B, S, D_RNN), BF16),
                   jax.ShapeDtypeStruct((B, 1, D_RNN), F32)],
        scratch_shapes=[pltpu.VMEM((tl + 8, D_RNN), F32), pltpu.VMEM((1, D_RNN), F32)],
        compiler_params=_cp(("parallel", "arbitrary")),
        name="conv_rglru",
    )(z3, z3, cbuf8, h0, cw8, cb, wr, br, wi, bi, lam)


def _sb_kernel(q_ref, k_ref, v_ref, kn_ref, vn_ref, o_ref, acc_ref, c_ref, *, tq, q_off, n_new):
    tk = KEY_BLOCK
    rs = min(tq, tk)
    ns = tq // rs
    qi = pl.program_id(2)
    q = (q_ref[0].astype(F32) * (SB_HEAD_DIM ** -0.5)).astype(BF16)
    qpos = q_off + qi * tq + lax.broadcasted_iota(jnp.int32, (tq, tk), 0)
    lane = lax.broadcasted_iota(jnp.int32, (rs, tk), 1)
    jj = lax.broadcasted_iota(jnp.int32, (tk, tk), 0)
    ll = lax.broadcasted_iota(jnp.int32, (tk, tk), 1)
    suffix = jnp.where(jj >= ll, 1.0, 0.0).astype(BF16)
    acc_ref[...] = jnp.zeros_like(acc_ref)
    c_ref[...] = jnp.zeros_like(c_ref)
    diag = [(q_off + qi * tq + (i + 1) * rs - 2) // tk for i in range(ns)]
    never = jnp.int32(2 ** 30)
    group = lax.broadcasted_iota(jnp.int32, (tq, 1), 0) // rs
    diag_row = jnp.zeros((tq, 1), jnp.int32)
    for i in range(ns):
        diag_row = jnp.where(group == i, diag[i], diag_row)

    def cond(carry):
        s, cmin = carry
        return (s <= diag[ns - 1]) & (cmin < SB_DEAD_MASS)

    head = pl.program_id(1)

    def head_rows(ref, start, n):
        return ref[0, pl.ds(SB_HEADS * start + head, n, stride=SB_HEADS), :]

    def step(s, ks, vs, kpos):
        zs = [_dot_nt(q[i * rs:(i + 1) * rs, :], ks[i]) for i in range(ns)]
        z = jnp.concatenate(zs, axis=0) if ns > 1 else zs[0]
        earlier = (jnp.concatenate(kpos, axis=0) if ns > 1 else kpos[0]) < qpos
        u = jnp.where(earlier, _softplus(z), 0.0)
        u_hi = u.astype(BF16)
        u_lo = (u - u_hi.astype(F32)).astype(BF16)
        suf = _dot(u_hi, suffix) + _dot(u_lo, suffix)
        c = c_ref[...]
        w = jnp.where(earlier, jnp.exp(z - suf - c), 0.0).astype(BF16)
        pv = [_dot(w[i * rs:(i + 1) * rs, :], vs[i]) for i in range(ns)]
        acc_ref[...] += jnp.concatenate(pv, axis=0) if ns > 1 else pv[0]
        c_new = c + suf[:, 0:1]
        c_ref[...] = c_new
        return s + 1, jnp.min(jnp.where(diag_row > s, c_new, jnp.inf))

    def body(carry):
        s, _ = carry
        ks, vs, kpos = [], [], []
        for i in range(ns):
            kb = diag[i] - s
            start = pl.multiple_of(jnp.maximum(kb, 0) * tk, tk)
            ks.append(head_rows(k_ref, start, tk).astype(BF16))
            vs.append(head_rows(v_ref, start, tk).astype(BF16))
            kpos.append(jnp.where(kb >= 0, start, never) + lane)
        return step(s, ks, vs, kpos)

    first = (jnp.int32(0), jnp.float32(0.0))
    if n_new:
        pad = jnp.zeros((tk - n_new, SB_HEAD_DIM), BF16)
        first = step(jnp.int32(0), [jnp.concatenate([head_rows(kn_ref, 0, n_new).astype(BF16), pad], axis=0)],
                     [jnp.concatenate([head_rows(vn_ref, 0, n_new).astype(BF16), pad], axis=0)],
                     [diag[0] * tk + lane])
    lax.while_loop(cond, body, first)
    o_ref[0] = acc_ref[...].astype(o_ref.dtype)


def _sb_attention(q_arr, k_arr, v_arr, kn_arr, vn_arr, q_cb, layer, tq, q_off, n_new):
    B, Tq, _ = q_arr.shape
    Tk = k_arr.shape[1] // SB_HEADS
    nrows = SB_HEADS * (n_new if n_new else 8)
    assert n_new == 0 or (Tq == tq == n_new <= KEY_BLOCK and q_off % KEY_BLOCK == 0 and q_off == Tk)
    return pl.pallas_call(
        functools.partial(_sb_kernel, tq=tq, q_off=q_off, n_new=n_new),
        grid=(B, SB_HEADS, Tq // tq),
        in_specs=[pl.BlockSpec((1, tq, SB_HEAD_DIM), lambda b, h, i: (b, i, q_cb + h)),
                  pl.BlockSpec((1, Tk * SB_HEADS, SB_HEAD_DIM), lambda b, h, i: (layer * B + b, 0, 0)),
                  pl.BlockSpec((1, Tk * SB_HEADS, SB_HEAD_DIM), lambda b, h, i: (layer * B + b, 0, 0)),
                  pl.BlockSpec((1, nrows, SB_HEAD_DIM), lambda b, h, i: (layer * B + b, 0, 0)),
                  pl.BlockSpec((1, nrows, SB_HEAD_DIM), lambda b, h, i: (layer * B + b, 0, 0))],
        out_specs=pl.BlockSpec((1, tq, SB_HEAD_DIM), lambda b, h, i: (b, i, h)),
        out_shape=jax.ShapeDtypeStruct((B, Tq, SB_WIDTH), BF16),
        scratch_shapes=[pltpu.VMEM((tq, SB_HEAD_DIM), F32), pltpu.VMEM((tq, 1), F32)],
        compiler_params=_cp(("parallel", "parallel", "arbitrary")),
        name="stick_breaking",
    )(q_arr, k_arr, v_arr, kn_arr, vn_arr)


def _mlstm_kernel(mq_ref, mk_ref, mv_ref, mo_ref, if_ref, c0_ref, n0_ref, m0_ref, g_ref,
                  yc_ref, c_out, n_out, m_out, c_s, n_s, m_s, *, ck):
    tl = mq_ref.shape[1]
    nh = ML_HEADS
    li = pl.program_id(1)

    @pl.when(li == 0)
    def _():
        c_s[...] = c0_ref[0]
        n_s[...] = n0_ref[0]
        m_s[...] = jnp.broadcast_to(m0_ref[0], (nh, 1, LANES))

    t_i = lax.broadcasted_iota(jnp.int32, (nh, ck, ck), 1)
    s_i = lax.broadcasted_iota(jnp.int32, (nh, ck, ck), 2)
    causal = s_i <= t_i
    diag = s_i == t_i
    gain = jnp.stack([g_ref[:, h * ML_V_DIM:(h + 1) * ML_V_DIM] for h in range(nh)])

    def heads(ref, r0, width):
        return jnp.stack([ref[0, pl.ds(r0, ck), h * width:(h + 1) * width] for h in range(nh)]).astype(F32)

    def chunk(ci, carry):
        r0 = pl.multiple_of(ci * ck, ck)
        ifb = if_ref[0, pl.ds(r0, ck), :]
        ig_col = jnp.stack([ifb[:, h:h + 1] for h in range(nh)])
        lf_col = -_softplus(-jnp.stack([ifb[:, nh + h:nh + h + 1] for h in range(nh)]))
        q4 = heads(mq_ref, r0, ML_QK_DIM)
        k4 = heads(mk_ref, r0, ML_QK_DIM) * (ML_QK_DIM ** -0.5)
        v4 = heads(mv_ref, r0, ML_V_DIM)
        o4 = heads(mo_ref, r0, ML_V_DIM)
        ig_row = jnp.sum(jnp.where(diag, ig_col, 0.0), axis=1, keepdims=True)
        lf_row = jnp.sum(jnp.where(diag, lf_col, 0.0), axis=1, keepdims=True)
        b_col = jnp.sum(jnp.where(causal, lf_row, 0.0), axis=2, keepdims=True)
        b_row = jnp.sum(jnp.where(t_i <= s_i, lf_col, 0.0), axis=1, keepdims=True)
        m_prev = m_s[:, :, 0:1]
        dmat = jnp.where(causal, b_col - b_row + ig_row, -jnp.inf)
        inter = b_col + m_prev
        m_t = jnp.maximum(inter, jnp.max(dmat, axis=2, keepdims=True))
        s_inter = jnp.exp(inter - m_t)
        qb = q4.astype(BF16)
        kb = k4.astype(BF16)
        vb = v4.astype(BF16)
        wqk = jnp.exp(dmat - m_t) * jnp.stack([_dot_nt(qb[h], kb[h]) for h in range(nh)])
        c_prev = c_s[...]
        n_prev = n_s[...]
        cb = c_prev.astype(BF16)
        wb = wqk.astype(BF16)
        num = (s_inter * jnp.stack([_dot(qb[h], cb[h]) for h in range(nh)])
               + jnp.stack([_dot(wb[h], vb[h]) for h in range(nh)]))
        den = (s_inter * jnp.sum(q4 * n_prev, axis=2, keepdims=True)
               + jnp.sum(wqk, axis=2, keepdims=True))
        hh = num / jnp.maximum(jnp.abs(den), jnp.exp(-m_t))
        b_end = b_col[:, ck - 1:ck, :]
        g_col = b_end - b_col + ig_col
        m_new = jnp.maximum(b_end + m_prev, jnp.max(g_col, axis=1, keepdims=True))
        s_old = jnp.exp(b_end + m_prev - m_new)
        kw = k4 * jnp.exp(g_col - m_new)
        kwb = kw.astype(BF16)
        c_s[...] = s_old * c_prev + jnp.stack([_dot_tn(kwb[h], vb[h]) for h in range(nh)])
        n_s[...] = s_old * n_prev + jnp.sum(kw, axis=1, keepdims=True)
        m_s[...] = jnp.broadcast_to(m_new, (nh, 1, LANES))
        mu = jnp.mean(hh, axis=2, keepdims=True)
        dlt = hh - mu
        var = jnp.mean(dlt * dlt, axis=2, keepdims=True)
        out = (dlt * lax.rsqrt(var + LN_EPS) * gain * _sigmoid(o4)).astype(yc_ref.dtype)
        for h in range(nh):
            yc_ref[0, pl.ds(r0, ck), h * ML_V_DIM:(h + 1) * ML_V_DIM] = out[h]
        return carry

    lax.fori_loop(0, tl // ck, chunk, 0)
    c_out[0] = c_s[...]
    n_out[0] = n_s[...]
    m_out[0] = m_s[:, :, 0:1]


def _mlstm(z3, if3, c0, n0, m0p, g, tl, ck):
    B, S, _ = z3.shape
    return pl.pallas_call(
        functools.partial(_mlstm_kernel, ck=ck),
        grid=(B, S // tl),
        in_specs=[pl.BlockSpec((1, tl, ML_QK_WIDTH), lambda b, l: (b, l, OFF_MQ // ML_QK_WIDTH)),
                  pl.BlockSpec((1, tl, ML_QK_WIDTH), lambda b, l: (b, l, OFF_MK // ML_QK_WIDTH)),
                  pl.BlockSpec((1, tl, ML_V_WIDTH), lambda b, l: (b, l, OFF_MV // ML_V_WIDTH)),
                  pl.BlockSpec((1, tl, ML_V_WIDTH), lambda b, l: (b, l, OFF_MO // ML_V_WIDTH)),
                  pl.BlockSpec((1, tl, LANES), lambda b, l: (b, l, 0)),
                  pl.BlockSpec((1, ML_HEADS, ML_QK_DIM, ML_V_DIM), lambda b, l: (b, 0, 0, 0)),
                  pl.BlockSpec((1, ML_HEADS, 1, ML_QK_DIM), lambda b, l: (b, 0, 0, 0)),
                  pl.BlockSpec((1, ML_HEADS, 1, 1), lambda b, l: (b, 0, 0, 0)),
                  pl.BlockSpec((1, ML_V_WIDTH), lambda b, l: (0, 0))],
        out_specs=[pl.BlockSpec((1, tl, ML_V_WIDTH), lambda b, l: (b, l, 0)),
                   pl.BlockSpec((1, ML_HEADS, ML_QK_DIM, ML_V_DIM), lambda b, l: (b, 0, 0, 0)),
                   pl.BlockSpec((1, ML_HEADS, 1, ML_QK_DIM), lambda b, l: (b, 0, 0, 0)),
                   pl.BlockSpec((1, ML_HEADS, 1, 1), lambda b, l: (b, 0, 0, 0))],
        out_shape=[jax.ShapeDtypeStruct((B, S, ML_V_WIDTH), BF16),
                   jax.ShapeDtypeStruct((B, ML_HEADS, ML_QK_DIM, ML_V_DIM), F32),
                   jax.ShapeDtypeStruct((B, ML_HEADS, 1, ML_QK_DIM), F32),
                   jax.ShapeDtypeStruct((B, ML_HEADS, 1, 1), F32)],
        scratch_shapes=[pltpu.VMEM((ML_HEADS, ML_QK_DIM, ML_V_DIM), F32),
                        pltpu.VMEM((ML_HEADS, 1, ML_QK_DIM), F32),
                        pltpu.VMEM((ML_HEADS, 1, LANES), F32)],
        compiler_params=_cp(("parallel", "arbitrary")),
        name="mlstm",
    )(z3, z3, z3, z3, if3, c0, n0, m0p, g)


def _merge_kernel(ya_ref, yb_ref, yc_ref, g0_ref, g1_ref, g2_ref, x_ref, wpa_ref, wpb_ref, wpc_ref,
                  wo_ref, lg_ref, lb_ref, o_ref, *, alpha):
    mixed = _sigmoid(g0_ref[...].astype(F32)) * _dot(ya_ref[...], wpa_ref[...])
    mixed = mixed + _sigmoid(g1_ref[...].astype(F32)) * _dot(yb_ref[...], wpb_ref[...])
    mixed = mixed + _sigmoid(g2_ref[...].astype(F32)) * _dot(yc_ref[...], wpc_ref[...])
    r = alpha * x_ref[...] + _dot(mixed.astype(BF16), wo_ref[...])
    o_ref[...] = _layer_norm(r, lg_ref[...], lb_ref[...])


def _merge(ya, yb, yc, z, x, wpa, wpb, wpc, wo, lg, lb, tm, alpha):
    T = x.shape[0]
    full = lambda shape: pl.BlockSpec(shape, lambda i: (0,) * len(shape))
    mgb = OFF_MG // D_MODEL
    return pl.pallas_call(
        functools.partial(_merge_kernel, alpha=alpha),
        grid=(T // tm,),
        in_specs=[pl.BlockSpec((tm, D_RNN), lambda i: (i, 0)),
                  pl.BlockSpec((tm, SB_WIDTH), lambda i: (i, 0)),
                  pl.BlockSpec((tm, ML_V_WIDTH), lambda i: (i, 0)),
                  pl.BlockSpec((tm, D_MODEL), lambda i: (i, mgb)),
                  pl.BlockSpec((tm, D_MODEL), lambda i: (i, mgb + 1)),
                  pl.BlockSpec((tm, D_MODEL), lambda i: (i, mgb + 2)),
                  pl.BlockSpec((tm, D_MODEL), lambda i: (i, 0)),
                  full((D_RNN, D_MODEL)), full((SB_WIDTH, D_MODEL)), full((ML_V_WIDTH, D_MODEL)),
                  full((D_MODEL, D_MODEL)), full((1, D_MODEL)), full((1, D_MODEL))],
        out_specs=pl.BlockSpec((tm, D_MODEL), lambda i: (i, 0)),
        out_shape=jax.ShapeDtypeStruct((T, D_MODEL), F32),
        compiler_params=_cp(("parallel",)),
        name="merge_ln1",
    )(ya, yb, yc, z, z, z, x, wpa, wpb, wpc, wo, lg, lb)


def _router_kernel(x_ref, rw_ref, rb_ref, tri_ref, topi_ref, gate_ref, rank_ref, cnt_ref, run_ref):
    @pl.when(pl.program_id(0) == 0)
    def _():
        run_ref[...] = jnp.zeros_like(run_ref)

    tm = x_ref.shape[0]
    l = _dot(x_ref[...].astype(BF16), rw_ref[...]) + rb_ref[...]
    lane = lax.broadcasted_iota(jnp.int32, (tm, LANES), 1)
    lane_f = lane.astype(F32)
    vals, idxs, hots = [], [], []
    for _ in range(TOP_K):
        m = jnp.max(l, axis=1, keepdims=True)
        idx = jnp.min(jnp.where(l == m, lane_f, float(LANES)), axis=1, keepdims=True)
        hot = lane_f == idx
        vals.append(m)
        idxs.append(idx)
        hots.append(hot)
        l = jnp.where(hot, -jnp.inf, l)
    ex = [jnp.exp(v - vals[0]) for v in vals]
    den = ex[0] + ex[1] + ex[2] + ex[3]
    onehot = jnp.zeros((tm, LANES), F32)
    for hot in hots:
        onehot = onehot + jnp.where(hot, 1.0, 0.0)
    before = _dot(tri_ref[...], onehot.astype(BF16)) + run_ref[...]
    topi = jnp.zeros((tm, LANES), F32)
    gate = jnp.zeros((tm, LANES), F32)
    rank = jnp.zeros((tm, LANES), F32)
    for k in range(TOP_K):
        rk = jnp.sum(jnp.where(hots[k], before, 0.0), axis=1, keepdims=True)
        topi = jnp.where(lane == k, idxs[k], topi)
        gate = jnp.where(lane == k, ex[k] / den, gate)
        rank = jnp.where(lane == k, rk, rank)
    run_new = run_ref[...] + jnp.sum(onehot, axis=0, keepdims=True)
    run_ref[...] = run_new
    topi_ref[...] = topi.astype(jnp.int32)
    gate_ref[...] = gate
    rank_ref[...] = rank.astype(jnp.int32)
    cnt_ref[...] = run_new.astype(jnp.int32)


def _router(x, rw, rb, tri, tm):
    T = x.shape[0]
    full = lambda shape: pl.BlockSpec(shape, lambda i: (0,) * len(shape))
    return pl.pallas_call(
        _router_kernel,
        grid=(T // tm,),
        in_specs=[pl.BlockSpec((tm, D_MODEL), lambda i: (i, 0)),
                  full((D_MODEL, LANES)), full((1, LANES)), full((tm, tm))],
        out_specs=[pl.BlockSpec((tm, LANES), lambda i: (i, 0)),
                   pl.BlockSpec((tm, LANES), lambda i: (i, 0)),
                   pl.BlockSpec((tm, LANES), lambda i: (i, 0)),
                   full((1, LANES))],
        out_shape=[jax.ShapeDtypeStruct((T, LANES), jnp.int32),
                   jax.ShapeDtypeStruct((T, LANES), F32),
                   jax.ShapeDtypeStruct((T, LANES), jnp.int32),
                   jax.ShapeDtypeStruct((1, LANES), jnp.int32)],
        scratch_shapes=[pltpu.VMEM((1, LANES), F32)],
        compiler_params=_cp(("arbitrary",)),
        name="router",
    )(x, rw, rb, tri)


ROW_CHUNKS = D_MODEL // LANES


def _tile_copy(src_ref, src_row, dst_ref, dst_row, sem):
    src = src_ref.at[pl.ds(pl.multiple_of(src_row * ROW_CHUNKS, ROW_CHUNKS), ROW_CHUNKS)]
    dst = dst_ref.at[pl.ds(pl.multiple_of(dst_row * ROW_CHUNKS, ROW_CHUNKS), ROW_CHUNKS)]
    return pltpu.make_async_copy(src, dst, sem)


def _interleave_store(ref, row0, value):
    n = value.shape[0]
    for c in range(ROW_CHUNKS):
        ref[pl.ds(row0 * ROW_CHUNKS + c, n, stride=ROW_CHUNKS), :] = value[:, c * LANES:(c + 1) * LANES]


def _interleave_load(ref, row0, n):
    return jnp.concatenate(
        [ref[pl.ds(row0 * ROW_CHUNKS + c, n, stride=ROW_CHUNKS), :] for c in range(ROW_CHUNKS)], axis=1)


def _dispatch_kernel(padlo_ref, padn_ref, dest_ref, x_ref, xs_hbm, xi, ztile, sem, *, ts):
    @pl.when(pl.program_id(0) == 0)
    def _():
        ztile[...] = jnp.zeros_like(ztile)

        def fill(e, c):
            lax.fori_loop(0, padn_ref[e],
                          lambda r, c2: (_tile_copy(ztile, 0, xs_hbm, padlo_ref[e] + r, sem).start(), c2)[1], 0)
            return c

        def fill_done(e, c):
            lax.fori_loop(0, padn_ref[e],
                          lambda r, c2: (_tile_copy(ztile, 0, xs_hbm, 0, sem).wait(), c2)[1], 0)
            return c

        lax.fori_loop(0, N_EXPERTS, fill, 0)
        lax.fori_loop(0, N_EXPERTS, fill_done, 0)

    _interleave_store(xi, 0, x_ref[...])

    def issue(t, c):
        for k in range(TOP_K):
            _tile_copy(xi, t, xs_hbm, dest_ref[TOP_K * t + k], sem).start(priority=k % 2)
        return c

    lax.fori_loop(0, ts, issue, 0)

    def drain(t, c):
        for k in range(TOP_K):
            _tile_copy(xi, 0, xs_hbm, 0, sem).wait()
        return c

    lax.fori_loop(0, ts, drain, 0)


def _dispatch(pad_lo, pad_n, dest_flat, x, rows, ts):
    T = x.shape[0]
    grid_spec = pltpu.PrefetchScalarGridSpec(
        num_scalar_prefetch=2,
        grid=(T // ts,),
        in_specs=[pl.BlockSpec((TOP_K * ts,), lambda i, lo, n: (i,), memory_space=pltpu.SMEM),
                  pl.BlockSpec((ts, D_MODEL), lambda i, lo, n: (i, 0))],
        out_specs=pl.BlockSpec(memory_space=pl.ANY),
        scratch_shapes=[pltpu.VMEM((ts * ROW_CHUNKS, LANES), F32), pltpu.VMEM((ROW_CHUNKS, LANES), F32),
                        pltpu.SemaphoreType.DMA(())],
    )
    return pl.pallas_call(
        functools.partial(_dispatch_kernel, ts=ts),
        grid_spec=grid_spec,
        out_shape=jax.ShapeDtypeStruct((rows * ROW_CHUNKS, LANES), F32),
        compiler_params=_cp(("arbitrary",)),
        name="moe_dispatch",
    )(pad_lo, pad_n, dest_flat, x)


def _experts_kernel(be_ref, nu_ref, xs_ref, w1_ref, b1_ref, w2_ref, b2_ref, y_ref, *, m):
    @pl.when(pl.program_id(0) < nu_ref[0])
    def _():
        gu = _dot(_interleave_load(xs_ref, 0, m).astype(BF16), w1_ref[0]) + b1_ref[0]
        g_ = jnp.minimum(gu[:, :D_FF], SWIGLU_LIMIT)
        up = jnp.clip(gu[:, D_FF:], -SWIGLU_LIMIT, SWIGLU_LIMIT)
        act = (up + 1.0) * g_ * _sigmoid(SWIGLU_ALPHA * g_)
        _interleave_store(y_ref, 0, _dot(act.astype(BF16), w2_ref[0]) + b2_ref[0])

    @pl.when(pl.program_id(0) >= nu_ref[0])
    def _():
        y_ref[...] = jnp.zeros_like(y_ref)


def _experts(block_e, n_used, xs, w1, b1, w2, b2, m):
    nblk = xs.shape[0] // (m * ROW_CHUNKS)
    grid_spec = pltpu.PrefetchScalarGridSpec(
        num_scalar_prefetch=2,
        grid=(nblk,),
        in_specs=[pl.BlockSpec((m * ROW_CHUNKS, LANES), lambda j, be, nu: (jnp.minimum(j, nu[0] - 1), 0)),
                  pl.BlockSpec((1, D_MODEL, 2 * D_FF), lambda j, be, nu: (be[j], 0, 0)),
                  pl.BlockSpec((1, 1, 2 * D_FF), lambda j, be, nu: (be[j], 0, 0)),
                  pl.BlockSpec((1, D_FF, D_MODEL), lambda j, be, nu: (be[j], 0, 0)),
                  pl.BlockSpec((1, 1, D_MODEL), lambda j, be, nu: (be[j], 0, 0))],
        out_specs=pl.BlockSpec((m * ROW_CHUNKS, LANES), lambda j, be, nu: (j, 0)),
    )
    return pl.pallas_call(
        functools.partial(_experts_kernel, m=m),
        grid_spec=grid_spec,
        out_shape=jax.ShapeDtypeStruct(xs.shape, F32),
        compiler_params=_cp(("arbitrary",)),
        name="moe_experts",
    )(block_e, n_used, xs, w1, b1, w2, b2)


def _combine_kernel(dest_a, dest_b, dest_next, x_ref, p_ref, gate_ref, y_hbm, wp_ref, wg_ref, bg_ref,
                    lg_ref, lb_ref, o_ref, ybuf_a, ybuf_b, sem_a, sem_b, *, alpha, tm, halves):
    i = pl.program_id(0)
    bufs = [(ybuf_a, sem_a), (ybuf_b, sem_b)][:halves]
    dests = [dest_a, dest_b][:halves]

    def issue(dest_ref, buf, sem):
        def one(t, c):
            for k in range(TOP_K):
                _tile_copy(y_hbm, dest_ref[TOP_K * t + k], buf, k * tm + t, sem).start(priority=k % 2)
            return c

        lax.fori_loop(0, tm, one, 0)

    def drain(buf, sem):
        def one(t, c):
            for k in range(TOP_K):
                _tile_copy(y_hbm, 0, buf, 0, sem).wait()
            return c

        lax.fori_loop(0, tm, one, 0)

    @pl.when(i == 0)
    def _():
        issue(dests[0], *bufs[0])

    for h in range(halves):
        if h + 1 < halves:
            issue(dests[h + 1], *bufs[h + 1])
        rows = pl.ds(h * tm, tm)
        x = x_ref[rows, :]
        ple = (_sigmoid(_dot(x.astype(BF16), wg_ref[...]) + bg_ref[...])
               * _dot(p_ref[rows, :].astype(BF16), wp_ref[...]))
        r = alpha * x + ple
        drain(*bufs[h])
        for k in range(TOP_K):
            r = r + gate_ref[rows, k:k + 1] * _interleave_load(bufs[h][0], k * tm, tm)
        o_ref[rows, :] = _layer_norm(r, lg_ref[...], lb_ref[...])
        if h == 0:
            @pl.when(i + 1 < pl.num_programs(0))
            def _():
                issue(dest_next, *bufs[0])


def _combine(dest_flat, x, p, gate, y, wp, wg, bg, lg, lb, tm, alpha):
    T = x.shape[0]
    ntiles = T // tm
    halves = 2 if ntiles % 2 == 0 else 1
    tb = halves * tm
    full = lambda shape: pl.BlockSpec(shape, lambda i: (0,) * len(shape))
    dest_spec = lambda fn: pl.BlockSpec((TOP_K * tm,), fn, memory_space=pltpu.SMEM)
    return pl.pallas_call(
        functools.partial(_combine_kernel, alpha=alpha, tm=tm, halves=halves),
        grid=(ntiles // halves,),
        in_specs=[dest_spec(lambda i: (halves * i,)),
                  dest_spec(lambda i: (halves * i + halves - 1,)),
                  dest_spec(lambda i: (jnp.minimum(halves * (i + 1), ntiles - 1),)),
                  pl.BlockSpec((tb, D_MODEL), lambda i: (i, 0)),
                  pl.BlockSpec((tb, D_PLE), lambda i: (i, 0)),
                  pl.BlockSpec((tb, LANES), lambda i: (i, 0)),
                  pl.BlockSpec(memory_space=pl.ANY),
                  full((D_PLE, D_MODEL)), full((D_MODEL, D_MODEL)), full((1, D_MODEL)),
                  full((1, D_MODEL)), full((1, D_MODEL))],
        out_specs=pl.BlockSpec((tb, D_MODEL), lambda i: (i, 0)),
        out_shape=jax.ShapeDtypeStruct((T, D_MODEL), F32),
        scratch_shapes=[pltpu.VMEM((TOP_K * tm * ROW_CHUNKS, LANES), F32),
                        pltpu.VMEM((TOP_K * tm * ROW_CHUNKS, LANES), F32),
                        pltpu.SemaphoreType.DMA(()), pltpu.SemaphoreType.DMA(())],
        compiler_params=_cp(("arbitrary",)),
        name="moe_combine_ln2",
    )(dest_flat, dest_flat, dest_flat, x, p, gate, y, wp, wg, bg, lg, lb)


def _pick(n, pref):
    t = min(n, pref)
    while n % t:
        t //= 2
    return t


def _layer(x, p, lw, conv_buf, lru_h, ml_c, ml_n, ml_m, k_past, v_past, k_all, v_all, prompt, alpha, layer):
    B, L, _ = x.shape
    T = B * L
    xf = x.reshape(T, D_MODEL)

    z = _inproj(xf, lw["w_in"], lw["b_in"], _pick(T, 1024), 1024)
    k_all, v_all, ifg = _kvif(xf, lw["w_kvif"], lw["b_kvif"], k_all, v_all, layer, _pick(T, 1024))
    z3 = z.reshape(B, L, N_MAIN)
    k_rows = k_all.reshape(-1, L * SB_HEADS, SB_HEAD_DIM)
    v_rows = v_all.reshape(-1, L * SB_HEADS, SB_HEAD_DIM)
    conv_new = z3[:, L - (CONV_W - 1):, OFF_AX:OFF_AX + D_RNN].astype(F32)

    cbuf8 = jnp.concatenate([jnp.zeros((B, 8 - (CONV_W - 1), D_RNN), F32), conv_buf.astype(F32)], axis=1)
    ya, lru_new = _lru(z3, cbuf8, lru_h.reshape(B, 1, D_RNN).astype(F32), lw["conv_w8"], lw["conv_b"],
                       lw["wr"], lw["br"], lw["wi"], lw["bi"], lw["lam"], _pick(L, 256), prompt)

    if prompt:
        yb = _sb_attention(z3, k_rows, v_rows, k_rows, v_rows, OFF_Q // SB_HEAD_DIM, layer,
                           _pick(L, 1024), 0, 0)
    else:
        yb = _sb_attention(z3, k_past, v_past, k_rows, v_rows, OFF_Q // SB_HEAD_DIM, layer,
                           L, k_past.shape[1] // SB_HEADS, L)

    ck = _pick(L, 128)
    yc, c_new, n_new, m_new = _mlstm(z3, ifg.reshape(B, L, LANES), ml_c.astype(F32),
                                     ml_n.astype(F32).reshape(B, ML_HEADS, 1, ML_QK_DIM),
                                     ml_m.astype(F32).reshape(B, ML_HEADS, 1, 1),
                                     lw["ml_g"], _pick(L, 512), ck)
    n_new = n_new.reshape(B, ML_HEADS, ML_QK_DIM)
    m_new = m_new.reshape(B, ML_HEADS)

    x1 = _merge(ya.reshape(T, D_RNN), yb.reshape(T, SB_WIDTH), yc.reshape(T, ML_V_WIDTH), z, xf,
                lw["w_pa"], lw["w_pb"], lw["w_pc"], lw["w_out"], lw["ln1_g"], lw["ln1_b"],
                _pick(T, 512), alpha)

    tr = _pick(T, 512)
    tri = jnp.tril(jnp.ones((tr, tr), BF16), -1)
    topi, gate, rank, cnt = _router(x1, lw["router_w"], lw["router_b"], tri, tr)
    TK = T * TOP_K
    m_rows = max(16, min(512, TK // N_EXPERTS))
    nb = TK // m_rows + N_EXPERTS
    counts = cnt[0, :N_EXPERTS]
    padded = (counts + (m_rows - 1)) // m_rows * m_rows
    pend = jnp.cumsum(padded)
    off = (pend - padded).astype(jnp.int32)
    blk_start = jnp.arange(nb, dtype=jnp.int32) * m_rows
    block_e = jnp.minimum(jnp.sum((pend[None, :] <= blk_start[:, None]).astype(jnp.int32), axis=1),
                          N_EXPERTS - 1).astype(jnp.int32)
    n_used = (pend[-1:] // m_rows).astype(jnp.int32)
    experts = jnp.arange(N_EXPERTS, dtype=jnp.int32)
    top4 = topi[:, :TOP_K]
    dest = rank[:, :TOP_K] + jnp.sum(jnp.where(top4[:, :, None] == experts, off, 0), axis=-1)
    dest_flat = dest.reshape(TK).astype(jnp.int32)
    xs = _dispatch(off + counts, (padded - counts).astype(jnp.int32), dest_flat, x1, nb * m_rows,
                   _pick(T, 256))
    y = _experts(block_e, n_used, xs, lw["exp_w1"], lw["exp_b1"], lw["exp_w2"], lw["exp_b2"], m_rows)
    x2 = _combine(dest_flat, x1, p.reshape(T, D_PLE), gate, y, lw["ple_w"], lw["ple_gate_w"],
                  lw["ple_gate_b"], lw["ln2_g"], lw["ln2_b"], _pick(T, 256), alpha)

    return (x2.reshape(B, L, D_MODEL), k_all, v_all,
            (conv_new, lru_new.reshape(B, D_RNN), c_new, n_new, m_new))


def _block_diag_groups(w):
    per = LRU_GROUP // LRU_BLOCK
    w4 = w.reshape(D_RNN // LRU_GROUP, per, LRU_BLOCK, LRU_BLOCK)
    eye = jnp.eye(per, dtype=w.dtype)
    return jnp.einsum("gacd,ab->gacbd", w4, eye).reshape(D_RNN // LRU_GROUP, LRU_GROUP, LRU_GROUP)


def _split_in(w):
    main = jnp.concatenate([w[..., :ORIG_K], w[..., ORIG_MQ:ORIG_IF], w[..., ORIG_MG:]], axis=-1)
    pad = jnp.zeros(w.shape[:-1] + (LANES - 2 * ML_HEADS,), w.dtype)
    kvif = jnp.concatenate([w[..., ORIG_K:ORIG_MQ], w[..., ORIG_IF:ORIG_MG], pad], axis=-1)
    return main, kvif


def kernel(x_prompt, x_sample, cache_sb_k, cache_sb_v, state_conv, state_lru, state_mlstm_c, state_mlstm_n, state_mlstm_m, p_prompt, p_sample, w_in, b_in, conv_w, conv_b, lru_wr, lru_br, lru_wi, lru_bi, lru_lambda, ml_norm_g, w_pa, w_pb, w_pc, w_out, ln1_g, ln1_b, router_w, router_b, exp_w1, exp_b1, exp_w2, exp_b2, ple_w, ple_gate_w, ple_gate_b, ln2_g, ln2_b):
    depth = w_in.shape[0]
    alpha = (2 * depth) ** 0.25
    B = x_prompt.shape[0]
    y_prompt, y_sample = x_prompt, x_sample
    S = x_prompt.shape[1]
    Bs, Ls = x_sample.shape[:2]
    P = cache_sb_k.shape[2]
    k_cache = cache_sb_k.astype(F32).reshape(depth * Bs, P * SB_HEADS, SB_HEAD_DIM)
    v_cache = cache_sb_v.astype(F32).reshape(depth * Bs, P * SB_HEADS, SB_HEAD_DIM)
    kp, vp = (_unwritten((depth * B * S * SB_HEADS, SB_HEAD_DIM)) for _ in range(2))
    ks, vs = (_unwritten((depth * Bs * Ls * SB_HEADS, SB_HEAD_DIM)) for _ in range(2))
    st_p, st_s = [], []
    for i in range(depth):
        row = lambda a: a[i].reshape(1, -1).astype(F32)
        w_main, w_kvif = _split_in(w_in[i])
        b_main, b_kvif = _split_in(b_in[i])
        lw = dict(
            w_in=w_main.astype(BF16), b_in=b_main.reshape(1, -1).astype(F32),
            w_kvif=w_kvif.astype(BF16), b_kvif=b_kvif.reshape(1, -1).astype(F32),
            conv_w8=jnp.pad(conv_w[i].astype(F32), ((0, 8 - CONV_W), (0, 0))), conv_b=row(conv_b),
            wr=_block_diag_groups(lru_wr[i]).astype(BF16), br=row(lru_br),
            wi=_block_diag_groups(lru_wi[i]).astype(BF16), bi=row(lru_bi), lam=row(lru_lambda),
            ml_g=row(ml_norm_g),
            w_pa=w_pa[i].astype(BF16), w_pb=w_pb[i].astype(BF16), w_pc=w_pc[i].astype(BF16),
            w_out=w_out[i].astype(BF16), ln1_g=row(ln1_g), ln1_b=row(ln1_b),
            router_w=jnp.pad(router_w[i], ((0, 0), (0, LANES - N_EXPERTS))).astype(BF16),
            router_b=jnp.pad(router_b[i].astype(F32), (0, LANES - N_EXPERTS),
                             constant_values=-1e30).reshape(1, LANES),
            exp_w1=exp_w1[i].astype(BF16), exp_b1=exp_b1[i].reshape(N_EXPERTS, 1, 2 * D_FF).astype(F32),
            exp_w2=exp_w2[i].astype(BF16), exp_b2=exp_b2[i].reshape(N_EXPERTS, 1, D_MODEL).astype(F32),
            ple_w=ple_w[i].astype(BF16), ple_gate_w=ple_gate_w[i].astype(BF16), ple_gate_b=row(ple_gate_b),
            ln2_g=row(ln2_g), ln2_b=row(ln2_b),
        )
        y_prompt, kp, vp, sp = _layer(y_prompt, p_prompt[i], lw,
                                      jnp.zeros((B, CONV_W - 1, D_RNN), F32), jnp.zeros((B, D_RNN), F32),
                                      jnp.zeros((B, ML_HEADS, ML_QK_DIM, ML_V_DIM), F32),
                                      jnp.zeros((B, ML_HEADS, ML_QK_DIM), F32), jnp.zeros((B, ML_HEADS), F32),
                                      None, None, kp, vp, True, alpha, i)
        y_sample, ks, vs, ss = _layer(y_sample, p_sample[i], lw, state_conv[i], state_lru[i], state_mlstm_c[i],
                                      state_mlstm_n[i], state_mlstm_m[i], k_cache, v_cache, ks, vs,
                                      False, alpha, i)
        st_p.append(sp)
        st_s.append(ss)
    conv_p, lru_p, c_p, n_p, m_p = [jnp.stack(s) for s in zip(*st_p)]
    conv_s, lru_s, c_s, n_s, m_s = [jnp.stack(s) for s in zip(*st_s)]
    kv_p = (depth, B, S, SB_HEADS, SB_HEAD_DIM)
    kv_s = (depth, Bs, Ls, SB_HEADS, SB_HEAD_DIM)
    return (y_prompt, y_sample, kp.reshape(kv_p), vp.reshape(kv_p), conv_p, lru_p, c_p, n_p, m_p,
            ks.reshape(kv_s), vs.reshape(kv_s), conv_s, lru_s, c_s, n_s, m_s)
```

```python
import functools

import jax
import jax.numpy as jnp
from jax import lax
from jax.experimental import pallas as pl
from jax.experimental.pallas import tpu as pltpu

F32 = jnp.float32
BF16 = jnp.bfloat16

D_MODEL = 1024
D_RNN = 1024
N_LRU_BLOCKS = 16
LRU_BLOCK = D_RNN // N_LRU_BLOCKS
LRU_GROUP = 256
CONV_W = 4
LRU_C = 8.0
SB_HEADS = 4
SB_HEAD_DIM = 128
SB_WIDTH = SB_HEADS * SB_HEAD_DIM
ML_HEADS = 4
ML_QK_DIM = 64
ML_V_DIM = 128
ML_QK_WIDTH = ML_HEADS * ML_QK_DIM
ML_V_WIDTH = ML_HEADS * ML_V_DIM
N_BRANCH = 3
N_EXPERTS = 32
TOP_K = 4
D_FF = 512
SWIGLU_LIMIT = 7.0
SWIGLU_ALPHA = 1.702
D_PLE = 256
LN_EPS = 1e-5
LANES = 128
KEY_BLOCK = 128

OFF_AX, OFF_AG = 0, 1024
OFF_Q = 2048
OFF_MQ, OFF_MK, OFF_MV, OFF_MO = 2560, 2816, 3072, 3584
OFF_MG = 4096
N_MAIN = 7168
N_KVIF = 2 * SB_WIDTH + LANES
ORIG_K, ORIG_MQ, ORIG_IF, ORIG_MG = 2560, 3584, 5120, 5128
SB_DEAD_MASS = 88.0
VMEM_LIMIT = 56 * 1024 * 1024


def _cp(sem, vmem=VMEM_LIMIT):
    return pltpu.CompilerParams(dimension_semantics=sem, vmem_limit_bytes=vmem)


def _dot(a, b):
    return jnp.dot(a, b, preferred_element_type=F32)


def _dot_nt(a, b):
    return lax.dot_general(a, b, (((1,), (1,)), ((), ())), preferred_element_type=F32)


def _dot_tn(a, b):
    return lax.dot_general(a, b, (((0,), (0,)), ((), ())), preferred_element_type=F32)


def _sigmoid(x):
    return 0.5 * jnp.tanh(0.5 * x) + 0.5


def _softplus(x):
    return jnp.maximum(x, 0.0) + jnp.log1p(jnp.exp(-jnp.abs(x)))


def _layer_norm(r, g, b):
    mu = jnp.mean(r, axis=-1, keepdims=True)
    d = r - mu
    var = jnp.mean(d * d, axis=-1, keepdims=True)
    return d * lax.rsqrt(var + LN_EPS) * g + b


def _inproj_kernel(x_ref, w_ref, b_ref, o_ref, xb_ref):
    @pl.when(pl.program_id(1) == 0)
    def _():
        xb_ref[...] = x_ref[...].astype(BF16)

    o_ref[...] = (_dot(xb_ref[...], w_ref[...]) + b_ref[...]).astype(o_ref.dtype)


def _inproj(x, w, b, tm, tn):
    T, D = x.shape
    N = w.shape[1]
    return pl.pallas_call(
        _inproj_kernel,
        grid=(T // tm, N // tn),
        in_specs=[pl.BlockSpec((tm, D), lambda i, j: (i, 0)),
                  pl.BlockSpec((D, tn), lambda i, j: (0, j)),
                  pl.BlockSpec((1, tn), lambda i, j: (0, j))],
        out_specs=pl.BlockSpec((tm, tn), lambda i, j: (i, j)),
        out_shape=jax.ShapeDtypeStruct((T, N), BF16),
        scratch_shapes=[pltpu.VMEM((tm, D), BF16)],
        compiler_params=_cp(("parallel", "arbitrary")),
        name="inproj",
    )(x, w, b)


def _kvif_kernel(x_ref, w_ref, b_ref, kall_ref, vall_ref, k_ref, v_ref, if_ref):
    del kall_ref, vall_ref
    tm = x_ref.shape[0]
    r = _dot(x_ref[...].astype(BF16), w_ref[...]) + b_ref[...]
    for h in range(SB_HEADS):
        k_ref[pl.ds(h, tm, stride=SB_HEADS), :] = r[:, h * SB_HEAD_DIM:(h + 1) * SB_HEAD_DIM]
        v_ref[pl.ds(h, tm, stride=SB_HEADS), :] = r[:, SB_WIDTH + h * SB_HEAD_DIM:SB_WIDTH + (h + 1) * SB_HEAD_DIM]
    if_ref[...] = r[:, 2 * SB_WIDTH:]


def _kvif(x, w, b, k_all, v_all, layer, tm):
    T, D = x.shape
    nt = T // tm
    kv_spec = pl.BlockSpec((SB_HEADS * tm, SB_HEAD_DIM), lambda i: (layer * nt + i, 0))
    return pl.pallas_call(
        _kvif_kernel,
        grid=(nt,),
        in_specs=[pl.BlockSpec((tm, D), lambda i: (i, 0)),
                  pl.BlockSpec((D, N_KVIF), lambda i: (0, 0)),
                  pl.BlockSpec((1, N_KVIF), lambda i: (0, 0)),
                  pl.BlockSpec(memory_space=pl.ANY),
                  pl.BlockSpec(memory_space=pl.ANY)],
        out_specs=[kv_spec, kv_spec, pl.BlockSpec((tm, LANES), lambda i: (i, 0))],
        out_shape=[jax.ShapeDtypeStruct(k_all.shape, F32),
                   jax.ShapeDtypeStruct(v_all.shape, F32),
                   jax.ShapeDtypeStruct((T, LANES), F32)],
        input_output_aliases={3: 0, 4: 1},
        compiler_params=_cp(("parallel",)),
        name="inproj_kvif",
    )(x, w, b, k_all, v_all)


def _unwritten(shape):
    return pl.pallas_call(lambda o_ref: None, out_specs=pl.BlockSpec(memory_space=pl.ANY),
                          out_shape=jax.ShapeDtypeStruct(shape, F32), name="kv_state_buffer")()


def _lru_kernel(ax_ref, ag_ref, cbuf_ref, h0_ref, cw_ref, cb_ref, wr_ref, br_ref, wi_ref, bi_ref,
                lam_ref, ya_ref, hl_ref, ext_ref, hc_ref, *, first_frame):
    tl = ax_ref.shape[1]
    li = pl.program_id(1)

    @pl.when(li == 0)
    def _():
        ext_ref[0:8, :] = cbuf_ref[0]
        hc_ref[...] = h0_ref[0]

    ext_ref[8:8 + tl, :] = ax_ref[0].astype(F32)
    ua = cb_ref[...] + ext_ref[5:5 + tl, :] * cw_ref[0:1, :]
    for j in range(1, CONV_W):
        ua = ua + ext_ref[5 + j:5 + j + tl, :] * cw_ref[j:j + 1, :]
    ext_ref[5:8, :] = ext_ref[5 + tl:8 + tl, :]

    uab = ua.astype(BF16)
    ng = D_RNN // LRU_GROUP
    r_pre = jnp.concatenate(
        [_dot(uab[:, g * LRU_GROUP:(g + 1) * LRU_GROUP], wr_ref[g]) for g in range(ng)], axis=1)
    i_pre = jnp.concatenate(
        [_dot(uab[:, g * LRU_GROUP:(g + 1) * LRU_GROUP], wi_ref[g]) for g in range(ng)], axis=1)
    r = _sigmoid(r_pre + br_ref[...])
    ig = _sigmoid(i_pre + bi_ref[...])
    log_a = (-LRU_C) * r * _softplus(-lam_ref[...])
    a = jnp.exp(log_a)
    th = jnp.tanh(log_a)
    mult = jnp.sqrt(-2.0 * th / (1.0 - th))
    row = lax.broadcasted_iota(jnp.int32, (tl, D_RNN), 0)
    if first_frame:
        mult = jnp.where((row == 0) & (li == 0), 1.0, mult)
    bx = mult * (ig * ua)

    a = a.reshape(tl // 8, 8, D_RNN)
    bx = bx.reshape(tl // 8, 8, D_RNN)
    sub = lax.broadcasted_iota(jnp.int32, (tl // 8, 8, D_RNN), 1)
    for d in (1, 2, 4):
        a_sh = jnp.where(sub < d, 1.0, pltpu.roll(a, d, 1))
        b_sh = jnp.where(sub < d, 0.0, pltpu.roll(bx, d, 1))
        bx = a * b_sh + bx
        a = a * a_sh
    carry = hc_ref[...]
    groups = []
    for g in range(tl // 8):
        hg = bx[g] + a[g] * carry
        carry = hg[7:8, :]
        groups.append(hg)
    h = jnp.concatenate(groups, axis=0)
    hc_ref[...] = carry
    hl_ref[0] = carry

    gx = ag_ref[0].astype(F32)
    gelu = 0.5 * gx * (1.0 + jnp.tanh(0.7978845608028654 * (gx + 0.044715 * (gx * gx * gx))))
    ya_ref[0] = (h * gelu).astype(ya_ref.dtype)


def _lru(z3, cbuf8, h0, cw8, cb, wr, br, wi, bi, lam, tl, first_frame):
    B, S, _ = z3.shape
    full = lambda shape: pl.BlockSpec(shape, lambda b, l: (0,) * len(shape))
    return pl.pallas_call(
        functools.partial(_lru_kernel, first_frame=first_frame),
        grid=(B, S // tl),
        in_specs=[pl.BlockSpec((1, tl, D_RNN), lambda b, l: (b, l, OFF_AX // D_RNN)),
                  pl.BlockSpec((1, tl, D_RNN), lambda b, l: (b, l, OFF_AG // D_RNN)),
                  pl.BlockSpec((1, 8, D_RNN), lambda b, l: (b, 0, 0)),
                  pl.BlockSpec((1, 1, D_RNN), lambda b, l: (b, 0, 0)),
                  full((8, D_RNN)), full((1, D_RNN)),
                  full((D_RNN // LRU_GROUP, LRU_GROUP, LRU_GROUP)), full((1, D_RNN)),
                  full((D_RNN // LRU_GROUP, LRU_GROUP, LRU_GROUP)), full((1, D_RNN)),
                  full((1, D_RNN))],
        out_specs=[pl.BlockSpec((1, tl, D_RNN), lambda b, l: (b, l, 0)),
                   pl.BlockSpec((1, 1, D_RNN), lambda b, l: (b, 0, 0))],
        out_shape=[jax.ShapeDtypeStruct((B, S, D_RNN), BF16),
                   jax.ShapeDtypeStruct((B, 1, D_RNN), F32)],
        scratch_shapes=[pltpu.VMEM((tl + 8, D_RNN), F32), pltpu.VMEM((1, D_RNN), F32)],
        compiler_params=_cp(("parallel", "arbitrary")),
        name="conv_rglru",
    )(z3, z3, cbuf8, h0, cw8, cb, wr, br, wi, bi, lam)


def _sb_kernel(q_ref, k_ref, v_ref, kn_ref, vn_ref, o_ref, acc_ref, c_ref, *, tq, q_off, n_new):
    tk = KEY_BLOCK
    rs = min(tq, tk)
    ns = tq // rs
    qi = pl.program_id(2)
    q = (q_ref[0].astype(F32) * (SB_HEAD_DIM ** -0.5)).astype(BF16)
    qpos = q_off + qi * tq + lax.broadcasted_iota(jnp.int32, (tq, tk), 0)
    lane = lax.broadcasted_iota(jnp.int32, (rs, tk), 1)
    jj = lax.broadcasted_iota(jnp.int32, (tk, tk), 0)
    ll = lax.broadcasted_iota(jnp.int32, (tk, tk), 1)
    suffix = jnp.where(jj >= ll, 1.0, 0.0).astype(BF16)
    acc_ref[...] = jnp.zeros_like(acc_ref)
    c_ref[...] = jnp.zeros_like(c_ref)
    diag = [(q_off + qi * tq + (i + 1) * rs - 2) // tk for i in range(ns)]
    never = jnp.int32(2 ** 30)
    group = lax.broadcasted_iota(jnp.int32, (tq, 1), 0) // rs
    diag_row = jnp.zeros((tq, 1), jnp.int32)
    for i in range(ns):
        diag_row = jnp.where(group == i, diag[i], diag_row)

    def cond(carry):
        s, cmin = carry
        return (s <= diag[ns - 1]) & (cmin < SB_DEAD_MASS)

    head = pl.program_id(1)

    def head_rows(ref, start, n):
        return ref[0, pl.ds(SB_HEADS * start + head, n, stride=SB_HEADS), :]

    def step(s, ks, vs, kpos):
        zs = [_dot_nt(q[i * rs:(i + 1) * rs, :], ks[i]) for i in range(ns)]
        z = jnp.concatenate(zs, axis=0) if ns > 1 else zs[0]
        earlier = (jnp.concatenate(kpos, axis=0) if ns > 1 else kpos[0]) < qpos
        u = jnp.where(earlier, _softplus(z), 0.0)
        u_hi = u.astype(BF16)
        u_lo = (u - u_hi.astype(F32)).astype(BF16)
        suf = _dot(u_hi, suffix) + _dot(u_lo, suffix)
        c = c_ref[...]
        w = jnp.where(earlier, jnp.exp(z - suf - c), 0.0).astype(BF16)
        pv = [_dot(w[i * rs:(i + 1) * rs, :], vs[i]) for i in range(ns)]
        acc_ref[...] += jnp.concatenate(pv, axis=0) if ns > 1 else pv[0]
        c_new = c + suf[:, 0:1]
        c_ref[...] = c_new
        return s + 1, jnp.min(jnp.where(diag_row > s, c_new, jnp.inf))

    def body(carry):
        s, _ = carry
        ks, vs, kpos = [], [], []
        for i in range(ns):
            kb = diag[i] - s
            start = pl.multiple_of(jnp.maximum(kb, 0) * tk, tk)
            ks.append(head_rows(k_ref, start, tk).astype(BF16))
            vs.append(head_rows(v_ref, start, tk).astype(BF16))
            kpos.append(jnp.where(kb >= 0, start, never) + lane)
        return step(s, ks, vs, kpos)

    first = (jnp.int32(0), jnp.float32(0.0))
    if n_new:
        pad = jnp.zeros((tk - n_new, SB_HEAD_DIM), BF16)
        first = step(jnp.int32(0), [jnp.concatenate([head_rows(kn_ref, 0, n_new).astype(BF16), pad], axis=0)],
                     [jnp.concatenate([head_rows(vn_ref, 0, n_new).astype(BF16), pad], axis=0)],
                     [diag[0] * tk + lane])
    lax.while_loop(cond, body, first)
    o_ref[0] = acc_ref[...].astype(o_ref.dtype)


def _sb_attention(q_arr, k_arr, v_arr, kn_arr, vn_arr, q_cb, layer, tq, q_off, n_new):
    B, Tq, _ = q_arr.shape
    Tk = k_arr.shape[1] // SB_HEADS
    nrows = SB_HEADS * (n_new if n_new else 8)
    assert n_new == 0 or (Tq == tq == n_new <= KEY_BLOCK and q_off % KEY_BLOCK == 0 and q_off == Tk)
    return pl.pallas_call(
        functools.partial(_sb_kernel, tq=tq, q_off=q_off, n_new=n_new),
        grid=(B, SB_HEADS, Tq // tq),
        in_specs=[pl.BlockSpec((1, tq, SB_HEAD_DIM), lambda b, h, i: (b, i, q_cb + h)),
                  pl.BlockSpec((1, Tk * SB_HEADS, SB_HEAD_DIM), lambda b, h, i: (layer * B + b, 0, 0)),
                  pl.BlockSpec((1, Tk * SB_HEADS, SB_HEAD_DIM), lambda b, h, i: (layer * B + b, 0, 0)),
                  pl.BlockSpec((1, nrows, SB_HEAD_DIM), lambda b, h, i: (layer * B + b, 0, 0)),
                  pl.BlockSpec((1, nrows, SB_HEAD_DIM), lambda b, h, i: (layer * B + b, 0, 0))],
        out_specs=pl.BlockSpec((1, tq, SB_HEAD_DIM), lambda b, h, i: (b, i, h)),
        out_shape=jax.ShapeDtypeStruct((B, Tq, SB_WIDTH), BF16),
        scratch_shapes=[pltpu.VMEM((tq, SB_HEAD_DIM), F32), pltpu.VMEM((tq, 1), F32)],
        compiler_params=_cp(("parallel", "parallel", "arbitrary")),
        name="stick_breaking",
    )(q_arr, k_arr, v_arr, kn_arr, vn_arr)


def _mlstm_kernel(mq_ref, mk_ref, mv_ref, mo_ref, if_ref, c0_ref, n0_ref, m0_ref, g_ref,
                  yc_ref, c_out, n_out, m_out, c_s, n_s, m_s, *, ck):
    tl = mq_ref.shape[1]
    nh = ML_HEADS
    li = pl.program_id(1)

    @pl.when(li == 0)
    def _():
        c_s[...] = c0_ref[0]
        n_s[...] = n0_ref[0]
        m_s[...] = jnp.broadcast_to(m0_ref[0], (nh, 1, LANES))

    t_i = lax.broadcasted_iota(jnp.int32, (nh, ck, ck), 1)
    s_i = lax.broadcasted_iota(jnp.int32, (nh, ck, ck), 2)
    causal = s_i <= t_i
    diag = s_i == t_i
    gain = jnp.stack([g_ref[:, h * ML_V_DIM:(h + 1) * ML_V_DIM] for h in range(nh)])

    def heads(ref, r0, width):
        return jnp.stack([ref[0, pl.ds(r0, ck), h * width:(h + 1) * width] for h in range(nh)]).astype(F32)

    def chunk(ci, carry):
        r0 = pl.multiple_of(ci * ck, ck)
        ifb = if_ref[0, pl.ds(r0, ck), :]
        ig_col = jnp.stack([ifb[:, h:h + 1] for h in range(nh)])
        lf_col = -_softplus(-jnp.stack([ifb[:, nh + h:nh + h + 1] for h in range(nh)]))
        q4 = heads(mq_ref, r0, ML_QK_DIM)
        k4 = heads(mk_ref, r0, ML_QK_DIM) * (ML_QK_DIM ** -0.5)
        v4 = heads(mv_ref, r0, ML_V_DIM)
        o4 = heads(mo_ref, r0, ML_V_DIM)
        ig_row = jnp.sum(jnp.where(diag, ig_col, 0.0), axis=1, keepdims=True)
        lf_row = jnp.sum(jnp.where(diag, lf_col, 0.0), axis=1, keepdims=True)
        b_col = jnp.sum(jnp.where(causal, lf_row, 0.0), axis=2, keepdims=True)
        b_row = jnp.sum(jnp.where(t_i <= s_i, lf_col, 0.0), axis=1, keepdims=True)
        m_prev = m_s[:, :, 0:1]
        dmat = jnp.where(causal, b_col - b_row + ig_row, -jnp.inf)
        inter = b_col + m_prev
        m_t = jnp.maximum(inter, jnp.max(dmat, axis=2, keepdims=True))
        s_inter = jnp.exp(inter - m_t)
        qb = q4.astype(BF16)
        kb = k4.astype(BF16)
        vb = v4.astype(BF16)
        wqk = jnp.exp(dmat - m_t) * jnp.stack([_dot_nt(qb[h], kb[h]) for h in range(nh)])
        c_prev = c_s[...]
        n_prev = n_s[...]
        cb = c_prev.astype(BF16)
        wb = wqk.astype(BF16)
        num = (s_inter * jnp.stack([_dot(qb[h], cb[h]) for h in range(nh)])
               + jnp.stack([_dot(wb[h], vb[h]) for h in range(nh)]))
        den = (s_inter * jnp.sum(q4 * n_prev, axis=2, keepdims=True)
               + jnp.sum(wqk, axis=2, keepdims=True))
        hh = num / jnp.maximum(jnp.abs(den), jnp.exp(-m_t))
        b_end = b_col[:, ck - 1:ck, :]
        g_col = b_end - b_col + ig_col
        m_new = jnp.maximum(b_end + m_prev, jnp.max(g_col, axis=1, keepdims=True))
        s_old = jnp.exp(b_end + m_prev - m_new)
        kw = k4 * jnp.exp(g_col - m_new)
        kwb = kw.astype(BF16)
        c_s[...] = s_old * c_prev + jnp.stack([_dot_tn(kwb[h], vb[h]) for h in range(nh)])
        n_s[...] = s_old * n_prev + jnp.sum(kw, axis=1, keepdims=True)
        m_s[...] = jnp.broadcast_to(m_new, (nh, 1, LANES))
        mu = jnp.mean(hh, axis=2, keepdims=True)
        dlt = hh - mu
        var = jnp.mean(dlt * dlt, axis=2, keepdims=True)
        out = (dlt * lax.rsqrt(var + LN_EPS) * gain * _sigmoid(o4)).astype(yc_ref.dtype)
        for h in range(nh):
            yc_ref[0, pl.ds(r0, ck), h * ML_V_DIM:(h + 1) * ML_V_DIM] = out[h]
        return carry

    lax.fori_loop(0, tl // ck, chunk, 0)
    c_out[0] = c_s[...]
    n_out[0] = n_s[...]
    m_out[0] = m_s[:, :, 0:1]


def _mlstm(z3, if3, c0, n0, m0p, g, tl, ck):
    B, S, _ = z3.shape
    return pl.pallas_call(
        functools.partial(_mlstm_kernel, ck=ck),
        grid=(B, S // tl),
        in_specs=[pl.BlockSpec((1, tl, ML_QK_WIDTH), lambda b, l: (b, l, OFF_MQ // ML_QK_WIDTH)),
                  pl.BlockSpec((1, tl, ML_QK_WIDTH), lambda b, l: (b, l, OFF_MK // ML_QK_WIDTH)),
                  pl.BlockSpec((1, tl, ML_V_WIDTH), lambda b, l: (b, l, OFF_MV // ML_V_WIDTH)),
                  pl.BlockSpec((1, tl, ML_V_WIDTH), lambda b, l: (b, l, OFF_MO // ML_V_WIDTH)),
                  pl.BlockSpec((1, tl, LANES), lambda b, l: (b, l, 0)),
                  pl.BlockSpec((1, ML_HEADS, ML_QK_DIM, ML_V_DIM), lambda b, l: (b, 0, 0, 0)),
                  pl.BlockSpec((1, ML_HEADS, 1, ML_QK_DIM), lambda b, l: (b, 0, 0, 0)),
                  pl.BlockSpec((1, ML_HEADS, 1, 1), lambda b, l: (b, 0, 0, 0)),
                  pl.BlockSpec((1, ML_V_WIDTH), lambda b, l: (0, 0))],
        out_specs=[pl.BlockSpec((1, tl, ML_V_WIDTH), lambda b, l: (b, l, 0)),
                   pl.BlockSpec((1, ML_HEADS, ML_QK_DIM, ML_V_DIM), lambda b, l: (b, 0, 0, 0)),
                   pl.BlockSpec((1, ML_HEADS, 1, ML_QK_DIM), lambda b, l: (b, 0, 0, 0)),
                   pl.BlockSpec((1, ML_HEADS, 1, 1), lambda b, l: (b, 0, 0, 0))],
        out_shape=[jax.ShapeDtypeStruct((B, S, ML_V_WIDTH), BF16),
                   jax.ShapeDtypeStruct((B, ML_HEADS, ML_QK_DIM, ML_V_DIM), F32),
                   jax.ShapeDtypeStruct((B, ML_HEADS, 1, ML_QK_DIM), F32),
                   jax.ShapeDtypeStruct((B, ML_HEADS, 1, 1), F32)],
        scratch_shapes=[pltpu.VMEM((ML_HEADS, ML_QK_DIM, ML_V_DIM), F32),
                        pltpu.VMEM((ML_HEADS, 1, ML_QK_DIM), F32),
                        pltpu.VMEM((ML_HEADS, 1, LANES), F32)],
        compiler_params=_cp(("parallel", "arbitrary")),
        name="mlstm",
    )(z3, z3, z3, z3, if3, c0, n0, m0p, g)


def _merge_kernel(ya_ref, yb_ref, yc_ref, g0_ref, g1_ref, g2_ref, x_ref, wpa_ref, wpb_ref, wpc_ref,
                  wo_ref, lg_ref, lb_ref, o_ref, *, alpha):
    mixed = _sigmoid(g0_ref[...].astype(F32)) * _dot(ya_ref[...], wpa_ref[...])
    mixed = mixed + _sigmoid(g1_ref[...].astype(F32)) * _dot(yb_ref[...], wpb_ref[...])
    mixed = mixed + _sigmoid(g2_ref[...].astype(F32)) * _dot(yc_ref[...], wpc_ref[...])
    r = alpha * x_ref[...] + _dot(mixed.astype(BF16), wo_ref[...])
    o_ref[...] = _layer_norm(r, lg_ref[...], lb_ref[...])


def _merge(ya, yb, yc, z, x, wpa, wpb, wpc, wo, lg, lb, tm, alpha):
    T = x.shape[0]
    full = lambda shape: pl.BlockSpec(shape, lambda i: (0,) * len(shape))
    mgb = OFF_MG // D_MODEL
    return pl.pallas_call(
        functools.partial(_merge_kernel, alpha=alpha),
        grid=(T // tm,),
        in_specs=[pl.BlockSpec((tm, D_RNN), lambda i: (i, 0)),
                  pl.BlockSpec((tm, SB_WIDTH), lambda i: (i, 0)),
                  pl.BlockSpec((tm, ML_V_WIDTH), lambda i: (i, 0)),
                  pl.BlockSpec((tm, D_MODEL), lambda i: (i, mgb)),
                  pl.BlockSpec((tm, D_MODEL), lambda i: (i, mgb + 1)),
                  pl.BlockSpec((tm, D_MODEL), lambda i: (i, mgb + 2)),
                  pl.BlockSpec((tm, D_MODEL), lambda i: (i, 0)),
                  full((D_RNN, D_MODEL)), full((SB_WIDTH, D_MODEL)), full((ML_V_WIDTH, D_MODEL)),
                  full((D_MODEL, D_MODEL)), full((1, D_MODEL)), full((1, D_MODEL))],
        out_specs=pl.BlockSpec((tm, D_MODEL), lambda i: (i, 0)),
        out_shape=jax.ShapeDtypeStruct((T, D_MODEL), F32),
        compiler_params=_cp(("parallel",)),
        name="merge_ln1",
    )(ya, yb, yc, z, z, z, x, wpa, wpb, wpc, wo, lg, lb)


def _router_kernel(x_ref, rw_ref, rb_ref, tri_ref, topi_ref, gate_ref, rank_ref, cnt_ref, run_ref):
    @pl.when(pl.program_id(0) == 0)
    def _():
        run_ref[...] = jnp.zeros_like(run_ref)

    tm = x_ref.shape[0]
    l = _dot(x_ref[...].astype(BF16), rw_ref[...]) + rb_ref[...]
    lane = lax.broadcasted_iota(jnp.int32, (tm, LANES), 1)
    lane_f = lane.astype(F32)
    vals, idxs, hots = [], [], []
    for _ in range(TOP_K):
        m = jnp.max(l, axis=1, keepdims=True)
        idx = jnp.min(jnp.where(l == m, lane_f, float(LANES)), axis=1, keepdims=True)
        hot = lane_f == idx
        vals.append(m)
        idxs.append(idx)
        hots.append(hot)
        l = jnp.where(hot, -jnp.inf, l)
    ex = [jnp.exp(v - vals[0]) for v in vals]
    den = ex[0] + ex[1] + ex[2] + ex[3]
    onehot = jnp.zeros((tm, LANES), F32)
    for hot in hots:
        onehot = onehot + jnp.where(hot, 1.0, 0.0)
    before = _dot(tri_ref[...], onehot.astype(BF16)) + run_ref[...]
    topi = jnp.zeros((tm, LANES), F32)
    gate = jnp.zeros((tm, LANES), F32)
    rank = jnp.zeros((tm, LANES), F32)
    for k in range(TOP_K):
        rk = jnp.sum(jnp.where(hots[k], before, 0.0), axis=1, keepdims=True)
        topi = jnp.where(lane == k, idxs[k], topi)
        gate = jnp.where(lane == k, ex[k] / den, gate)
        rank = jnp.where(lane == k, rk, rank)
    run_new = run_ref[...] + jnp.sum(onehot, axis=0, keepdims=True)
    run_ref[...] = run_new
    topi_ref[...] = topi.astype(jnp.int32)
    gate_ref[...] = gate
    rank_ref[...] = rank.astype(jnp.int32)
    cnt_ref[...] = run_new.astype(jnp.int32)


def _router(x, rw, rb, tri, tm):
    T = x.shape[0]
    full = lambda shape: pl.BlockSpec(shape, lambda i: (0,) * len(shape))
    return pl.pallas_call(
        _router_kernel,
        grid=(T // tm,),
        in_specs=[pl.BlockSpec((tm, D_MODEL), lambda i: (i, 0)),
                  full((D_MODEL, LANES)), full((1, LANES)), full((tm, tm))],
        out_specs=[pl.BlockSpec((tm, LANES), lambda i: (i, 0)),
                   pl.BlockSpec((tm, LANES), lambda i: (i, 0)),
                   pl.BlockSpec((tm, LANES), lambda i: (i, 0)),
                   full((1, LANES))],
        out_shape=[jax.ShapeDtypeStruct((T, LANES), jnp.int32),
                   jax.ShapeDtypeStruct((T, LANES), F32),
                   jax.ShapeDtypeStruct((T, LANES), jnp.int32),
                   jax.ShapeDtypeStruct((1, LANES), jnp.int32)],
        scratch_shapes=[pltpu.VMEM((1, LANES), F32)],
        compiler_params=_cp(("arbitrary",)),
        name="router",
    )(x, rw, rb, tri)


ROW_CHUNKS = D_MODEL // 2 // LANES
U32 = jnp.uint32


def _tile_copy(src_ref, src_row, dst_ref, dst_row, sem):
    src = src_ref.at[pl.ds(pl.multiple_of(src_row * ROW_CHUNKS, ROW_CHUNKS), ROW_CHUNKS)]
    dst = dst_ref.at[pl.ds(pl.multiple_of(dst_row * ROW_CHUNKS, ROW_CHUNKS), ROW_CHUNKS)]
    return pltpu.make_async_copy(src, dst, sem)


def _interleave_store(ref, row0, value):
    n = value.shape[0]
    bits = pltpu.bitcast(value.astype(BF16).astype(F32), U32)
    half = D_MODEL // 2
    packed = lax.shift_right_logical(bits[:, :half], U32(16)) | bits[:, half:]
    for c in range(ROW_CHUNKS):
        ref[pl.ds(row0 * ROW_CHUNKS + c, n, stride=ROW_CHUNKS), :] = packed[:, c * LANES:(c + 1) * LANES]


def _interleave_load(ref, row0, n):
    packed = jnp.concatenate(
        [ref[pl.ds(row0 * ROW_CHUNKS + c, n, stride=ROW_CHUNKS), :] for c in range(ROW_CHUNKS)], axis=1)
    lo = pltpu.bitcast(lax.shift_left(packed, U32(16)), F32)
    hi = pltpu.bitcast(packed & U32(0xFFFF0000), F32)
    return jnp.concatenate([lo, hi], axis=1)


def _dispatch_kernel(padlo_ref, padn_ref, dest_ref, x_ref, xs_hbm, xi, ztile, sem, *, ts):
    @pl.when(pl.program_id(0) == 0)
    def _():
        ztile[...] = jnp.zeros_like(ztile)

        def fill(e, c):
            lax.fori_loop(0, padn_ref[e],
                          lambda r, c2: (_tile_copy(ztile, 0, xs_hbm, padlo_ref[e] + r, sem).start(), c2)[1], 0)
            return c

        def fill_done(e, c):
            lax.fori_loop(0, padn_ref[e],
                          lambda r, c2: (_tile_copy(ztile, 0, xs_hbm, 0, sem).wait(), c2)[1], 0)
            return c

        lax.fori_loop(0, N_EXPERTS, fill, 0)
        lax.fori_loop(0, N_EXPERTS, fill_done, 0)

    _interleave_store(xi, 0, x_ref[...])

    def issue(t, c):
        for k in range(TOP_K):
            _tile_copy(xi, t, xs_hbm, dest_ref[TOP_K * t + k], sem).start(priority=k % 2)
        return c

    lax.fori_loop(0, ts, issue, 0)

    def drain(t, c):
        for k in range(TOP_K):
            _tile_copy(xi, 0, xs_hbm, 0, sem).wait()
        return c

    lax.fori_loop(0, ts, drain, 0)


def _dispatch(pad_lo, pad_n, dest_flat, x, rows, ts):
    T = x.shape[0]
    grid_spec = pltpu.PrefetchScalarGridSpec(
        num_scalar_prefetch=2,
        grid=(T // ts,),
        in_specs=[pl.BlockSpec((TOP_K * ts,), lambda i, lo, n: (i,), memory_space=pltpu.SMEM),
                  pl.BlockSpec((ts, D_MODEL), lambda i, lo, n: (i, 0))],
        out_specs=pl.BlockSpec(memory_space=pl.ANY),
        scratch_shapes=[pltpu.VMEM((ts * ROW_CHUNKS, LANES), U32), pltpu.VMEM((8, LANES), U32),
                        pltpu.SemaphoreType.DMA(())],
    )
    return pl.pallas_call(
        functools.partial(_dispatch_kernel, ts=ts),
        grid_spec=grid_spec,
        out_shape=jax.ShapeDtypeStruct((rows * ROW_CHUNKS, LANES), U32),
        compiler_params=_cp(("arbitrary",)),
        name="moe_dispatch",
    )(pad_lo, pad_n, dest_flat, x)


def _experts_kernel(be_ref, nu_ref, xs_ref, w1_ref, b1_ref, w2_ref, b2_ref, y_ref, w1b_ref, w2b_ref, *, m):
    j = pl.program_id(0)

    @pl.when((j == 0) | (be_ref[j] != be_ref[jnp.maximum(j - 1, 0)]))
    def _():
        w1b_ref[...] = w1_ref[0, 0].astype(BF16)
        w2b_ref[...] = w2_ref[0, 0].astype(BF16)

    @pl.when(j < nu_ref[0])
    def _():
        gu = _dot(_interleave_load(xs_ref, 0, m).astype(BF16), w1b_ref[...]) + b1_ref[0]
        g_ = jnp.minimum(gu[:, :D_FF], SWIGLU_LIMIT)
        up = jnp.clip(gu[:, D_FF:], -SWIGLU_LIMIT, SWIGLU_LIMIT)
        act = (up + 1.0) * g_ * _sigmoid(SWIGLU_ALPHA * g_)
        _interleave_store(y_ref, 0, _dot(act.astype(BF16), w2b_ref[...]) + b2_ref[0])

    @pl.when(pl.program_id(0) >= nu_ref[0])
    def _():
        y_ref[...] = jnp.zeros_like(y_ref)


def _experts(block_e, n_used, xs, w1, b1, w2, b2, m, layer):
    nblk = xs.shape[0] // (m * ROW_CHUNKS)
    grid_spec = pltpu.PrefetchScalarGridSpec(
        num_scalar_prefetch=2,
        grid=(nblk,),
        in_specs=[pl.BlockSpec((m * ROW_CHUNKS, LANES), lambda j, be, nu: (jnp.minimum(j, nu[0] - 1), 0)),
                  pl.BlockSpec((1, 1, D_MODEL, 2 * D_FF), lambda j, be, nu: (layer, be[j], 0, 0)),
                  pl.BlockSpec((1, 1, 2 * D_FF), lambda j, be, nu: (be[j], 0, 0)),
                  pl.BlockSpec((1, 1, D_FF, D_MODEL), lambda j, be, nu: (layer, be[j], 0, 0)),
                  pl.BlockSpec((1, 1, D_MODEL), lambda j, be, nu: (be[j], 0, 0))],
        out_specs=pl.BlockSpec((m * ROW_CHUNKS, LANES), lambda j, be, nu: (j, 0)),
        scratch_shapes=[pltpu.VMEM((D_MODEL, 2 * D_FF), BF16), pltpu.VMEM((D_FF, D_MODEL), BF16)],
    )
    return pl.pallas_call(
        functools.partial(_experts_kernel, m=m),
        grid_spec=grid_spec,
        out_shape=jax.ShapeDtypeStruct(xs.shape, U32),
        compiler_params=_cp(("arbitrary",)),
        name="moe_experts",
    )(block_e, n_used, xs, w1, b1, w2, b2)


def _combine_kernel(dest_a, dest_b, dest_next, x_ref, p_ref, gate_ref, y_hbm, wp_ref, wg_ref, bg_ref,
                    lg_ref, lb_ref, o_ref, ybuf_a, ybuf_b, sem_a, sem_b, *, alpha, tm, halves):
    i = pl.program_id(0)
    bufs = [(ybuf_a, sem_a), (ybuf_b, sem_b)][:halves]
    dests = [dest_a, dest_b][:halves]

    def issue(dest_ref, buf, sem):
        def one(t, c):
            for k in range(TOP_K):
                _tile_copy(y_hbm, dest_ref[TOP_K * t + k], buf, k * tm + t, sem).start(priority=k % 2)
            return c

        lax.fori_loop(0, tm, one, 0)

    def drain(buf, sem):
        def one(t, c):
            for k in range(TOP_K):
                _tile_copy(y_hbm, 0, buf, 0, sem).wait()
            return c

        lax.fori_loop(0, tm, one, 0)

    @pl.when(i == 0)
    def _():
        issue(dests[0], *bufs[0])

    for h in range(halves):
        if h + 1 < halves:
            issue(dests[h + 1], *bufs[h + 1])
        rows = pl.ds(h * tm, tm)
        x = x_ref[rows, :]
        ple = (_sigmoid(_dot(x.astype(BF16), wg_ref[...]) + bg_ref[...])
               * _dot(p_ref[rows, :].astype(BF16), wp_ref[...]))
        r = alpha * x + ple
        drain(*bufs[h])
        for k in range(TOP_K):
            r = r + gate_ref[rows, k:k + 1] * _interleave_load(bufs[h][0], k * tm, tm)
        o_ref[rows, :] = _layer_norm(r, lg_ref[...], lb_ref[...])
        if h == 0:
            @pl.when(i + 1 < pl.num_programs(0))
            def _():
                issue(dest_next, *bufs[0])


def _combine(dest_flat, x, p, gate, y, wp, wg, bg, lg, lb, tm, alpha):
    T = x.shape[0]
    ntiles = T // tm
    halves = 2 if ntiles % 2 == 0 else 1
    tb = halves * tm
    full = lambda shape: pl.BlockSpec(shape, lambda i: (0,) * len(shape))
    dest_spec = lambda fn: pl.BlockSpec((TOP_K * tm,), fn, memory_space=pltpu.SMEM)
    return pl.pallas_call(
        functools.partial(_combine_kernel, alpha=alpha, tm=tm, halves=halves),
        grid=(ntiles // halves,),
        in_specs=[dest_spec(lambda i: (halves * i,)),
                  dest_spec(lambda i: (halves * i + halves - 1,)),
                  dest_spec(lambda i: (jnp.minimum(halves * (i + 1), ntiles - 1),)),
                  pl.BlockSpec((tb, D_MODEL), lambda i: (i, 0)),
                  pl.BlockSpec((tb, D_PLE), lambda i: (i, 0)),
                  pl.BlockSpec((tb, LANES), lambda i: (i, 0)),
                  pl.BlockSpec(memory_space=pl.ANY),
                  full((D_PLE, D_MODEL)), full((D_MODEL, D_MODEL)), full((1, D_MODEL)),
                  full((1, D_MODEL)), full((1, D_MODEL))],
        out_specs=pl.BlockSpec((tb, D_MODEL), lambda i: (i, 0)),
        out_shape=jax.ShapeDtypeStruct((T, D_MODEL), F32),
        scratch_shapes=[pltpu.VMEM((TOP_K * tm * ROW_CHUNKS, LANES), U32),
                        pltpu.VMEM((TOP_K * tm * ROW_CHUNKS, LANES), U32),
                        pltpu.SemaphoreType.DMA(()), pltpu.SemaphoreType.DMA(())],
        compiler_params=_cp(("arbitrary",)),
        name="moe_combine_ln2",
    )(dest_flat, dest_flat, dest_flat, x, p, gate, y, wp, wg, bg, lg, lb)


def _pick(n, pref):
    t = min(n, pref)
    while n % t:
        t //= 2
    return t


def _layer(x, p, lw, conv_buf, lru_h, ml_c, ml_n, ml_m, k_past, v_past, k_all, v_all, prompt, alpha, layer):
    B, L, _ = x.shape
    T = B * L
    xf = x.reshape(T, D_MODEL)

    z = _inproj(xf, lw["w_in"], lw["b_in"], _pick(T, 1024), 1024)
    k_all, v_all, ifg = _kvif(xf, lw["w_kvif"], lw["b_kvif"], k_all, v_all, layer, _pick(T, 1024))
    z3 = z.reshape(B, L, N_MAIN)
    k_rows = k_all.reshape(-1, L * SB_HEADS, SB_HEAD_DIM)
    v_rows = v_all.reshape(-1, L * SB_HEADS, SB_HEAD_DIM)
    conv_new = z3[:, L - (CONV_W - 1):, OFF_AX:OFF_AX + D_RNN].astype(F32)

    cbuf8 = jnp.concatenate([jnp.zeros((B, 8 - (CONV_W - 1), D_RNN), F32), conv_buf.astype(F32)], axis=1)
    ya, lru_new = _lru(z3, cbuf8, lru_h.reshape(B, 1, D_RNN).astype(F32), lw["conv_w8"], lw["conv_b"],
                       lw["wr"], lw["br"], lw["wi"], lw["bi"], lw["lam"], _pick(L, 256), prompt)

    if prompt:
        yb = _sb_attention(z3, k_rows, v_rows, k_rows, v_rows, OFF_Q // SB_HEAD_DIM, layer,
                           _pick(L, 1024), 0, 0)
    else:
        yb = _sb_attention(z3, k_past, v_past, k_rows, v_rows, OFF_Q // SB_HEAD_DIM, layer,
                           L, k_past.shape[1] // SB_HEADS, L)

    ck = _pick(L, 128)
    yc, c_new, n_new, m_new = _mlstm(z3, ifg.reshape(B, L, LANES), ml_c.astype(F32),
                                     ml_n.astype(F32).reshape(B, ML_HEADS, 1, ML_QK_DIM),
                                     ml_m.astype(F32).reshape(B, ML_HEADS, 1, 1),
                                     lw["ml_g"], _pick(L, 512), ck)
    n_new = n_new.reshape(B, ML_HEADS, ML_QK_DIM)
    m_new = m_new.reshape(B, ML_HEADS)

    x1 = _merge(ya.reshape(T, D_RNN), yb.reshape(T, SB_WIDTH), yc.reshape(T, ML_V_WIDTH), z, xf,
                lw["w_pa"], lw["w_pb"], lw["w_pc"], lw["w_out"], lw["ln1_g"], lw["ln1_b"],
                _pick(T, 512), alpha)

    tr = _pick(T, 512)
    tri = jnp.tril(jnp.ones((tr, tr), BF16), -1)
    topi, gate, rank, cnt = _router(x1, lw["router_w"], lw["router_b"], tri, tr)
    TK = T * TOP_K
    m_rows = max(16, min(512, TK // N_EXPERTS))
    nb = TK // m_rows + N_EXPERTS
    counts = cnt[0, :N_EXPERTS]
    padded = (counts + (m_rows - 1)) // m_rows * m_rows
    pend = jnp.cumsum(padded)
    off = (pend - padded).astype(jnp.int32)
    blk_start = jnp.arange(nb, dtype=jnp.int32) * m_rows
    block_e = jnp.minimum(jnp.sum((pend[None, :] <= blk_start[:, None]).astype(jnp.int32), axis=1),
                          N_EXPERTS - 1).astype(jnp.int32)
    n_used = (pend[-1:] // m_rows).astype(jnp.int32)
    experts = jnp.arange(N_EXPERTS, dtype=jnp.int32)
    top4 = topi[:, :TOP_K]
    dest = rank[:, :TOP_K] + jnp.sum(jnp.where(top4[:, :, None] == experts, off, 0), axis=-1)
    dest_flat = dest.reshape(TK).astype(jnp.int32)
    xs = _dispatch(off + counts, (padded - counts).astype(jnp.int32), dest_flat, x1, nb * m_rows,
                   _pick(T, 256))
    y = _experts(block_e, n_used, xs, lw["exp_w1"], lw["exp_b1"], lw["exp_w2"], lw["exp_b2"], m_rows, layer)
    x2 = _combine(dest_flat, x1, p.reshape(T, D_PLE), gate, y, lw["ple_w"], lw["ple_gate_w"],
                  lw["ple_gate_b"], lw["ln2_g"], lw["ln2_b"], _pick(T, 256), alpha)

    return (x2.reshape(B, L, D_MODEL), k_all, v_all,
            (conv_new, lru_new.reshape(B, D_RNN), c_new, n_new, m_new))


def _block_diag_groups(w):
    per = LRU_GROUP // LRU_BLOCK
    w4 = w.reshape(D_RNN // LRU_GROUP, per, LRU_BLOCK, LRU_BLOCK)
    eye = jnp.eye(per, dtype=w.dtype)
    return jnp.einsum("gacd,ab->gacbd", w4, eye).reshape(D_RNN // LRU_GROUP, LRU_GROUP, LRU_GROUP)


def _split_in(w):
    main = jnp.concatenate([w[..., :ORIG_K], w[..., ORIG_MQ:ORIG_IF], w[..., ORIG_MG:]], axis=-1)
    pad = jnp.zeros(w.shape[:-1] + (LANES - 2 * ML_HEADS,), w.dtype)
    kvif = jnp.concatenate([w[..., ORIG_K:ORIG_MQ], w[..., ORIG_IF:ORIG_MG], pad], axis=-1)
    return main, kvif


def kernel(x_prompt, x_sample, cache_sb_k, cache_sb_v, state_conv, state_lru, state_mlstm_c, state_mlstm_n, state_mlstm_m, p_prompt, p_sample, w_in, b_in, conv_w, conv_b, lru_wr, lru_br, lru_wi, lru_bi, lru_lambda, ml_norm_g, w_pa, w_pb, w_pc, w_out, ln1_g, ln1_b, router_w, router_b, exp_w1, exp_b1, exp_w2, exp_b2, ple_w, ple_gate_w, ple_gate_b, ln2_g, ln2_b):
    depth = w_in.shape[0]
    alpha = (2 * depth) ** 0.25
    B = x_prompt.shape[0]
    y_prompt, y_sample = x_prompt, x_sample
    S = x_prompt.shape[1]
    Bs, Ls = x_sample.shape[:2]
    P = cache_sb_k.shape[2]
    k_cache = cache_sb_k.astype(F32).reshape(depth * Bs, P * SB_HEADS, SB_HEAD_DIM)
    v_cache = cache_sb_v.astype(F32).reshape(depth * Bs, P * SB_HEADS, SB_HEAD_DIM)
    kp, vp = (_unwritten((depth * B * S * SB_HEADS, SB_HEAD_DIM)) for _ in range(2))
    ks, vs = (_unwritten((depth * Bs * Ls * SB_HEADS, SB_HEAD_DIM)) for _ in range(2))
    st_p, st_s = [], []
    for i in range(depth):
        row = lambda a: a[i].reshape(1, -1).astype(F32)
        w_main, w_kvif = _split_in(w_in[i])
        b_main, b_kvif = _split_in(b_in[i])
        lw = dict(
            w_in=w_main.astype(BF16), b_in=b_main.reshape(1, -1).astype(F32),
            w_kvif=w_kvif.astype(BF16), b_kvif=b_kvif.reshape(1, -1).astype(F32),
            conv_w8=jnp.pad(conv_w[i].astype(F32), ((0, 8 - CONV_W), (0, 0))), conv_b=row(conv_b),
            wr=_block_diag_groups(lru_wr[i]).astype(BF16), br=row(lru_br),
            wi=_block_diag_groups(lru_wi[i]).astype(BF16), bi=row(lru_bi), lam=row(lru_lambda),
            ml_g=row(ml_norm_g),
            w_pa=w_pa[i].astype(BF16), w_pb=w_pb[i].astype(BF16), w_pc=w_pc[i].astype(BF16),
            w_out=w_out[i].astype(BF16), ln1_g=row(ln1_g), ln1_b=row(ln1_b),
            router_w=jnp.pad(router_w[i], ((0, 0), (0, LANES - N_EXPERTS))).astype(BF16),
            router_b=jnp.pad(router_b[i].astype(F32), (0, LANES - N_EXPERTS),
                             constant_values=-1e30).reshape(1, LANES),
            exp_w1=exp_w1.astype(F32), exp_b1=exp_b1[i].reshape(N_EXPERTS, 1, 2 * D_FF).astype(F32),
            exp_w2=exp_w2.astype(F32), exp_b2=exp_b2[i].reshape(N_EXPERTS, 1, D_MODEL).astype(F32),
            ple_w=ple_w[i].astype(BF16), ple_gate_w=ple_gate_w[i].astype(BF16), ple_gate_b=row(ple_gate_b),
            ln2_g=row(ln2_g), ln2_b=row(ln2_b),
        )
        y_prompt, kp, vp, sp = _layer(y_prompt, p_prompt[i], lw,
                                      jnp.zeros((B, CONV_W - 1, D_RNN), F32), jnp.zeros((B, D_RNN), F32),
                                      jnp.zeros((B, ML_HEADS, ML_QK_DIM, ML_V_DIM), F32),
                                      jnp.zeros((B, ML_HEADS, ML_QK_DIM), F32), jnp.zeros((B, ML_HEADS), F32),
                                      None, None, kp, vp, True, alpha, i)
        y_sample, ks, vs, ss = _layer(y_sample, p_sample[i], lw, state_conv[i], state_lru[i], state_mlstm_c[i],
                                      state_mlstm_n[i], state_mlstm_m[i], k_cache, v_cache, ks, vs,
                                      False, alpha, i)
        st_p.append(sp)
        st_s.append(ss)
    conv_p, lru_p, c_p, n_p, m_p = [jnp.stack(s) for s in zip(*st_p)]
    conv_s, lru_s, c_s, n_s, m_s = [jnp.stack(s) for s in zip(*st_s)]
    kv_p = (depth, B, S, SB_HEADS, SB_HEAD_DIM)
    kv_s = (depth, Bs, Ls, SB_HEADS, SB_HEAD_DIM)
    return (y_prompt, y_sample, kp.reshape(kv_p), vp.reshape(kv_p), conv_p, lru_p, c_p, n_p, m_p,
            ks.reshape(kv_s), vs.reshape(kv_s), conv_s, lru_s, c_s, n_s, m_s)
```

```python
import functools

import jax
import jax.numpy as jnp
from jax import lax
from jax.experimental import pallas as pl
from jax.experimental.pallas import tpu as pltpu

F32 = jnp.float32
BF16 = jnp.bfloat16

D_MODEL = 1024
D_RNN = 1024
N_LRU_BLOCKS = 16
LRU_BLOCK = D_RNN // N_LRU_BLOCKS
LRU_GROUP = 256
CONV_W = 4
LRU_C = 8.0
SB_HEADS = 4
SB_HEAD_DIM = 128
SB_WIDTH = SB_HEADS * SB_HEAD_DIM
ML_HEADS = 4
ML_QK_DIM = 64
ML_V_DIM = 128
ML_QK_WIDTH = ML_HEADS * ML_QK_DIM
ML_V_WIDTH = ML_HEADS * ML_V_DIM
N_BRANCH = 3
N_EXPERTS = 32
TOP_K = 4
D_FF = 512
SWIGLU_LIMIT = 7.0
SWIGLU_ALPHA = 1.702
D_PLE = 256
LN_EPS = 1e-5
LANES = 128
KEY_BLOCK = 128

OFF_AX, OFF_AG = 0, 1024
OFF_Q = 2048
OFF_MQ, OFF_MK, OFF_MV, OFF_MO = 2560, 2816, 3072, 3584
OFF_MG = 4096
N_MAIN = 7168
N_KVIF = 2 * SB_WIDTH + LANES
ORIG_K, ORIG_MQ, ORIG_IF, ORIG_MG = 2560, 3584, 5120, 5128
SB_DEAD_MASS = 88.0
VMEM_LIMIT = 56 * 1024 * 1024


def _cp(sem, vmem=VMEM_LIMIT):
    return pltpu.CompilerParams(dimension_semantics=sem, vmem_limit_bytes=vmem)


def _dot(a, b):
    return jnp.dot(a, b, preferred_element_type=F32)


def _dot_nt(a, b):
    return lax.dot_general(a, b, (((1,), (1,)), ((), ())), preferred_element_type=F32)


def _dot_tn(a, b):
    return lax.dot_general(a, b, (((0,), (0,)), ((), ())), preferred_element_type=F32)


def _sigmoid(x):
    return 0.5 * jnp.tanh(0.5 * x) + 0.5


def _softplus(x):
    return jnp.maximum(x, 0.0) + jnp.log1p(jnp.exp(-jnp.abs(x)))


def _layer_norm(r, g, b):
    mu = jnp.mean(r, axis=-1, keepdims=True)
    d = r - mu
    var = jnp.mean(d * d, axis=-1, keepdims=True)
    return d * lax.rsqrt(var + LN_EPS) * g + b


def _inproj_kernel(x_ref, w_ref, b_ref, o_ref, xb_ref):
    @pl.when(pl.program_id(1) == 0)
    def _():
        xb_ref[...] = x_ref[...].astype(BF16)

    o_ref[...] = (_dot(xb_ref[...], w_ref[...]) + b_ref[...]).astype(o_ref.dtype)


def _inproj(x, w, b, tm, tn):
    T, D = x.shape
    N = w.shape[1]
    return pl.pallas_call(
        _inproj_kernel,
        grid=(T // tm, N // tn),
        in_specs=[pl.BlockSpec((tm, D), lambda i, j: (i, 0)),
                  pl.BlockSpec((D, tn), lambda i, j: (0, j)),
                  pl.BlockSpec((1, tn), lambda i, j: (0, j))],
        out_specs=pl.BlockSpec((tm, tn), lambda i, j: (i, j)),
        out_shape=jax.ShapeDtypeStruct((T, N), BF16),
        scratch_shapes=[pltpu.VMEM((tm, D), BF16)],
        compiler_params=_cp(("parallel", "arbitrary")),
        name="inproj",
    )(x, w, b)


def _kvif_kernel(x_ref, w_ref, b_ref, kall_ref, vall_ref, k_ref, v_ref, if_ref):
    del kall_ref, vall_ref
    tm = x_ref.shape[0]
    r = _dot(x_ref[...].astype(BF16), w_ref[...]) + b_ref[...]
    for h in range(SB_HEADS):
        k_ref[pl.ds(h, tm, stride=SB_HEADS), :] = r[:, h * SB_HEAD_DIM:(h + 1) * SB_HEAD_DIM]
        v_ref[pl.ds(h, tm, stride=SB_HEADS), :] = r[:, SB_WIDTH + h * SB_HEAD_DIM:SB_WIDTH + (h + 1) * SB_HEAD_DIM]
    if_ref[...] = r[:, 2 * SB_WIDTH:]


def _kvif(x, w, b, k_all, v_all, layer, tm):
    T, D = x.shape
    nt = T // tm
    kv_spec = pl.BlockSpec((SB_HEADS * tm, SB_HEAD_DIM), lambda i: (layer * nt + i, 0))
    return pl.pallas_call(
        _kvif_kernel,
        grid=(nt,),
        in_specs=[pl.BlockSpec((tm, D), lambda i: (i, 0)),
                  pl.BlockSpec((D, N_KVIF), lambda i: (0, 0)),
                  pl.BlockSpec((1, N_KVIF), lambda i: (0, 0)),
                  pl.BlockSpec(memory_space=pl.ANY),
                  pl.BlockSpec(memory_space=pl.ANY)],
        out_specs=[kv_spec, kv_spec, pl.BlockSpec((tm, LANES), lambda i: (i, 0))],
        out_shape=[jax.ShapeDtypeStruct(k_all.shape, F32),
                   jax.ShapeDtypeStruct(v_all.shape, F32),
                   jax.ShapeDtypeStruct((T, LANES), F32)],
        input_output_aliases={3: 0, 4: 1},
        compiler_params=_cp(("parallel",)),
        name="inproj_kvif",
    )(x, w, b, k_all, v_all)


def _unwritten(shape):
    return pl.pallas_call(lambda o_ref: None, out_specs=pl.BlockSpec(memory_space=pl.ANY),
                          out_shape=jax.ShapeDtypeStruct(shape, F32), name="kv_state_buffer")()


def _lru_kernel(ax_ref, ag_ref, cbuf_ref, h0_ref, cw_ref, cb_ref, wr_ref, br_ref, wi_ref, bi_ref,
                lam_ref, ya_ref, hl_ref, ext_ref, hc_ref, *, first_frame):
    tl = ax_ref.shape[1]
    li = pl.program_id(1)

    @pl.when(li == 0)
    def _():
        ext_ref[0:8, :] = cbuf_ref[0]
        hc_ref[...] = h0_ref[0]

    ext_ref[8:8 + tl, :] = ax_ref[0].astype(F32)
    ua = cb_ref[...] + ext_ref[5:5 + tl, :] * cw_ref[0:1, :]
    for j in range(1, CONV_W):
        ua = ua + ext_ref[5 + j:5 + j + tl, :] * cw_ref[j:j + 1, :]
    ext_ref[5:8, :] = ext_ref[5 + tl:8 + tl, :]

    uab = ua.astype(BF16)
    ng = D_RNN // LRU_GROUP
    r_pre = jnp.concatenate(
        [_dot(uab[:, g * LRU_GROUP:(g + 1) * LRU_GROUP], wr_ref[g]) for g in range(ng)], axis=1)
    i_pre = jnp.concatenate(
        [_dot(uab[:, g * LRU_GROUP:(g + 1) * LRU_GROUP], wi_ref[g]) for g in range(ng)], axis=1)
    r = _sigmoid(r_pre + br_ref[...])
    ig = _sigmoid(i_pre + bi_ref[...])
    log_a = (-LRU_C) * r * _softplus(-lam_ref[...])
    a = jnp.exp(log_a)
    th = jnp.tanh(log_a)
    mult = jnp.sqrt(-2.0 * th / (1.0 - th))
    row = lax.broadcasted_iota(jnp.int32, (tl, D_RNN), 0)
    if first_frame:
        mult = jnp.where((row == 0) & (li == 0), 1.0, mult)
    bx = mult * (ig * ua)

    a = a.reshape(tl // 8, 8, D_RNN)
    bx = bx.reshape(tl // 8, 8, D_RNN)
    sub = lax.broadcasted_iota(jnp.int32, (tl // 8, 8, D_RNN), 1)
    for d in (1, 2, 4):
        a_sh = jnp.where(sub < d, 1.0, pltpu.roll(a, d, 1))
        b_sh = jnp.where(sub < d, 0.0, pltpu.roll(bx, d, 1))
        bx = a * b_sh + bx
        a = a * a_sh
    carry = hc_ref[...]
    groups = []
    for g in range(tl // 8):
        hg = bx[g] + a[g] * carry
        carry = hg[7:8, :]
        groups.append(hg)
    h = jnp.concatenate(groups, axis=0)
    hc_ref[...] = carry
    hl_ref[0] = carry

    gx = ag_ref[0].astype(F32)
    gelu = 0.5 * gx * (1.0 + jnp.tanh(0.7978845608028654 * (gx + 0.044715 * (gx * gx * gx))))
    ya_ref[0] = (h * gelu).astype(ya_ref.dtype)


def _lru(z3, cbuf8, h0, cw8, cb, wr, br, wi, bi, lam, tl, first_frame):
    B, S, _ = z3.shape
    full = lambda shape: pl.BlockSpec(shape, lambda b, l: (0,) * len(shape))
    return pl.pallas_call(
        functools.partial(_lru_kernel, first_frame=first_frame),
        grid=(B, S // tl),
        in_specs=[pl.BlockSpec((1, tl, D_RNN), lambda b, l: (b, l, OFF_AX // D_RNN)),
                  pl.BlockSpec((1, tl, D_RNN), lambda b, l: (b, l, OFF_AG // D_RNN)),
                  pl.BlockSpec((1, 8, D_RNN), lambda b, l: (b, 0, 0)),
                  pl.BlockSpec((1, 1, D_RNN), lambda b, l: (b, 0, 0)),
                  full((8, D_RNN)), full((1, D_RNN)),
                  full((D_RNN // LRU_GROUP, LRU_GROUP, LRU_GROUP)), full((1, D_RNN)),
                  full((D_RNN // LRU_GROUP, LRU_GROUP, LRU_GROUP)), full((1, D_RNN)),
                  full((1, D_RNN))],
        out_specs=[pl.BlockSpec((1, tl, D_RNN), lambda b, l: (b, l, 0)),
                   pl.BlockSpec((1, 1, D_RNN), lambda b, l: (b, 0, 0))],
        out_shape=[jax.ShapeDtypeStruct((B, S, D_RNN), BF16),
                   jax.ShapeDtypeStruct((B, 1, D_RNN), F32)],
        scratch_shapes=[pltpu.VMEM((tl + 8, D_RNN), F32), pltpu.VMEM((1, D_RNN), F32)],
        compiler_params=_cp(("parallel", "arbitrary")),
        name="conv_rglru",
    )(z3, z3, cbuf8, h0, cw8, cb, wr, br, wi, bi, lam)


def _sb_kernel(q_ref, k_ref, v_ref, kn_ref, vn_ref, o_ref, acc_ref, c_ref, *, tq, q_off, n_new):
    tk = KEY_BLOCK
    rs = min(tq, tk)
    ns = tq // rs
    qi = pl.program_id(2)
    q = (q_ref[0].astype(F32) * (SB_HEAD_DIM ** -0.5)).astype(BF16)
    qpos = q_off + qi * tq + lax.broadcasted_iota(jnp.int32, (tq, tk), 0)
    lane = lax.broadcasted_iota(jnp.int32, (rs, tk), 1)
    jj = lax.broadcasted_iota(jnp.int32, (tk, tk), 0)
    ll = lax.broadcasted_iota(jnp.int32, (tk, tk), 1)
    suffix = jnp.where(jj >= ll, 1.0, 0.0).astype(BF16)
    acc_ref[...] = jnp.zeros_like(acc_ref)
    c_ref[...] = jnp.zeros_like(c_ref)
    diag = [(q_off + qi * tq + (i + 1) * rs - 2) // tk for i in range(ns)]
    never = jnp.int32(2 ** 30)
    group = lax.broadcasted_iota(jnp.int32, (tq, 1), 0) // rs
    diag_row = jnp.zeros((tq, 1), jnp.int32)
    for i in range(ns):
        diag_row = jnp.where(group == i, diag[i], diag_row)

    def cond(carry):
        s, cmin = carry
        return (s <= diag[ns - 1]) & (cmin < SB_DEAD_MASS)

    head = pl.program_id(1)

    def head_rows(ref, start, n):
        return ref[0, pl.ds(SB_HEADS * start + head, n, stride=SB_HEADS), :]

    def step(s, ks, vs, kpos):
        zs = [_dot_nt(q[i * rs:(i + 1) * rs, :], ks[i]) for i in range(ns)]
        z = jnp.concatenate(zs, axis=0) if ns > 1 else zs[0]
        earlier = (jnp.concatenate(kpos, axis=0) if ns > 1 else kpos[0]) < qpos
        u = jnp.where(earlier, _softplus(z), 0.0)
        u_hi = u.astype(BF16)
        u_lo = (u - u_hi.astype(F32)).astype(BF16)
        suf = _dot(u_hi, suffix) + _dot(u_lo, suffix)
        c = c_ref[...]
        w = jnp.where(earlier, jnp.exp(z - suf - c), 0.0).astype(BF16)
        pv = [_dot(w[i * rs:(i + 1) * rs, :], vs[i]) for i in range(ns)]
        acc_ref[...] += jnp.concatenate(pv, axis=0) if ns > 1 else pv[0]
        c_new = c + suf[:, 0:1]
        c_ref[...] = c_new
        return s + 1, jnp.min(jnp.where(diag_row > s, c_new, jnp.inf))

    def body(carry):
        s, _ = carry
        ks, vs, kpos = [], [], []
        for i in range(ns):
            kb = diag[i] - s
            start = pl.multiple_of(jnp.maximum(kb, 0) * tk, tk)
            ks.append(head_rows(k_ref, start, tk).astype(BF16))
            vs.append(head_rows(v_ref, start, tk).astype(BF16))
            kpos.append(jnp.where(kb >= 0, start, never) + lane)
        return step(s, ks, vs, kpos)

    first = (jnp.int32(0), jnp.float32(0.0))
    if n_new:
        pad = jnp.zeros((tk - n_new, SB_HEAD_DIM), BF16)
        first = step(jnp.int32(0), [jnp.concatenate([head_rows(kn_ref, 0, n_new).astype(BF16), pad], axis=0)],
                     [jnp.concatenate([head_rows(vn_ref, 0, n_new).astype(BF16), pad], axis=0)],
                     [diag[0] * tk + lane])
    lax.while_loop(cond, body, first)
    o_ref[0] = acc_ref[...].astype(o_ref.dtype)


def _sb_attention(q_arr, k_arr, v_arr, kn_arr, vn_arr, q_cb, layer, tq, q_off, n_new):
    B, Tq, _ = q_arr.shape
    Tk = k_arr.shape[1] // SB_HEADS
    nrows = SB_HEADS * (n_new if n_new else 8)
    assert n_new == 0 or (Tq == tq == n_new <= KEY_BLOCK and q_off % KEY_BLOCK == 0 and q_off == Tk)
    return pl.pallas_call(
        functools.partial(_sb_kernel, tq=tq, q_off=q_off, n_new=n_new),
        grid=(B, SB_HEADS, Tq // tq),
        in_specs=[pl.BlockSpec((1, tq, SB_HEAD_DIM), lambda b, h, i: (b, i, q_cb + h)),
                  pl.BlockSpec((1, Tk * SB_HEADS, SB_HEAD_DIM), lambda b, h, i: (layer * B + b, 0, 0)),
                  pl.BlockSpec((1, Tk * SB_HEADS, SB_HEAD_DIM), lambda b, h, i: (layer * B + b, 0, 0)),
                  pl.BlockSpec((1, nrows, SB_HEAD_DIM), lambda b, h, i: (layer * B + b, 0, 0)),
                  pl.BlockSpec((1, nrows, SB_HEAD_DIM), lambda b, h, i: (layer * B + b, 0, 0))],
        out_specs=pl.BlockSpec((1, tq, SB_HEAD_DIM), lambda b, h, i: (b, i, h)),
        out_shape=jax.ShapeDtypeStruct((B, Tq, SB_WIDTH), BF16),
        scratch_shapes=[pltpu.VMEM((tq, SB_HEAD_DIM), F32), pltpu.VMEM((tq, 1), F32)],
        compiler_params=_cp(("parallel", "parallel", "arbitrary")),
        name="stick_breaking",
    )(q_arr, k_arr, v_arr, kn_arr, vn_arr)


def _mlstm_kernel(mq_ref, mk_ref, mv_ref, mo_ref, if_ref, c0_ref, n0_ref, m0_ref, g_ref,
                  yc_ref, c_out, n_out, m_out, c_s, n_s, m_s, *, ck):
    tl = mq_ref.shape[1]
    nh = ML_HEADS
    li = pl.program_id(1)

    @pl.when(li == 0)
    def _():
        c_s[...] = c0_ref[0]
        n_s[...] = n0_ref[0]
        m_s[...] = jnp.broadcast_to(m0_ref[0], (nh, 1, LANES))

    t_i = lax.broadcasted_iota(jnp.int32, (nh, ck, ck), 1)
    s_i = lax.broadcasted_iota(jnp.int32, (nh, ck, ck), 2)
    causal = s_i <= t_i
    diag = s_i == t_i
    gain = jnp.stack([g_ref[:, h * ML_V_DIM:(h + 1) * ML_V_DIM] for h in range(nh)])

    def heads(ref, r0, width):
        return jnp.stack([ref[0, pl.ds(r0, ck), h * width:(h + 1) * width] for h in range(nh)]).astype(F32)

    def chunk(ci, carry):
        r0 = pl.multiple_of(ci * ck, ck)
        ifb = if_ref[0, pl.ds(r0, ck), :]
        ig_col = jnp.stack([ifb[:, h:h + 1] for h in range(nh)])
        lf_col = -_softplus(-jnp.stack([ifb[:, nh + h:nh + h + 1] for h in range(nh)]))
        q4 = heads(mq_ref, r0, ML_QK_DIM)
        k4 = heads(mk_ref, r0, ML_QK_DIM) * (ML_QK_DIM ** -0.5)
        v4 = heads(mv_ref, r0, ML_V_DIM)
        o4 = heads(mo_ref, r0, ML_V_DIM)
        ig_row = jnp.sum(jnp.where(diag, ig_col, 0.0), axis=1, keepdims=True)
        lf_row = jnp.sum(jnp.where(diag, lf_col, 0.0), axis=1, keepdims=True)
        b_col = jnp.sum(jnp.where(causal, lf_row, 0.0), axis=2, keepdims=True)
        b_row = jnp.sum(jnp.where(t_i <= s_i, lf_col, 0.0), axis=1, keepdims=True)
        m_prev = m_s[:, :, 0:1]
        dmat = jnp.where(causal, b_col - b_row + ig_row, -jnp.inf)
        inter = b_col + m_prev
        m_t = jnp.maximum(inter, jnp.max(dmat, axis=2, keepdims=True))
        s_inter = jnp.exp(inter - m_t)
        qb = q4.astype(BF16)
        kb = k4.astype(BF16)
        vb = v4.astype(BF16)
        wqk = jnp.exp(dmat - m_t) * jnp.stack([_dot_nt(qb[h], kb[h]) for h in range(nh)])
        c_prev = c_s[...]
        n_prev = n_s[...]
        cb = c_prev.astype(BF16)
        wb = wqk.astype(BF16)
        num = (s_inter * jnp.stack([_dot(qb[h], cb[h]) for h in range(nh)])
               + jnp.stack([_dot(wb[h], vb[h]) for h in range(nh)]))
        den = (s_inter * jnp.sum(q4 * n_prev, axis=2, keepdims=True)
               + jnp.sum(wqk, axis=2, keepdims=True))
        hh = num / jnp.maximum(jnp.abs(den), jnp.exp(-m_t))
        b_end = b_col[:, ck - 1:ck, :]
        g_col = b_end - b_col + ig_col
        m_new = jnp.maximum(b_end + m_prev, jnp.max(g_col, axis=1, keepdims=True))
        s_old = jnp.exp(b_end + m_prev - m_new)
        kw = k4 * jnp.exp(g_col - m_new)
        kwb = kw.astype(BF16)
        c_s[...] = s_old * c_prev + jnp.stack([_dot_tn(kwb[h], vb[h]) for h in range(nh)])
        n_s[...] = s_old * n_prev + jnp.sum(kw, axis=1, keepdims=True)
        m_s[...] = jnp.broadcast_to(m_new, (nh, 1, LANES))
        mu = jnp.mean(hh, axis=2, keepdims=True)
        dlt = hh - mu
        var = jnp.mean(dlt * dlt, axis=2, keepdims=True)
        out = (dlt * lax.rsqrt(var + LN_EPS) * gain * _sigmoid(o4)).astype(yc_ref.dtype)
        for h in range(nh):
            yc_ref[0, pl.ds(r0, ck), h * ML_V_DIM:(h + 1) * ML_V_DIM] = out[h]
        return carry

    lax.fori_loop(0, tl // ck, chunk, 0)
    c_out[0] = c_s[...]
    n_out[0] = n_s[...]
    m_out[0] = m_s[:, :, 0:1]


def _mlstm(z3, if3, c0, n0, m0p, g, tl, ck):
    B, S, _ = z3.shape
    return pl.pallas_call(
        functools.partial(_mlstm_kernel, ck=ck),
        grid=(B, S // tl),
        in_specs=[pl.BlockSpec((1, tl, ML_QK_WIDTH), lambda b, l: (b, l, OFF_MQ // ML_QK_WIDTH)),
                  pl.BlockSpec((1, tl, ML_QK_WIDTH), lambda b, l: (b, l, OFF_MK // ML_QK_WIDTH)),
                  pl.BlockSpec((1, tl, ML_V_WIDTH), lambda b, l: (b, l, OFF_MV // ML_V_WIDTH)),
                  pl.BlockSpec((1, tl, ML_V_WIDTH), lambda b, l: (b, l, OFF_MO // ML_V_WIDTH)),
                  pl.BlockSpec((1, tl, LANES), lambda b, l: (b, l, 0)),
                  pl.BlockSpec((1, ML_HEADS, ML_QK_DIM, ML_V_DIM), lambda b, l: (b, 0, 0, 0)),
                  pl.BlockSpec((1, ML_HEADS, 1, ML_QK_DIM), lambda b, l: (b, 0, 0, 0)),
                  pl.BlockSpec((1, ML_HEADS, 1, 1), lambda b, l: (b, 0, 0, 0)),
                  pl.BlockSpec((1, ML_V_WIDTH), lambda b, l: (0, 0))],
        out_specs=[pl.BlockSpec((1, tl, ML_V_WIDTH), lambda b, l: (b, l, 0)),
                   pl.BlockSpec((1, ML_HEADS, ML_QK_DIM, ML_V_DIM), lambda b, l: (b, 0, 0, 0)),
                   pl.BlockSpec((1, ML_HEADS, 1, ML_QK_DIM), lambda b, l: (b, 0, 0, 0)),
                   pl.BlockSpec((1, ML_HEADS, 1, 1), lambda b, l: (b, 0, 0, 0))],
        out_shape=[jax.ShapeDtypeStruct((B, S, ML_V_WIDTH), BF16),
                   jax.ShapeDtypeStruct((B, ML_HEADS, ML_QK_DIM, ML_V_DIM), F32),
                   jax.ShapeDtypeStruct((B, ML_HEADS, 1, ML_QK_DIM), F32),
                   jax.ShapeDtypeStruct((B, ML_HEADS, 1, 1), F32)],
        scratch_shapes=[pltpu.VMEM((ML_HEADS, ML_QK_DIM, ML_V_DIM), F32),
                        pltpu.VMEM((ML_HEADS, 1, ML_QK_DIM), F32),
                        pltpu.VMEM((ML_HEADS, 1, LANES), F32)],
        compiler_params=_cp(("parallel", "arbitrary")),
        name="mlstm",
    )(z3, z3, z3, z3, if3, c0, n0, m0p, g)


def _merge_kernel(ya_ref, yb_ref, yc_ref, g0_ref, g1_ref, g2_ref, x_ref, wpa_ref, wpb_ref, wpc_ref,
                  wo_ref, lg_ref, lb_ref, rw_ref, rb_ref, tri_ref,
                  o_ref, topi_ref, gate_ref, rank_ref, cnt_ref, run_ref, *, alpha):
    mixed = _sigmoid(g0_ref[...].astype(F32)) * _dot(ya_ref[...], wpa_ref[...])
    mixed = mixed + _sigmoid(g1_ref[...].astype(F32)) * _dot(yb_ref[...], wpb_ref[...])
    mixed = mixed + _sigmoid(g2_ref[...].astype(F32)) * _dot(yc_ref[...], wpc_ref[...])
    r = alpha * x_ref[...] + _dot(mixed.astype(BF16), wo_ref[...])
    x1 = _layer_norm(r, lg_ref[...], lb_ref[...])
    o_ref[...] = x1
    _route(x1, rw_ref, rb_ref, tri_ref, topi_ref, gate_ref, rank_ref, cnt_ref, run_ref)


def _merge(ya, yb, yc, z, x, wpa, wpb, wpc, wo, lg, lb, rw, rb, tri, tm, alpha):
    T = x.shape[0]
    full = lambda shape: pl.BlockSpec(shape, lambda i: (0,) * len(shape))
    tile = lambda width, col=0: pl.BlockSpec((tm, width), lambda i: (i, col))
    mgb = OFF_MG // D_MODEL
    return pl.pallas_call(
        functools.partial(_merge_kernel, alpha=alpha),
        grid=(T // tm,),
        in_specs=[tile(D_RNN), tile(SB_WIDTH), tile(ML_V_WIDTH),
                  tile(D_MODEL, mgb), tile(D_MODEL, mgb + 1), tile(D_MODEL, mgb + 2), tile(D_MODEL),
                  full((D_RNN, D_MODEL)), full((SB_WIDTH, D_MODEL)), full((ML_V_WIDTH, D_MODEL)),
                  full((D_MODEL, D_MODEL)), full((1, D_MODEL)), full((1, D_MODEL)),
                  full((D_MODEL, LANES)), full((1, LANES)), full((tm, tm))],
        out_specs=[tile(D_MODEL), tile(LANES), tile(LANES), tile(LANES), full((1, LANES))],
        out_shape=[jax.ShapeDtypeStruct((T, D_MODEL), F32),
                   jax.ShapeDtypeStruct((T, LANES), jnp.int32),
                   jax.ShapeDtypeStruct((T, LANES), F32),
                   jax.ShapeDtypeStruct((T, LANES), jnp.int32),
                   jax.ShapeDtypeStruct((1, LANES), jnp.int32)],
        scratch_shapes=[pltpu.VMEM((1, LANES), F32)],
        compiler_params=_cp(("arbitrary",)),
        name="merge_ln1_router",
    )(ya, yb, yc, z, z, z, x, wpa, wpb, wpc, wo, lg, lb, rw, rb, tri)


def _route(x, rw_ref, rb_ref, tri_ref, topi_ref, gate_ref, rank_ref, cnt_ref, run_ref):
    @pl.when(pl.program_id(0) == 0)
    def _():
        run_ref[...] = jnp.zeros_like(run_ref)

    tm = x.shape[0]
    l = _dot(x.astype(BF16), rw_ref[...]) + rb_ref[...]
    lane = lax.broadcasted_iota(jnp.int32, (tm, LANES), 1)
    lane_f = lane.astype(F32)
    vals, idxs, hots = [], [], []
    for _ in range(TOP_K):
        m = jnp.max(l, axis=1, keepdims=True)
        idx = jnp.min(jnp.where(l == m, lane_f, float(LANES)), axis=1, keepdims=True)
        hot = lane_f == idx
        vals.append(m)
        idxs.append(idx)
        hots.append(hot)
        l = jnp.where(hot, -jnp.inf, l)
    ex = [jnp.exp(v - vals[0]) for v in vals]
    den = ex[0] + ex[1] + ex[2] + ex[3]
    onehot = jnp.zeros((tm, LANES), F32)
    for hot in hots:
        onehot = onehot + jnp.where(hot, 1.0, 0.0)
    before = _dot(tri_ref[...], onehot.astype(BF16)) + run_ref[...]
    topi = jnp.zeros((tm, LANES), F32)
    gate = jnp.zeros((tm, LANES), F32)
    rank = jnp.zeros((tm, LANES), F32)
    for k in range(TOP_K):
        rk = jnp.sum(jnp.where(hots[k], before, 0.0), axis=1, keepdims=True)
        topi = jnp.where(lane == k, idxs[k], topi)
        gate = jnp.where(lane == k, ex[k] / den, gate)
        rank = jnp.where(lane == k, rk, rank)
    run_new = run_ref[...] + jnp.sum(onehot, axis=0, keepdims=True)
    run_ref[...] = run_new
    topi_ref[...] = topi.astype(jnp.int32)
    gate_ref[...] = gate
    rank_ref[...] = rank.astype(jnp.int32)
    cnt_ref[...] = run_new.astype(jnp.int32)


ROW_CHUNKS = D_MODEL // 2 // LANES
U32 = jnp.uint32


def _tile_copy(src_ref, src_row, dst_ref, dst_row, sem):
    src = src_ref.at[pl.ds(pl.multiple_of(src_row * ROW_CHUNKS, ROW_CHUNKS), ROW_CHUNKS)]
    dst = dst_ref.at[pl.ds(pl.multiple_of(dst_row * ROW_CHUNKS, ROW_CHUNKS), ROW_CHUNKS)]
    return pltpu.make_async_copy(src, dst, sem)


def _interleave_store(ref, row0, value):
    n = value.shape[0]
    bits = pltpu.bitcast(value.astype(BF16).astype(F32), U32)
    half = D_MODEL // 2
    packed = lax.shift_right_logical(bits[:, :half], U32(16)) | bits[:, half:]
    for c in range(ROW_CHUNKS):
        ref[pl.ds(row0 * ROW_CHUNKS + c, n, stride=ROW_CHUNKS), :] = packed[:, c * LANES:(c + 1) * LANES]


def _interleave_load(ref, row0, n):
    packed = jnp.concatenate(
        [ref[pl.ds(row0 * ROW_CHUNKS + c, n, stride=ROW_CHUNKS), :] for c in range(ROW_CHUNKS)], axis=1)
    lo = pltpu.bitcast(lax.shift_left(packed, U32(16)), F32)
    hi = pltpu.bitcast(packed & U32(0xFFFF0000), F32)
    return jnp.concatenate([lo, hi], axis=1)


def _dispatch_kernel(padlo_ref, padn_ref, dest_ref, x_ref, xs_hbm, xi, ztile, sem, *, ts):
    @pl.when(pl.program_id(0) == 0)
    def _():
        ztile[...] = jnp.zeros_like(ztile)

        def fill(e, c):
            lax.fori_loop(0, padn_ref[e],
                          lambda r, c2: (_tile_copy(ztile, 0, xs_hbm, padlo_ref[e] + r, sem).start(), c2)[1], 0)
            return c

        def fill_done(e, c):
            lax.fori_loop(0, padn_ref[e],
                          lambda r, c2: (_tile_copy(ztile, 0, xs_hbm, 0, sem).wait(), c2)[1], 0)
            return c

        lax.fori_loop(0, N_EXPERTS, fill, 0)
        lax.fori_loop(0, N_EXPERTS, fill_done, 0)

    _interleave_store(xi, 0, x_ref[...])

    def issue(t, c):
        for k in range(TOP_K):
            _tile_copy(xi, t, xs_hbm, dest_ref[TOP_K * t + k], sem).start(priority=k % 2)
        return c

    lax.fori_loop(0, ts, issue, 0)

    def drain(t, c):
        for k in range(TOP_K):
            _tile_copy(xi, 0, xs_hbm, 0, sem).wait()
        return c

    lax.fori_loop(0, ts, drain, 0)


def _dispatch(pad_lo, pad_n, dest_flat, x, rows, ts):
    T = x.shape[0]
    grid_spec = pltpu.PrefetchScalarGridSpec(
        num_scalar_prefetch=2,
        grid=(T // ts,),
        in_specs=[pl.BlockSpec((TOP_K * ts,), lambda i, lo, n: (i,), memory_space=pltpu.SMEM),
                  pl.BlockSpec((ts, D_MODEL), lambda i, lo, n: (i, 0))],
        out_specs=pl.BlockSpec(memory_space=pl.ANY),
        scratch_shapes=[pltpu.VMEM((ts * ROW_CHUNKS, LANES), U32), pltpu.VMEM((8, LANES), U32),
                        pltpu.SemaphoreType.DMA(())],
    )
    return pl.pallas_call(
        functools.partial(_dispatch_kernel, ts=ts),
        grid_spec=grid_spec,
        out_shape=jax.ShapeDtypeStruct((rows * ROW_CHUNKS, LANES), U32),
        compiler_params=_cp(("arbitrary",)),
        name="moe_dispatch",
    )(pad_lo, pad_n, dest_flat, x)


def _experts_kernel(be_ref, nu_ref, xs_ref, w1_ref, b1_ref, w2_ref, b2_ref, y_ref, w1b_ref, w2b_ref, *, m):
    j = pl.program_id(0)

    @pl.when((j == 0) | (be_ref[j] != be_ref[jnp.maximum(j - 1, 0)]))
    def _():
        w1b_ref[...] = w1_ref[0, 0].astype(BF16)
        w2b_ref[...] = w2_ref[0, 0].astype(BF16)

    @pl.when(j < nu_ref[0])
    def _():
        gu = _dot(_interleave_load(xs_ref, 0, m).astype(BF16), w1b_ref[...]) + b1_ref[0]
        g_ = jnp.minimum(gu[:, :D_FF], SWIGLU_LIMIT)
        up = jnp.clip(gu[:, D_FF:], -SWIGLU_LIMIT, SWIGLU_LIMIT)
        act = (up + 1.0) * g_ * _sigmoid(SWIGLU_ALPHA * g_)
        _interleave_store(y_ref, 0, _dot(act.astype(BF16), w2b_ref[...]) + b2_ref[0])

    @pl.when(pl.program_id(0) >= nu_ref[0])
    def _():
        y_ref[...] = jnp.zeros_like(y_ref)


def _experts(block_e, n_used, xs, w1, b1, w2, b2, m, layer):
    nblk = xs.shape[0] // (m * ROW_CHUNKS)
    grid_spec = pltpu.PrefetchScalarGridSpec(
        num_scalar_prefetch=2,
        grid=(nblk,),
        in_specs=[pl.BlockSpec((m * ROW_CHUNKS, LANES), lambda j, be, nu: (jnp.minimum(j, nu[0] - 1), 0)),
                  pl.BlockSpec((1, 1, D_MODEL, 2 * D_FF), lambda j, be, nu: (layer, be[j], 0, 0)),
                  pl.BlockSpec((1, 1, 2 * D_FF), lambda j, be, nu: (be[j], 0, 0)),
                  pl.BlockSpec((1, 1, D_FF, D_MODEL), lambda j, be, nu: (layer, be[j], 0, 0)),
                  pl.BlockSpec((1, 1, D_MODEL), lambda j, be, nu: (be[j], 0, 0))],
        out_specs=pl.BlockSpec((m * ROW_CHUNKS, LANES), lambda j, be, nu: (j, 0)),
        scratch_shapes=[pltpu.VMEM((D_MODEL, 2 * D_FF), BF16), pltpu.VMEM((D_FF, D_MODEL), BF16)],
    )
    return pl.pallas_call(
        functools.partial(_experts_kernel, m=m),
        grid_spec=grid_spec,
        out_shape=jax.ShapeDtypeStruct(xs.shape, U32),
        compiler_params=_cp(("arbitrary",)),
        name="moe_experts",
    )(block_e, n_used, xs, w1, b1, w2, b2)


def _combine_kernel(dest_a, dest_b, dest_next, x_ref, p_ref, gate_ref, y_hbm, wp_ref, wg_ref, bg_ref,
                    lg_ref, lb_ref, o_ref, ybuf_a, ybuf_b, sem_a, sem_b, *, alpha, tm, halves):
    i = pl.program_id(0)
    bufs = [(ybuf_a, sem_a), (ybuf_b, sem_b)][:halves]
    dests = [dest_a, dest_b][:halves]

    def issue(dest_ref, buf, sem):
        def one(t, c):
            for k in range(TOP_K):
                _tile_copy(y_hbm, dest_ref[TOP_K * t + k], buf, k * tm + t, sem).start(priority=k % 2)
            return c

        lax.fori_loop(0, tm, one, 0)

    def drain(buf, sem):
        def one(t, c):
            for k in range(TOP_K):
                _tile_copy(y_hbm, 0, buf, 0, sem).wait()
            return c

        lax.fori_loop(0, tm, one, 0)

    @pl.when(i == 0)
    def _():
        issue(dests[0], *bufs[0])

    for h in range(halves):
        if h + 1 < halves:
            issue(dests[h + 1], *bufs[h + 1])
        rows = pl.ds(h * tm, tm)
        x = x_ref[rows, :]
        ple = (_sigmoid(_dot(x.astype(BF16), wg_ref[...]) + bg_ref[...])
               * _dot(p_ref[rows, :].astype(BF16), wp_ref[...]))
        r = alpha * x + ple
        drain(*bufs[h])
        for k in range(TOP_K):
            r = r + gate_ref[rows, k:k + 1] * _interleave_load(bufs[h][0], k * tm, tm)
        o_ref[rows, :] = _layer_norm(r, lg_ref[...], lb_ref[...])
        if h == 0:
            @pl.when(i + 1 < pl.num_programs(0))
            def _():
                issue(dest_next, *bufs[0])


def _combine(dest_flat, x, p, gate, y, wp, wg, bg, lg, lb, tm, alpha):
    T = x.shape[0]
    ntiles = T // tm
    halves = 2 if ntiles % 2 == 0 else 1
    tb = halves * tm
    full = lambda shape: pl.BlockSpec(shape, lambda i: (0,) * len(shape))
    dest_spec = lambda fn: pl.BlockSpec((TOP_K * tm,), fn, memory_space=pltpu.SMEM)
    return pl.pallas_call(
        functools.partial(_combine_kernel, alpha=alpha, tm=tm, halves=halves),
        grid=(ntiles // halves,),
        in_specs=[dest_spec(lambda i: (halves * i,)),
                  dest_spec(lambda i: (halves * i + halves - 1,)),
                  dest_spec(lambda i: (jnp.minimum(halves * (i + 1), ntiles - 1),)),
                  pl.BlockSpec((tb, D_MODEL), lambda i: (i, 0)),
                  pl.BlockSpec((tb, D_PLE), lambda i: (i, 0)),
                  pl.BlockSpec((tb, LANES), lambda i: (i, 0)),
                  pl.BlockSpec(memory_space=pl.ANY),
                  full((D_PLE, D_MODEL)), full((D_MODEL, D_MODEL)), full((1, D_MODEL)),
                  full((1, D_MODEL)), full((1, D_MODEL))],
        out_specs=pl.BlockSpec((tb, D_MODEL), lambda i: (i, 0)),
        out_shape=jax.ShapeDtypeStruct((T, D_MODEL), F32),
        scratch_shapes=[pltpu.VMEM((TOP_K * tm * ROW_CHUNKS, LANES), U32),
                        pltpu.VMEM((TOP_K * tm * ROW_CHUNKS, LANES), U32),
                        pltpu.SemaphoreType.DMA(()), pltpu.SemaphoreType.DMA(())],
        compiler_params=_cp(("arbitrary",)),
        name="moe_combine_ln2",
    )(dest_flat, dest_flat, dest_flat, x, p, gate, y, wp, wg, bg, lg, lb)


def _pick(n, pref):
    t = min(n, pref)
    while n % t:
        t //= 2
    return t


def _layer(x, p, lw, conv_buf, lru_h, ml_c, ml_n, ml_m, k_past, v_past, k_all, v_all, prompt, alpha, layer):
    B, L, _ = x.shape
    T = B * L
    xf = x.reshape(T, D_MODEL)

    z = _inproj(xf, lw["w_in"], lw["b_in"], _pick(T, 2048), 1024)
    k_all, v_all, ifg = _kvif(xf, lw["w_kvif"], lw["b_kvif"], k_all, v_all, layer, _pick(T, 1024))
    z3 = z.reshape(B, L, N_MAIN)
    k_rows = k_all.reshape(-1, L * SB_HEADS, SB_HEAD_DIM)
    v_rows = v_all.reshape(-1, L * SB_HEADS, SB_HEAD_DIM)
    conv_new = z3[:, L - (CONV_W - 1):, OFF_AX:OFF_AX + D_RNN].astype(F32)

    cbuf8 = jnp.concatenate([jnp.zeros((B, 8 - (CONV_W - 1), D_RNN), F32), conv_buf.astype(F32)], axis=1)
    ya, lru_new = _lru(z3, cbuf8, lru_h.reshape(B, 1, D_RNN).astype(F32), lw["conv_w8"], lw["conv_b"],
                       lw["wr"], lw["br"], lw["wi"], lw["bi"], lw["lam"], _pick(L, 256), prompt)

    if prompt:
        yb = _sb_attention(z3, k_rows, v_rows, k_rows, v_rows, OFF_Q // SB_HEAD_DIM, layer,
                           _pick(L, 1024), 0, 0)
    else:
        yb = _sb_attention(z3, k_past, v_past, k_rows, v_rows, OFF_Q // SB_HEAD_DIM, layer,
                           L, k_past.shape[1] // SB_HEADS, L)

    ck = _pick(L, 128)
    yc, c_new, n_new, m_new = _mlstm(z3, ifg.reshape(B, L, LANES), ml_c.astype(F32),
                                     ml_n.astype(F32).reshape(B, ML_HEADS, 1, ML_QK_DIM),
                                     ml_m.astype(F32).reshape(B, ML_HEADS, 1, 1),
                                     lw["ml_g"], _pick(L, 512), ck)
    n_new = n_new.reshape(B, ML_HEADS, ML_QK_DIM)
    m_new = m_new.reshape(B, ML_HEADS)

    tr = _pick(T, 512)
    tri = jnp.tril(jnp.ones((tr, tr), BF16), -1)
    x1, topi, gate, rank, cnt = _merge(
        ya.reshape(T, D_RNN), yb.reshape(T, SB_WIDTH), yc.reshape(T, ML_V_WIDTH), z, xf,
        lw["w_pa"], lw["w_pb"], lw["w_pc"], lw["w_out"], lw["ln1_g"], lw["ln1_b"],
        lw["router_w"], lw["router_b"], tri, tr, alpha)

    TK = T * TOP_K
    m_rows = max(16, min(512, TK // N_EXPERTS))
    nb = TK // m_rows + N_EXPERTS
    counts = cnt[0, :N_EXPERTS]
    padded = (counts + (m_rows - 1)) // m_rows * m_rows
    pend = jnp.cumsum(padded)
    off = (pend - padded).astype(jnp.int32)
    blk_start = jnp.arange(nb, dtype=jnp.int32) * m_rows
    block_e = jnp.minimum(jnp.sum((pend[None, :] <= blk_start[:, None]).astype(jnp.int32), axis=1),
                          N_EXPERTS - 1).astype(jnp.int32)
    n_used = (pend[-1:] // m_rows).astype(jnp.int32)
    experts = jnp.arange(N_EXPERTS, dtype=jnp.int32)
    top4 = topi[:, :TOP_K]
    dest = rank[:, :TOP_K] + jnp.sum(jnp.where(top4[:, :, None] == experts, off, 0), axis=-1)
    dest_flat = dest.reshape(TK).astype(jnp.int32)
    xs = _dispatch(off + counts, (padded - counts).astype(jnp.int32), dest_flat, x1, nb * m_rows,
                   _pick(T, 256))
    y = _experts(block_e, n_used, xs, lw["exp_w1"], lw["exp_b1"], lw["exp_w2"], lw["exp_b2"], m_rows, layer)
    x2 = _combine(dest_flat, x1, p.reshape(T, D_PLE), gate, y, lw["ple_w"], lw["ple_gate_w"],
                  lw["ple_gate_b"], lw["ln2_g"], lw["ln2_b"], _pick(T, 256), alpha)

    return (x2.reshape(B, L, D_MODEL), k_all, v_all,
            (conv_new, lru_new.reshape(B, D_RNN), c_new, n_new, m_new))


def _block_diag_groups(w):
    per = LRU_GROUP // LRU_BLOCK
    w4 = w.reshape(D_RNN // LRU_GROUP, per, LRU_BLOCK, LRU_BLOCK)
    eye = jnp.eye(per, dtype=w.dtype)
    return jnp.einsum("gacd,ab->gacbd", w4, eye).reshape(D_RNN // LRU_GROUP, LRU_GROUP, LRU_GROUP)


def _split_in(w):
    main = jnp.concatenate([w[..., :ORIG_K], w[..., ORIG_MQ:ORIG_IF], w[..., ORIG_MG:]], axis=-1)
    pad = jnp.zeros(w.shape[:-1] + (LANES - 2 * ML_HEADS,), w.dtype)
    kvif = jnp.concatenate([w[..., ORIG_K:ORIG_MQ], w[..., ORIG_IF:ORIG_MG], pad], axis=-1)
    return main, kvif


def kernel(x_prompt, x_sample, cache_sb_k, cache_sb_v, state_conv, state_lru, state_mlstm_c, state_mlstm_n, state_mlstm_m, p_prompt, p_sample, w_in, b_in, conv_w, conv_b, lru_wr, lru_br, lru_wi, lru_bi, lru_lambda, ml_norm_g, w_pa, w_pb, w_pc, w_out, ln1_g, ln1_b, router_w, router_b, exp_w1, exp_b1, exp_w2, exp_b2, ple_w, ple_gate_w, ple_gate_b, ln2_g, ln2_b):
    depth = w_in.shape[0]
    alpha = (2 * depth) ** 0.25
    B = x_prompt.shape[0]
    y_prompt, y_sample = x_prompt, x_sample
    S = x_prompt.shape[1]
    Bs, Ls = x_sample.shape[:2]
    P = cache_sb_k.shape[2]
    k_cache = cache_sb_k.astype(F32).reshape(depth * Bs, P * SB_HEADS, SB_HEAD_DIM)
    v_cache = cache_sb_v.astype(F32).reshape(depth * Bs, P * SB_HEADS, SB_HEAD_DIM)
    kp, vp = (_unwritten((depth * B * S * SB_HEADS, SB_HEAD_DIM)) for _ in range(2))
    ks, vs = (_unwritten((depth * Bs * Ls * SB_HEADS, SB_HEAD_DIM)) for _ in range(2))
    st_p, st_s = [], []
    for i in range(depth):
        row = lambda a: a[i].reshape(1, -1).astype(F32)
        w_main, w_kvif = _split_in(w_in[i])
        b_main, b_kvif = _split_in(b_in[i])
        lw = dict(
            w_in=w_main.astype(BF16), b_in=b_main.reshape(1, -1).astype(F32),
            w_kvif=w_kvif.astype(BF16), b_kvif=b_kvif.reshape(1, -1).astype(F32),
            conv_w8=jnp.pad(conv_w[i].astype(F32), ((0, 8 - CONV_W), (0, 0))), conv_b=row(conv_b),
            wr=_block_diag_groups(lru_wr[i]).astype(BF16), br=row(lru_br),
            wi=_block_diag_groups(lru_wi[i]).astype(BF16), bi=row(lru_bi), lam=row(lru_lambda),
            ml_g=row(ml_norm_g),
            w_pa=w_pa[i].astype(BF16), w_pb=w_pb[i].astype(BF16), w_pc=w_pc[i].astype(BF16),
            w_out=w_out[i].astype(BF16), ln1_g=row(ln1_g), ln1_b=row(ln1_b),
            router_w=jnp.pad(router_w[i], ((0, 0), (0, LANES - N_EXPERTS))).astype(BF16),
            router_b=jnp.pad(router_b[i].astype(F32), (0, LANES - N_EXPERTS),
                             constant_values=-1e30).reshape(1, LANES),
            exp_w1=exp_w1.astype(F32), exp_b1=exp_b1[i].reshape(N_EXPERTS, 1, 2 * D_FF).astype(F32),
            exp_w2=exp_w2.astype(F32), exp_b2=exp_b2[i].reshape(N_EXPERTS, 1, D_MODEL).astype(F32),
            ple_w=ple_w[i].astype(BF16), ple_gate_w=ple_gate_w[i].astype(BF16), ple_gate_b=row(ple_gate_b),
            ln2_g=row(ln2_g), ln2_b=row(ln2_b),
        )
        y_prompt, kp, vp, sp = _layer(y_prompt, p_prompt[i], lw,
                                      jnp.zeros((B, CONV_W - 1, D_RNN), F32), jnp.zeros((B, D_RNN), F32),
                                      jnp.zeros((B, ML_HEADS, ML_QK_DIM, ML_V_DIM), F32),
                                      jnp.zeros((B, ML_HEADS, ML_QK_DIM), F32), jnp.zeros((B, ML_HEADS), F32),
                                      None, None, kp, vp, True, alpha, i)
        y_sample, ks, vs, ss = _layer(y_sample, p_sample[i], lw, state_conv[i], state_lru[i], state_mlstm_c[i],
                                      state_mlstm_n[i], state_mlstm_m[i], k_cache, v_cache, ks, vs,
                                      False, alpha, i)
        st_p.append(sp)
        st_s.append(ss)
    conv_p, lru_p, c_p, n_p, m_p = [jnp.stack(s) for s in zip(*st_p)]
    conv_s, lru_s, c_s, n_s, m_s = [jnp.stack(s) for s in zip(*st_s)]
    kv_p = (depth, B, S, SB_HEADS, SB_HEAD_DIM)
    kv_s = (depth, Bs, Ls, SB_HEADS, SB_HEAD_DIM)
    return (y_prompt, y_sample, kp.reshape(kv_p), vp.reshape(kv_p), conv_p, lru_p, c_p, n_p, m_p,
            ks.reshape(kv_s), vs.reshape(kv_s), conv_s, lru_s, c_s, n_s, m_s)
```

```python
import functools

import jax
import jax.numpy as jnp
from jax import lax
from jax.experimental import pallas as pl
from jax.experimental.pallas import tpu as pltpu

F32 = jnp.float32
BF16 = jnp.bfloat16

D_MODEL = 1024
D_RNN = 1024
N_LRU_BLOCKS = 16
LRU_BLOCK = D_RNN // N_LRU_BLOCKS
LRU_GROUP = 256
CONV_W = 4
LRU_C = 8.0
SB_HEADS = 4
SB_HEAD_DIM = 128
SB_WIDTH = SB_HEADS * SB_HEAD_DIM
ML_HEADS = 4
ML_QK_DIM = 64
ML_V_DIM = 128
ML_QK_WIDTH = ML_HEADS * ML_QK_DIM
ML_V_WIDTH = ML_HEADS * ML_V_DIM
N_BRANCH = 3
N_EXPERTS = 32
TOP_K = 4
D_FF = 512
SWIGLU_LIMIT = 7.0
SWIGLU_ALPHA = 1.702
D_PLE = 256
LN_EPS = 1e-5
LANES = 128
KEY_BLOCK = 128

OFF_AX, OFF_AG = 0, 1024
OFF_Q = 2048
OFF_MQ, OFF_MK, OFF_MV, OFF_MO = 2560, 2816, 3072, 3584
OFF_MG = 4096
N_MAIN = 7168
N_KVIF = 2 * SB_WIDTH + LANES
ORIG_K, ORIG_MQ, ORIG_IF, ORIG_MG = 2560, 3584, 5120, 5128
SB_DEAD_MASS = 88.0
VMEM_LIMIT = 56 * 1024 * 1024


def _cp(sem, vmem=VMEM_LIMIT):
    return pltpu.CompilerParams(dimension_semantics=sem, vmem_limit_bytes=vmem)


def _dot(a, b):
    return jnp.dot(a, b, preferred_element_type=F32)


def _dot_nt(a, b):
    return lax.dot_general(a, b, (((1,), (1,)), ((), ())), preferred_element_type=F32)


def _dot_tn(a, b):
    return lax.dot_general(a, b, (((0,), (0,)), ((), ())), preferred_element_type=F32)


def _sigmoid(x):
    return 0.5 * jnp.tanh(0.5 * x) + 0.5


def _softplus(x):
    return jnp.maximum(x, 0.0) + jnp.log1p(jnp.exp(-jnp.abs(x)))


def _layer_norm(r, g, b):
    mu = jnp.mean(r, axis=-1, keepdims=True)
    d = r - mu
    var = jnp.mean(d * d, axis=-1, keepdims=True)
    return d * lax.rsqrt(var + LN_EPS) * g + b


def _inproj_kernel(x_ref, w_ref, b_ref, o_ref, xb_ref):
    @pl.when(pl.program_id(1) == 0)
    def _():
        xb_ref[...] = x_ref[...].astype(BF16)

    o_ref[...] = (_dot(xb_ref[...], w_ref[...]) + b_ref[...]).astype(o_ref.dtype)


def _inproj(x, w, b, tm, tn):
    T, D = x.shape
    N = w.shape[1]
    return pl.pallas_call(
        _inproj_kernel,
        grid=(T // tm, N // tn),
        in_specs=[pl.BlockSpec((tm, D), lambda i, j: (i, 0)),
                  pl.BlockSpec((D, tn), lambda i, j: (0, j)),
                  pl.BlockSpec((1, tn), lambda i, j: (0, j))],
        out_specs=pl.BlockSpec((tm, tn), lambda i, j: (i, j)),
        out_shape=jax.ShapeDtypeStruct((T, N), BF16),
        scratch_shapes=[pltpu.VMEM((tm, D), BF16)],
        compiler_params=_cp(("parallel", "arbitrary")),
        name="inproj",
    )(x, w, b)


def _kvif_kernel(x_ref, w_ref, b_ref, kall_ref, vall_ref, k_ref, v_ref, if_ref):
    del kall_ref, vall_ref
    tm = x_ref.shape[0]
    r = _dot(x_ref[...].astype(BF16), w_ref[...]) + b_ref[...]
    for h in range(SB_HEADS):
        k_ref[pl.ds(h, tm, stride=SB_HEADS), :] = r[:, h * SB_HEAD_DIM:(h + 1) * SB_HEAD_DIM]
        v_ref[pl.ds(h, tm, stride=SB_HEADS), :] = r[:, SB_WIDTH + h * SB_HEAD_DIM:SB_WIDTH + (h + 1) * SB_HEAD_DIM]
    if_ref[...] = r[:, 2 * SB_WIDTH:]


def _kvif(x, w, b, k_all, v_all, layer, tm):
    T, D = x.shape
    nt = T // tm
    kv_spec = pl.BlockSpec((SB_HEADS * tm, SB_HEAD_DIM), lambda i: (layer * nt + i, 0))
    return pl.pallas_call(
        _kvif_kernel,
        grid=(nt,),
        in_specs=[pl.BlockSpec((tm, D), lambda i: (i, 0)),
                  pl.BlockSpec((D, N_KVIF), lambda i: (0, 0)),
                  pl.BlockSpec((1, N_KVIF), lambda i: (0, 0)),
                  pl.BlockSpec(memory_space=pl.ANY),
                  pl.BlockSpec(memory_space=pl.ANY)],
        out_specs=[kv_spec, kv_spec, pl.BlockSpec((tm, LANES), lambda i: (i, 0))],
        out_shape=[jax.ShapeDtypeStruct(k_all.shape, F32),
                   jax.ShapeDtypeStruct(v_all.shape, F32),
                   jax.ShapeDtypeStruct((T, LANES), F32)],
        input_output_aliases={3: 0, 4: 1},
        compiler_params=_cp(("parallel",)),
        name="inproj_kvif",
    )(x, w, b, k_all, v_all)


def _unwritten(shape):
    return pl.pallas_call(lambda o_ref: None, out_specs=pl.BlockSpec(memory_space=pl.ANY),
                          out_shape=jax.ShapeDtypeStruct(shape, F32), name="kv_state_buffer")()


def _lru_kernel(ax_ref, ag_ref, cbuf_ref, h0_ref, cw_ref, cb_ref, wr_ref, br_ref, wi_ref, bi_ref,
                lam_ref, ya_ref, hl_ref, ext_ref, hc_ref, *, first_frame):
    tl = ax_ref.shape[1]
    li = pl.program_id(1)

    @pl.when(li == 0)
    def _():
        ext_ref[0:8, :] = cbuf_ref[0]
        hc_ref[...] = h0_ref[0]

    ext_ref[8:8 + tl, :] = ax_ref[0].astype(F32)
    ua = cb_ref[...] + ext_ref[5:5 + tl, :] * cw_ref[0:1, :]
    for j in range(1, CONV_W):
        ua = ua + ext_ref[5 + j:5 + j + tl, :] * cw_ref[j:j + 1, :]
    ext_ref[5:8, :] = ext_ref[5 + tl:8 + tl, :]

    uab = ua.astype(BF16)
    ng = D_RNN // LRU_GROUP
    r_pre = jnp.concatenate(
        [_dot(uab[:, g * LRU_GROUP:(g + 1) * LRU_GROUP], wr_ref[g]) for g in range(ng)], axis=1)
    i_pre = jnp.concatenate(
        [_dot(uab[:, g * LRU_GROUP:(g + 1) * LRU_GROUP], wi_ref[g]) for g in range(ng)], axis=1)
    r = _sigmoid(r_pre + br_ref[...])
    ig = _sigmoid(i_pre + bi_ref[...])
    log_a = (-LRU_C) * r * _softplus(-lam_ref[...])
    a = jnp.exp(log_a)
    th = jnp.tanh(log_a)
    mult = jnp.sqrt(-2.0 * th / (1.0 - th))
    row = lax.broadcasted_iota(jnp.int32, (tl, D_RNN), 0)
    if first_frame:
        mult = jnp.where((row == 0) & (li == 0), 1.0, mult)
    bx = mult * (ig * ua)

    a = a.reshape(tl // 8, 8, D_RNN)
    bx = bx.reshape(tl // 8, 8, D_RNN)
    sub = lax.broadcasted_iota(jnp.int32, (tl // 8, 8, D_RNN), 1)
    for d in (1, 2, 4):
        a_sh = jnp.where(sub < d, 1.0, pltpu.roll(a, d, 1))
        b_sh = jnp.where(sub < d, 0.0, pltpu.roll(bx, d, 1))
        bx = a * b_sh + bx
        a = a * a_sh
    carry = hc_ref[...]
    groups = []
    for g in range(tl // 8):
        hg = bx[g] + a[g] * carry
        carry = hg[7:8, :]
        groups.append(hg)
    h = jnp.concatenate(groups, axis=0)
    hc_ref[...] = carry
    hl_ref[0] = carry

    gx = ag_ref[0].astype(F32)
    gelu = 0.5 * gx * (1.0 + jnp.tanh(0.7978845608028654 * (gx + 0.044715 * (gx * gx * gx))))
    ya_ref[0] = (h * gelu).astype(ya_ref.dtype)


def _lru(z3, cbuf8, h0, cw8, cb, wr, br, wi, bi, lam, tl, first_frame):
    B, S, _ = z3.shape
    full = lambda shape: pl.BlockSpec(shape, lambda b, l: (0,) * len(shape))
    return pl.pallas_call(
        functools.partial(_lru_kernel, first_frame=first_frame),
        grid=(B, S // tl),
        in_specs=[pl.BlockSpec((1, tl, D_RNN), lambda b, l: (b, l, OFF_AX // D_RNN)),
                  pl.BlockSpec((1, tl, D_RNN), lambda b, l: (b, l, OFF_AG // D_RNN)),
                  pl.BlockSpec((1, 8, D_RNN), lambda b, l: (b, 0, 0)),
                  pl.BlockSpec((1, 1, D_RNN), lambda b, l: (b, 0, 0)),
                  full((8, D_RNN)), full((1, D_RNN)),
                  full((D_RNN // LRU_GROUP, LRU_GROUP, LRU_GROUP)), full((1, D_RNN)),
                  full((D_RNN // LRU_GROUP, LRU_GROUP, LRU_GROUP)), full((1, D_RNN)),
                  full((1, D_RNN))],
        out_specs=[pl.BlockSpec((1, tl, D_RNN), lambda b, l: (b, l, 0)),
                   pl.BlockSpec((1, 1, D_RNN), lambda b, l: (b, 0, 0))],
        out_shape=[jax.ShapeDtypeStruct((B, S, D_RNN), BF16),
                   jax.ShapeDtypeStruct((B, 1, D_RNN), F32)],
        scratch_shapes=[pltpu.VMEM((tl + 8, D_RNN), F32), pltpu.VMEM((1, D_RNN), F32)],
        compiler_params=_cp(("parallel", "arbitrary")),
        name="conv_rglru",
    )(z3, z3, cbuf8, h0, cw8, cb, wr, br, wi, bi, lam)


def _sb_kernel(q_ref, k_ref, v_ref, kn_ref, vn_ref, o_ref, acc_ref, c_ref, *, tq, q_off, n_new):
    tk = KEY_BLOCK
    rs = min(tq, tk)
    ns = tq // rs
    qi = pl.program_id(2)
    q = (q_ref[0].astype(F32) * (SB_HEAD_DIM ** -0.5)).astype(BF16)
    qpos = q_off + qi * tq + lax.broadcasted_iota(jnp.int32, (tq, tk), 0)
    lane = lax.broadcasted_iota(jnp.int32, (rs, tk), 1)
    jj = lax.broadcasted_iota(jnp.int32, (tk, tk), 0)
    ll = lax.broadcasted_iota(jnp.int32, (tk, tk), 1)
    suffix = jnp.where(jj >= ll, 1.0, 0.0).astype(BF16)
    acc_ref[...] = jnp.zeros_like(acc_ref)
    c_ref[...] = jnp.zeros_like(c_ref)
    diag = [(q_off + qi * tq + (i + 1) * rs - 2) // tk for i in range(ns)]
    never = jnp.int32(2 ** 30)
    group = lax.broadcasted_iota(jnp.int32, (tq, 1), 0) // rs
    diag_row = jnp.zeros((tq, 1), jnp.int32)
    for i in range(ns):
        diag_row = jnp.where(group == i, diag[i], diag_row)

    def cond(carry):
        s, cmin = carry
        return (s <= diag[ns - 1]) & (cmin < SB_DEAD_MASS)

    head = pl.program_id(1)

    def head_rows(ref, start, n):
        return ref[0, pl.ds(SB_HEADS * start + head, n, stride=SB_HEADS), :]

    def step(s, ks, vs, kpos):
        zs = [_dot_nt(q[i * rs:(i + 1) * rs, :], ks[i]) for i in range(ns)]
        z = jnp.concatenate(zs, axis=0) if ns > 1 else zs[0]
        earlier = (jnp.concatenate(kpos, axis=0) if ns > 1 else kpos[0]) < qpos
        u = jnp.where(earlier, _softplus(z), 0.0)
        u_hi = u.astype(BF16)
        u_lo = (u - u_hi.astype(F32)).astype(BF16)
        suf = _dot(u_hi, suffix) + _dot(u_lo, suffix)
        c = c_ref[...]
        w = jnp.where(earlier, jnp.exp(z - suf - c), 0.0).astype(BF16)
        pv = [_dot(w[i * rs:(i + 1) * rs, :], vs[i]) for i in range(ns)]
        acc_ref[...] += jnp.concatenate(pv, axis=0) if ns > 1 else pv[0]
        c_new = c + suf[:, 0:1]
        c_ref[...] = c_new
        return s + 1, jnp.min(jnp.where(diag_row > s, c_new, jnp.inf))

    def body(carry):
        s, _ = carry
        ks, vs, kpos = [], [], []
        for i in range(ns):
            kb = diag[i] - s
            start = pl.multiple_of(jnp.maximum(kb, 0) * tk, tk)
            ks.append(head_rows(k_ref, start, tk).astype(BF16))
            vs.append(head_rows(v_ref, start, tk).astype(BF16))
            kpos.append(jnp.where(kb >= 0, start, never) + lane)
        return step(s, ks, vs, kpos)

    first = (jnp.int32(0), jnp.float32(0.0))
    if n_new:
        pad = jnp.zeros((tk - n_new, SB_HEAD_DIM), BF16)
        first = step(jnp.int32(0), [jnp.concatenate([head_rows(kn_ref, 0, n_new).astype(BF16), pad], axis=0)],
                     [jnp.concatenate([head_rows(vn_ref, 0, n_new).astype(BF16), pad], axis=0)],
                     [diag[0] * tk + lane])
    lax.while_loop(cond, body, first)
    o_ref[0] = acc_ref[...].astype(o_ref.dtype)


def _sb_attention(q_arr, k_arr, v_arr, kn_arr, vn_arr, q_cb, layer, tq, q_off, n_new):
    B, Tq, _ = q_arr.shape
    Tk = k_arr.shape[1] // SB_HEADS
    nrows = SB_HEADS * (n_new if n_new else 8)
    assert n_new == 0 or (Tq == tq == n_new <= KEY_BLOCK and q_off % KEY_BLOCK == 0 and q_off == Tk)
    return pl.pallas_call(
        functools.partial(_sb_kernel, tq=tq, q_off=q_off, n_new=n_new),
        grid=(B, SB_HEADS, Tq // tq),
        in_specs=[pl.BlockSpec((1, tq, SB_HEAD_DIM), lambda b, h, i: (b, i, q_cb + h)),
                  pl.BlockSpec((1, Tk * SB_HEADS, SB_HEAD_DIM), lambda b, h, i: (layer * B + b, 0, 0)),
                  pl.BlockSpec((1, Tk * SB_HEADS, SB_HEAD_DIM), lambda b, h, i: (layer * B + b, 0, 0)),
                  pl.BlockSpec((1, nrows, SB_HEAD_DIM), lambda b, h, i: (layer * B + b, 0, 0)),
                  pl.BlockSpec((1, nrows, SB_HEAD_DIM), lambda b, h, i: (layer * B + b, 0, 0))],
        out_specs=pl.BlockSpec((1, tq, SB_HEAD_DIM), lambda b, h, i: (b, i, h)),
        out_shape=jax.ShapeDtypeStruct((B, Tq, SB_WIDTH), BF16),
        scratch_shapes=[pltpu.VMEM((tq, SB_HEAD_DIM), F32), pltpu.VMEM((tq, 1), F32)],
        compiler_params=_cp(("parallel", "parallel", "arbitrary")),
        name="stick_breaking",
    )(q_arr, k_arr, v_arr, kn_arr, vn_arr)


def _mlstm_kernel(mq_ref, mk_ref, mv_ref, mo_ref, if_ref, c0_ref, n0_ref, m0_ref, g_ref,
                  yc_ref, c_out, n_out, m_out, c_s, n_s, m_s, *, ck):
    tl = mq_ref.shape[1]
    nh = ML_HEADS
    li = pl.program_id(1)

    @pl.when(li == 0)
    def _():
        c_s[...] = c0_ref[0]
        n_s[...] = n0_ref[0]
        m_s[...] = jnp.broadcast_to(m0_ref[0], (nh, 1, LANES))

    t_i = lax.broadcasted_iota(jnp.int32, (nh, ck, ck), 1)
    s_i = lax.broadcasted_iota(jnp.int32, (nh, ck, ck), 2)
    causal = s_i <= t_i
    diag = s_i == t_i
    gain = jnp.stack([g_ref[:, h * ML_V_DIM:(h + 1) * ML_V_DIM] for h in range(nh)])

    def heads(ref, r0, width):
        return jnp.stack([ref[0, pl.ds(r0, ck), h * width:(h + 1) * width] for h in range(nh)]).astype(F32)

    def chunk(ci, carry):
        r0 = pl.multiple_of(ci * ck, ck)
        ifb = if_ref[0, pl.ds(r0, ck), :]
        ig_col = jnp.stack([ifb[:, h:h + 1] for h in range(nh)])
        lf_col = -_softplus(-jnp.stack([ifb[:, nh + h:nh + h + 1] for h in range(nh)]))
        q4 = heads(mq_ref, r0, ML_QK_DIM)
        k4 = heads(mk_ref, r0, ML_QK_DIM) * (ML_QK_DIM ** -0.5)
        v4 = heads(mv_ref, r0, ML_V_DIM)
        o4 = heads(mo_ref, r0, ML_V_DIM)
        ig_row = jnp.sum(jnp.where(diag, ig_col, 0.0), axis=1, keepdims=True)
        lf_row = jnp.sum(jnp.where(diag, lf_col, 0.0), axis=1, keepdims=True)
        b_col = jnp.sum(jnp.where(causal, lf_row, 0.0), axis=2, keepdims=True)
        b_row = jnp.sum(jnp.where(t_i <= s_i, lf_col, 0.0), axis=1, keepdims=True)
        m_prev = m_s[:, :, 0:1]
        dmat = jnp.where(causal, b_col - b_row + ig_row, -jnp.inf)
        inter = b_col + m_prev
        m_t = jnp.maximum(inter, jnp.max(dmat, axis=2, keepdims=True))
        s_inter = jnp.exp(inter - m_t)
        qb = q4.astype(BF16)
        kb = k4.astype(BF16)
        vb = v4.astype(BF16)
        wqk = jnp.exp(dmat - m_t) * jnp.stack([_dot_nt(qb[h], kb[h]) for h in range(nh)])
        c_prev = c_s[...]
        n_prev = n_s[...]
        cb = c_prev.astype(BF16)
        wb = wqk.astype(BF16)
        num = (s_inter * jnp.stack([_dot(qb[h], cb[h]) for h in range(nh)])
               + jnp.stack([_dot(wb[h], vb[h]) for h in range(nh)]))
        den = (s_inter * jnp.sum(q4 * n_prev, axis=2, keepdims=True)
               + jnp.sum(wqk, axis=2, keepdims=True))
        hh = num / jnp.maximum(jnp.abs(den), jnp.exp(-m_t))
        b_end = b_col[:, ck - 1:ck, :]
        g_col = b_end - b_col + ig_col
        m_new = jnp.maximum(b_end + m_prev, jnp.max(g_col, axis=1, keepdims=True))
        s_old = jnp.exp(b_end + m_prev - m_new)
        kw = k4 * jnp.exp(g_col - m_new)
        kwb = kw.astype(BF16)
        c_s[...] = s_old * c_prev + jnp.stack([_dot_tn(kwb[h], vb[h]) for h in range(nh)])
        n_s[...] = s_old * n_prev + jnp.sum(kw, axis=1, keepdims=True)
        m_s[...] = jnp.broadcast_to(m_new, (nh, 1, LANES))
        mu = jnp.mean(hh, axis=2, keepdims=True)
        dlt = hh - mu
        var = jnp.mean(dlt * dlt, axis=2, keepdims=True)
        out = (dlt * lax.rsqrt(var + LN_EPS) * gain * _sigmoid(o4)).astype(yc_ref.dtype)
        for h in range(nh):
            yc_ref[0, pl.ds(r0, ck), h * ML_V_DIM:(h + 1) * ML_V_DIM] = out[h]
        return carry

    lax.fori_loop(0, tl // ck, chunk, 0)
    c_out[0] = c_s[...]
    n_out[0] = n_s[...]
    m_out[0] = m_s[:, :, 0:1]


def _mlstm(z3, if3, c0, n0, m0p, g, tl, ck):
    B, S, _ = z3.shape
    return pl.pallas_call(
        functools.partial(_mlstm_kernel, ck=ck),
        grid=(B, S // tl),
        in_specs=[pl.BlockSpec((1, tl, ML_QK_WIDTH), lambda b, l: (b, l, OFF_MQ // ML_QK_WIDTH)),
                  pl.BlockSpec((1, tl, ML_QK_WIDTH), lambda b, l: (b, l, OFF_MK // ML_QK_WIDTH)),
                  pl.BlockSpec((1, tl, ML_V_WIDTH), lambda b, l: (b, l, OFF_MV // ML_V_WIDTH)),
                  pl.BlockSpec((1, tl, ML_V_WIDTH), lambda b, l: (b, l, OFF_MO // ML_V_WIDTH)),
                  pl.BlockSpec((1, tl, LANES), lambda b, l: (b, l, 0)),
                  pl.BlockSpec((1, ML_HEADS, ML_QK_DIM, ML_V_DIM), lambda b, l: (b, 0, 0, 0)),
                  pl.BlockSpec((1, ML_HEADS, 1, ML_QK_DIM), lambda b, l: (b, 0, 0, 0)),
                  pl.BlockSpec((1, ML_HEADS, 1, 1), lambda b, l: (b, 0, 0, 0)),
                  pl.BlockSpec((1, ML_V_WIDTH), lambda b, l: (0, 0))],
        out_specs=[pl.BlockSpec((1, tl, ML_V_WIDTH), lambda b, l: (b, l, 0)),
                   pl.BlockSpec((1, ML_HEADS, ML_QK_DIM, ML_V_DIM), lambda b, l: (b, 0, 0, 0)),
                   pl.BlockSpec((1, ML_HEADS, 1, ML_QK_DIM), lambda b, l: (b, 0, 0, 0)),
                   pl.BlockSpec((1, ML_HEADS, 1, 1), lambda b, l: (b, 0, 0, 0))],
        out_shape=[jax.ShapeDtypeStruct((B, S, ML_V_WIDTH), BF16),
                   jax.ShapeDtypeStruct((B, ML_HEADS, ML_QK_DIM, ML_V_DIM), F32),
                   jax.ShapeDtypeStruct((B, ML_HEADS, 1, ML_QK_DIM), F32),
                   jax.ShapeDtypeStruct((B, ML_HEADS, 1, 1), F32)],
        scratch_shapes=[pltpu.VMEM((ML_HEADS, ML_QK_DIM, ML_V_DIM), F32),
                        pltpu.VMEM((ML_HEADS, 1, ML_QK_DIM), F32),
                        pltpu.VMEM((ML_HEADS, 1, LANES), F32)],
        compiler_params=_cp(("parallel", "arbitrary")),
        name="mlstm",
    )(z3, z3, z3, z3, if3, c0, n0, m0p, g)


def _merge_kernel(ya_ref, yb_ref, yc_ref, g0_ref, g1_ref, g2_ref, x_ref, wpa_ref, wpb_ref, wpc_ref,
                  wo_ref, lg_ref, lb_ref, rw_ref, rb_ref, tri_ref,
                  o_ref, topi_ref, gate_ref, rank_ref, cnt_ref, run_ref, xprev_ref, *, alpha):
    i = pl.program_id(0)

    @pl.when(i == 0)
    def _():
        run_ref[...] = jnp.zeros_like(run_ref)
        xprev_ref[...] = jnp.zeros_like(xprev_ref)

    _route(xprev_ref[...], jnp.where(i > 0, 1.0, 0.0), rw_ref, rb_ref, tri_ref, topi_ref, gate_ref,
           rank_ref, cnt_ref, run_ref)
    mixed = _sigmoid(g0_ref[...].astype(F32)) * _dot(ya_ref[...], wpa_ref[...])
    mixed = mixed + _sigmoid(g1_ref[...].astype(F32)) * _dot(yb_ref[...], wpb_ref[...])
    mixed = mixed + _sigmoid(g2_ref[...].astype(F32)) * _dot(yc_ref[...], wpc_ref[...])
    r = alpha * x_ref[...] + _dot(mixed.astype(BF16), wo_ref[...])
    x1 = _layer_norm(r, lg_ref[...], lb_ref[...])
    o_ref[...] = x1
    xprev_ref[...] = x1


def _merge(ya, yb, yc, z, x, wpa, wpb, wpc, wo, lg, lb, rw, rb, tri, tm, alpha):
    T = x.shape[0]
    nt = T // tm
    full = lambda shape: pl.BlockSpec(shape, lambda i: (0,) * len(shape))
    tile = lambda width, col=0: pl.BlockSpec((tm, width), lambda i: (jnp.minimum(i, nt - 1), col))
    routed = pl.BlockSpec((tm, LANES), lambda i: (jnp.maximum(i - 1, 0), 0))
    mgb = OFF_MG // D_MODEL
    return pl.pallas_call(
        functools.partial(_merge_kernel, alpha=alpha),
        grid=(nt + 1,),
        in_specs=[tile(D_RNN), tile(SB_WIDTH), tile(ML_V_WIDTH),
                  tile(D_MODEL, mgb), tile(D_MODEL, mgb + 1), tile(D_MODEL, mgb + 2), tile(D_MODEL),
                  full((D_RNN, D_MODEL)), full((SB_WIDTH, D_MODEL)), full((ML_V_WIDTH, D_MODEL)),
                  full((D_MODEL, D_MODEL)), full((1, D_MODEL)), full((1, D_MODEL)),
                  full((D_MODEL, LANES)), full((1, LANES)), full((tm, tm))],
        out_specs=[tile(D_MODEL), routed, routed, routed, full((1, LANES))],
        out_shape=[jax.ShapeDtypeStruct((T, D_MODEL), F32),
                   jax.ShapeDtypeStruct((T, LANES), jnp.int32),
                   jax.ShapeDtypeStruct((T, LANES), F32),
                   jax.ShapeDtypeStruct((T, LANES), jnp.int32),
                   jax.ShapeDtypeStruct((1, LANES), jnp.int32)],
        scratch_shapes=[pltpu.VMEM((1, LANES), F32), pltpu.VMEM((tm, D_MODEL), F32)],
        compiler_params=_cp(("arbitrary",)),
        name="merge_ln1_router",
    )(ya, yb, yc, z, z, z, x, wpa, wpb, wpc, wo, lg, lb, rw, rb, tri)


def _route(x, live, rw_ref, rb_ref, tri_ref, topi_ref, gate_ref, rank_ref, cnt_ref, run_ref):
    tm = x.shape[0]
    l = _dot(x.astype(BF16), rw_ref[...]) + rb_ref[...]
    lane = lax.broadcasted_iota(jnp.int32, (tm, LANES), 1)
    lane_f = lane.astype(F32)
    vals, idxs, hots = [], [], []
    for _ in range(TOP_K):
        m = jnp.max(l, axis=1, keepdims=True)
        idx = jnp.min(jnp.where(l == m, lane_f, float(LANES)), axis=1, keepdims=True)
        hot = lane_f == idx
        vals.append(m)
        idxs.append(idx)
        hots.append(hot)
        l = jnp.where(hot, -jnp.inf, l)
    ex = [jnp.exp(v - vals[0]) for v in vals]
    den = ex[0] + ex[1] + ex[2] + ex[3]
    onehot = jnp.zeros((tm, LANES), F32)
    for hot in hots:
        onehot = onehot + jnp.where(hot, 1.0, 0.0)
    before = _dot(tri_ref[...], onehot.astype(BF16)) + run_ref[...]
    topi = jnp.zeros((tm, LANES), F32)
    gate = jnp.zeros((tm, LANES), F32)
    rank = jnp.zeros((tm, LANES), F32)
    for k in range(TOP_K):
        rk = jnp.sum(jnp.where(hots[k], before, 0.0), axis=1, keepdims=True)
        topi = jnp.where(lane == k, idxs[k], topi)
        gate = jnp.where(lane == k, ex[k] / den, gate)
        rank = jnp.where(lane == k, rk, rank)
    run_new = run_ref[...] + live * jnp.sum(onehot, axis=0, keepdims=True)
    run_ref[...] = run_new
    topi_ref[...] = topi.astype(jnp.int32)
    gate_ref[...] = gate
    rank_ref[...] = rank.astype(jnp.int32)
    cnt_ref[...] = run_new.astype(jnp.int32)


ROW_CHUNKS = D_MODEL // 2 // LANES
U32 = jnp.uint32


def _tile_copy(src_ref, src_row, dst_ref, dst_row, sem):
    src = src_ref.at[pl.ds(pl.multiple_of(src_row * ROW_CHUNKS, ROW_CHUNKS), ROW_CHUNKS)]
    dst = dst_ref.at[pl.ds(pl.multiple_of(dst_row * ROW_CHUNKS, ROW_CHUNKS), ROW_CHUNKS)]
    return pltpu.make_async_copy(src, dst, sem)


def _interleave_store(ref, row0, value):
    n = value.shape[0]
    bits = pltpu.bitcast(value.astype(BF16).astype(F32), U32)
    half = D_MODEL // 2
    packed = lax.shift_right_logical(bits[:, :half], U32(16)) | bits[:, half:]
    for c in range(ROW_CHUNKS):
        ref[pl.ds(row0 * ROW_CHUNKS + c, n, stride=ROW_CHUNKS), :] = packed[:, c * LANES:(c + 1) * LANES]


def _interleave_load(ref, row0, n):
    packed = jnp.concatenate(
        [ref[pl.ds(row0 * ROW_CHUNKS + c, n, stride=ROW_CHUNKS), :] for c in range(ROW_CHUNKS)], axis=1)
    lo = pltpu.bitcast(lax.shift_left(packed, U32(16)), F32)
    hi = pltpu.bitcast(packed & U32(0xFFFF0000), F32)
    return jnp.concatenate([lo, hi], axis=1)


def _dispatch_kernel(padlo_ref, padn_ref, dest_ref, x_ref, xs_hbm, xi, ztile, sem, *, ts):
    @pl.when(pl.program_id(0) == 0)
    def _():
        ztile[...] = jnp.zeros_like(ztile)

        def fill(e, c):
            lax.fori_loop(0, padn_ref[e],
                          lambda r, c2: (_tile_copy(ztile, 0, xs_hbm, padlo_ref[e] + r, sem).start(), c2)[1], 0)
            return c

        def fill_done(e, c):
            lax.fori_loop(0, padn_ref[e],
                          lambda r, c2: (_tile_copy(ztile, 0, xs_hbm, 0, sem).wait(), c2)[1], 0)
            return c

        lax.fori_loop(0, N_EXPERTS, fill, 0)
        lax.fori_loop(0, N_EXPERTS, fill_done, 0)

    _interleave_store(xi, 0, x_ref[...])

    def issue(t, c):
        for k in range(TOP_K):
            _tile_copy(xi, t, xs_hbm, dest_ref[TOP_K * t + k], sem).start(priority=k % 2)
        return c

    lax.fori_loop(0, ts, issue, 0)

    def drain(t, c):
        for k in range(TOP_K):
            _tile_copy(xi, 0, xs_hbm, 0, sem).wait()
        return c

    lax.fori_loop(0, ts, drain, 0)


def _dispatch(pad_lo, pad_n, dest_flat, x, rows, ts):
    T = x.shape[0]
    grid_spec = pltpu.PrefetchScalarGridSpec(
        num_scalar_prefetch=2,
        grid=(T // ts,),
        in_specs=[pl.BlockSpec((TOP_K * ts,), lambda i, lo, n: (i,), memory_space=pltpu.SMEM),
                  pl.BlockSpec((ts, D_MODEL), lambda i, lo, n: (i, 0))],
        out_specs=pl.BlockSpec(memory_space=pl.ANY),
        scratch_shapes=[pltpu.VMEM((ts * ROW_CHUNKS, LANES), U32), pltpu.VMEM((8, LANES), U32),
                        pltpu.SemaphoreType.DMA(())],
    )
    return pl.pallas_call(
        functools.partial(_dispatch_kernel, ts=ts),
        grid_spec=grid_spec,
        out_shape=jax.ShapeDtypeStruct((rows * ROW_CHUNKS, LANES), U32),
        compiler_params=_cp(("arbitrary",)),
        name="moe_dispatch",
    )(pad_lo, pad_n, dest_flat, x)


def _experts_kernel(be_ref, nu_ref, xs_ref, w1_ref, b1_ref, w2_ref, b2_ref, y_ref, w1b_ref, w2b_ref, *, m):
    j = pl.program_id(0)

    @pl.when((j == 0) | (be_ref[j] != be_ref[jnp.maximum(j - 1, 0)]))
    def _():
        w1b_ref[...] = w1_ref[0, 0].astype(BF16)
        w2b_ref[...] = w2_ref[0, 0].astype(BF16)

    @pl.when(j < nu_ref[0])
    def _():
        gu = _dot(_interleave_load(xs_ref, 0, m).astype(BF16), w1b_ref[...]) + b1_ref[0]
        g_ = jnp.minimum(gu[:, :D_FF], SWIGLU_LIMIT)
        up = jnp.clip(gu[:, D_FF:], -SWIGLU_LIMIT, SWIGLU_LIMIT)
        act = (up + 1.0) * g_ * _sigmoid(SWIGLU_ALPHA * g_)
        _interleave_store(y_ref, 0, _dot(act.astype(BF16), w2b_ref[...]) + b2_ref[0])

    @pl.when(pl.program_id(0) >= nu_ref[0])
    def _():
        y_ref[...] = jnp.zeros_like(y_ref)


def _experts(block_e, n_used, xs, w1, b1, w2, b2, m, layer):
    nblk = xs.shape[0] // (m * ROW_CHUNKS)
    grid_spec = pltpu.PrefetchScalarGridSpec(
        num_scalar_prefetch=2,
        grid=(nblk,),
        in_specs=[pl.BlockSpec((m * ROW_CHUNKS, LANES), lambda j, be, nu: (jnp.minimum(j, nu[0] - 1), 0)),
                  pl.BlockSpec((1, 1, D_MODEL, 2 * D_FF), lambda j, be, nu: (layer, be[j], 0, 0)),
                  pl.BlockSpec((1, 1, 2 * D_FF), lambda j, be, nu: (be[j], 0, 0)),
                  pl.BlockSpec((1, 1, D_FF, D_MODEL), lambda j, be, nu: (layer, be[j], 0, 0)),
                  pl.BlockSpec((1, 1, D_MODEL), lambda j, be, nu: (be[j], 0, 0))],
        out_specs=pl.BlockSpec((m * ROW_CHUNKS, LANES), lambda j, be, nu: (j, 0)),
        scratch_shapes=[pltpu.VMEM((D_MODEL, 2 * D_FF), BF16), pltpu.VMEM((D_FF, D_MODEL), BF16)],
    )
    return pl.pallas_call(
        functools.partial(_experts_kernel, m=m),
        grid_spec=grid_spec,
        out_shape=jax.ShapeDtypeStruct(xs.shape, U32),
        compiler_params=_cp(("arbitrary",)),
        name="moe_experts",
    )(block_e, n_used, xs, w1, b1, w2, b2)


def _combine_kernel(dest_a, dest_b, dest_next, x_ref, p_ref, gate_ref, y_hbm, wp_ref, wg_ref, bg_ref,
                    lg_ref, lb_ref, o_ref, ybuf_a, ybuf_b, sem_a, sem_b, *, alpha, tm, halves):
    i = pl.program_id(0)
    bufs = [(ybuf_a, sem_a), (ybuf_b, sem_b)][:halves]
    dests = [dest_a, dest_b][:halves]

    def issue(dest_ref, buf, sem):
        def one(t, c):
            for k in range(TOP_K):
                _tile_copy(y_hbm, dest_ref[TOP_K * t + k], buf, k * tm + t, sem).start(priority=k % 2)
            return c

        lax.fori_loop(0, tm, one, 0)

    def drain(buf, sem):
        def one(t, c):
            for k in range(TOP_K):
                _tile_copy(y_hbm, 0, buf, 0, sem).wait()
            return c

        lax.fori_loop(0, tm, one, 0)

    @pl.when(i == 0)
    def _():
        issue(dests[0], *bufs[0])

    for h in range(halves):
        if h + 1 < halves:
            issue(dests[h + 1], *bufs[h + 1])
        rows = pl.ds(h * tm, tm)
        x = x_ref[rows, :]
        ple = (_sigmoid(_dot(x.astype(BF16), wg_ref[...]) + bg_ref[...])
               * _dot(p_ref[rows, :].astype(BF16), wp_ref[...]))
        r = alpha * x + ple
        drain(*bufs[h])
        for k in range(TOP_K):
            r = r + gate_ref[rows, k:k + 1] * _interleave_load(bufs[h][0], k * tm, tm)
        o_ref[rows, :] = _layer_norm(r, lg_ref[...], lb_ref[...])
        if h == 0:
            @pl.when(i + 1 < pl.num_programs(0))
            def _():
                issue(dest_next, *bufs[0])


def _combine(dest_flat, x, p, gate, y, wp, wg, bg, lg, lb, tm, alpha):
    T = x.shape[0]
    ntiles = T // tm
    halves = 2 if ntiles % 2 == 0 else 1
    tb = halves * tm
    full = lambda shape: pl.BlockSpec(shape, lambda i: (0,) * len(shape))
    dest_spec = lambda fn: pl.BlockSpec((TOP_K * tm,), fn, memory_space=pltpu.SMEM)
    return pl.pallas_call(
        functools.partial(_combine_kernel, alpha=alpha, tm=tm, halves=halves),
        grid=(ntiles // halves,),
        in_specs=[dest_spec(lambda i: (halves * i,)),
                  dest_spec(lambda i: (halves * i + halves - 1,)),
                  dest_spec(lambda i: (jnp.minimum(halves * (i + 1), ntiles - 1),)),
                  pl.BlockSpec((tb, D_MODEL), lambda i: (i, 0)),
                  pl.BlockSpec((tb, D_PLE), lambda i: (i, 0)),
                  pl.BlockSpec((tb, LANES), lambda i: (i, 0)),
                  pl.BlockSpec(memory_space=pl.ANY),
                  full((D_PLE, D_MODEL)), full((D_MODEL, D_MODEL)), full((1, D_MODEL)),
                  full((1, D_MODEL)), full((1, D_MODEL))],
        out_specs=pl.BlockSpec((tb, D_MODEL), lambda i: (i, 0)),
        out_shape=jax.ShapeDtypeStruct((T, D_MODEL), F32),
        scratch_shapes=[pltpu.VMEM((TOP_K * tm * ROW_CHUNKS, LANES), U32),
                        pltpu.VMEM((TOP_K * tm * ROW_CHUNKS, LANES), U32),
                        pltpu.SemaphoreType.DMA(()), pltpu.SemaphoreType.DMA(())],
        compiler_params=_cp(("arbitrary",)),
        name="moe_combine_ln2",
    )(dest_flat, dest_flat, dest_flat, x, p, gate, y, wp, wg, bg, lg, lb)


def _pick(n, pref):
    t = min(n, pref)
    while n % t:
        t //= 2
    return t


def _layer(x, p, lw, conv_buf, lru_h, ml_c, ml_n, ml_m, k_past, v_past, k_all, v_all, prompt, alpha, layer):
    B, L, _ = x.shape
    T = B * L
    xf = x.reshape(T, D_MODEL)

    z = _inproj(xf, lw["w_in"], lw["b_in"], _pick(T, 2048), 1024)
    k_all, v_all, ifg = _kvif(xf, lw["w_kvif"], lw["b_kvif"], k_all, v_all, layer, _pick(T, 1024))
    z3 = z.reshape(B, L, N_MAIN)
    k_rows = k_all.reshape(-1, L * SB_HEADS, SB_HEAD_DIM)
    v_rows = v_all.reshape(-1, L * SB_HEADS, SB_HEAD_DIM)
    conv_new = z3[:, L - (CONV_W - 1):, OFF_AX:OFF_AX + D_RNN].astype(F32)

    cbuf8 = jnp.concatenate([jnp.zeros((B, 8 - (CONV_W - 1), D_RNN), F32), conv_buf.astype(F32)], axis=1)
    ya, lru_new = _lru(z3, cbuf8, lru_h.reshape(B, 1, D_RNN).astype(F32), lw["conv_w8"], lw["conv_b"],
                       lw["wr"], lw["br"], lw["wi"], lw["bi"], lw["lam"], _pick(L, 256), prompt)

    if prompt:
        yb = _sb_attention(z3, k_rows, v_rows, k_rows, v_rows, OFF_Q // SB_HEAD_DIM, layer,
                           _pick(L, 2048), 0, 0)
    else:
        yb = _sb_attention(z3, k_past, v_past, k_rows, v_rows, OFF_Q // SB_HEAD_DIM, layer,
                           L, k_past.shape[1] // SB_HEADS, L)

    ck = _pick(L, 128)
    yc, c_new, n_new, m_new = _mlstm(z3, ifg.reshape(B, L, LANES), ml_c.astype(F32),
                                     ml_n.astype(F32).reshape(B, ML_HEADS, 1, ML_QK_DIM),
                                     ml_m.astype(F32).reshape(B, ML_HEADS, 1, 1),
                                     lw["ml_g"], _pick(L, 512), ck)
    n_new = n_new.reshape(B, ML_HEADS, ML_QK_DIM)
    m_new = m_new.reshape(B, ML_HEADS)

    tr = _pick(T, 512)
    tri = jnp.tril(jnp.ones((tr, tr), BF16), -1)
    x1, topi, gate, rank, cnt = _merge(
        ya.reshape(T, D_RNN), yb.reshape(T, SB_WIDTH), yc.reshape(T, ML_V_WIDTH), z, xf,
        lw["w_pa"], lw["w_pb"], lw["w_pc"], lw["w_out"], lw["ln1_g"], lw["ln1_b"],
        lw["router_w"], lw["router_b"], tri, tr, alpha)

    TK = T * TOP_K
    m_rows = max(16, min(512, TK // N_EXPERTS))
    nb = TK // m_rows + N_EXPERTS
    counts = cnt[0, :N_EXPERTS]
    padded = (counts + (m_rows - 1)) // m_rows * m_rows
    pend = jnp.cumsum(padded)
    off = (pend - padded).astype(jnp.int32)
    blk_start = jnp.arange(nb, dtype=jnp.int32) * m_rows
    block_e = jnp.minimum(jnp.sum((pend[None, :] <= blk_start[:, None]).astype(jnp.int32), axis=1),
                          N_EXPERTS - 1).astype(jnp.int32)
    n_used = (pend[-1:] // m_rows).astype(jnp.int32)
    experts = jnp.arange(N_EXPERTS, dtype=jnp.int32)
    top4 = topi[:, :TOP_K]
    dest = rank[:, :TOP_K] + jnp.sum(jnp.where(top4[:, :, None] == experts, off, 0), axis=-1)
    dest_flat = dest.reshape(TK).astype(jnp.int32)
    xs = _dispatch(off + counts, (padded - counts).astype(jnp.int32), dest_flat, x1, nb * m_rows,
                   _pick(T, 256))
    y = _experts(block_e, n_used, xs, lw["exp_w1"], lw["exp_b1"], lw["exp_w2"], lw["exp_b2"], m_rows, layer)
    x2 = _combine(dest_flat, x1, p.reshape(T, D_PLE), gate, y, lw["ple_w"], lw["ple_gate_w"],
                  lw["ple_gate_b"], lw["ln2_g"], lw["ln2_b"], _pick(T, 256), alpha)

    return (x2.reshape(B, L, D_MODEL), k_all, v_all,
            (conv_new, lru_new.reshape(B, D_RNN), c_new, n_new, m_new))


def _block_diag_groups(w):
    per = LRU_GROUP // LRU_BLOCK
    w4 = w.reshape(D_RNN // LRU_GROUP, per, LRU_BLOCK, LRU_BLOCK)
    eye = jnp.eye(per, dtype=w.dtype)
    return jnp.einsum("gacd,ab->gacbd", w4, eye).reshape(D_RNN // LRU_GROUP, LRU_GROUP, LRU_GROUP)


def _split_in(w):
    main = jnp.concatenate([w[..., :ORIG_K], w[..., ORIG_MQ:ORIG_IF], w[..., ORIG_MG:]], axis=-1)
    pad = jnp.zeros(w.shape[:-1] + (LANES - 2 * ML_HEADS,), w.dtype)
    kvif = jnp.concatenate([w[..., ORIG_K:ORIG_MQ], w[..., ORIG_IF:ORIG_MG], pad], axis=-1)
    return main, kvif


def kernel(x_prompt, x_sample, cache_sb_k, cache_sb_v, state_conv, state_lru, state_mlstm_c, state_mlstm_n, state_mlstm_m, p_prompt, p_sample, w_in, b_in, conv_w, conv_b, lru_wr, lru_br, lru_wi, lru_bi, lru_lambda, ml_norm_g, w_pa, w_pb, w_pc, w_out, ln1_g, ln1_b, router_w, router_b, exp_w1, exp_b1, exp_w2, exp_b2, ple_w, ple_gate_w, ple_gate_b, ln2_g, ln2_b):
    depth = w_in.shape[0]
    alpha = (2 * depth) ** 0.25
    B = x_prompt.shape[0]
    y_prompt, y_sample = x_prompt, x_sample
    S = x_prompt.shape[1]
    Bs, Ls = x_sample.shape[:2]
    P = cache_sb_k.shape[2]
    k_cache = cache_sb_k.astype(F32).reshape(depth * Bs, P * SB_HEADS, SB_HEAD_DIM)
    v_cache = cache_sb_v.astype(F32).reshape(depth * Bs, P * SB_HEADS, SB_HEAD_DIM)
    kp, vp = (_unwritten((depth * B * S * SB_HEADS, SB_HEAD_DIM)) for _ in range(2))
    ks, vs = (_unwritten((depth * Bs * Ls * SB_HEADS, SB_HEAD_DIM)) for _ in range(2))
    st_p, st_s = [], []
    for i in range(depth):
        row = lambda a: a[i].reshape(1, -1).astype(F32)
        w_main, w_kvif = _split_in(w_in[i])
        b_main, b_kvif = _split_in(b_in[i])
        lw = dict(
            w_in=w_main.astype(BF16), b_in=b_main.reshape(1, -1).astype(F32),
            w_kvif=w_kvif.astype(BF16), b_kvif=b_kvif.reshape(1, -1).astype(F32),
            conv_w8=jnp.pad(conv_w[i].astype(F32), ((0, 8 - CONV_W), (0, 0))), conv_b=row(conv_b),
            wr=_block_diag_groups(lru_wr[i]).astype(BF16), br=row(lru_br),
            wi=_block_diag_groups(lru_wi[i]).astype(BF16), bi=row(lru_bi), lam=row(lru_lambda),
            ml_g=row(ml_norm_g),
            w_pa=w_pa[i].astype(BF16), w_pb=w_pb[i].astype(BF16), w_pc=w_pc[i].astype(BF16),
            w_out=w_out[i].astype(BF16), ln1_g=row(ln1_g), ln1_b=row(ln1_b),
            router_w=jnp.pad(router_w[i], ((0, 0), (0, LANES - N_EXPERTS))).astype(BF16),
            router_b=jnp.pad(router_b[i].astype(F32), (0, LANES - N_EXPERTS),
                             constant_values=-1e30).reshape(1, LANES),
            exp_w1=exp_w1.astype(F32), exp_b1=exp_b1[i].reshape(N_EXPERTS, 1, 2 * D_FF).astype(F32),
            exp_w2=exp_w2.astype(F32), exp_b2=exp_b2[i].reshape(N_EXPERTS, 1, D_MODEL).astype(F32),
            ple_w=ple_w[i].astype(BF16), ple_gate_w=ple_gate_w[i].astype(BF16), ple_gate_b=row(ple_gate_b),
            ln2_g=row(ln2_g), ln2_b=row(ln2_b),
        )
        y_prompt, kp, vp, sp = _layer(y_prompt, p_prompt[i], lw,
                                      jnp.zeros((B, CONV_W - 1, D_RNN), F32), jnp.zeros((B, D_RNN), F32),
                                      jnp.zeros((B, ML_HEADS, ML_QK_DIM, ML_V_DIM), F32),
                                      jnp.zeros((B, ML_HEADS, ML_QK_DIM), F32), jnp.zeros((B, ML_HEADS), F32),
                                      None, None, kp, vp, True, alpha, i)
        y_sample, ks, vs, ss = _layer(y_sample, p_sample[i], lw, state_conv[i], state_lru[i], state_mlstm_c[i],
                                      state_mlstm_n[i], state_mlstm_m[i], k_cache, v_cache, ks, vs,
                                      False, alpha, i)
        st_p.append(sp)
        st_s.append(ss)
    conv_p, lru_p, c_p, n_p, m_p = [jnp.stack(s) for s in zip(*st_p)]
    conv_s, lru_s, c_s, n_s, m_s = [jnp.stack(s) for s in zip(*st_s)]
    kv_p = (depth, B, S, SB_HEADS, SB_HEAD_DIM)
    kv_s = (depth, Bs, Ls, SB_HEADS, SB_HEAD_DIM)
    return (y_prompt, y_sample, kp.reshape(kv_p), vp.reshape(kv_p), conv_p, lru_p, c_p, n_p, m_p,
            ks.reshape(kv_s), vs.reshape(kv_s), conv_s, lru_s, c_s, n_s, m_s)
```

```python
import functools

import jax
import jax.numpy as jnp
from jax import lax
from jax.experimental import pallas as pl
from jax.experimental.pallas import tpu as pltpu

F32 = jnp.float32
BF16 = jnp.bfloat16

D_MODEL = 1024
D_RNN = 1024
N_LRU_BLOCKS = 16
LRU_BLOCK = D_RNN // N_LRU_BLOCKS
LRU_GROUP = 256
CONV_W = 4
LRU_C = 8.0
SB_HEADS = 4
SB_HEAD_DIM = 128
SB_WIDTH = SB_HEADS * SB_HEAD_DIM
ML_HEADS = 4
ML_QK_DIM = 64
ML_V_DIM = 128
ML_QK_WIDTH = ML_HEADS * ML_QK_DIM
ML_V_WIDTH = ML_HEADS * ML_V_DIM
N_BRANCH = 3
N_EXPERTS = 32
TOP_K = 4
D_FF = 512
SWIGLU_LIMIT = 7.0
SWIGLU_ALPHA = 1.702
D_PLE = 256
LN_EPS = 1e-5
LANES = 128
KEY_BLOCK = 128

OFF_AX, OFF_AG = 0, 1024
OFF_Q = 2048
OFF_MQ, OFF_MK, OFF_MV, OFF_MO = 2560, 2816, 3072, 3584
OFF_MG = 4096
N_MAIN = 7168
N_KVIF = 2 * SB_WIDTH + LANES
ORIG_K, ORIG_MQ, ORIG_IF, ORIG_MG = 2560, 3584, 5120, 5128
SB_DEAD_MASS = 88.0
VMEM_LIMIT = 56 * 1024 * 1024


def _cp(sem, vmem=VMEM_LIMIT):
    return pltpu.CompilerParams(dimension_semantics=sem, vmem_limit_bytes=vmem)


def _dot(a, b):
    return jnp.dot(a, b, preferred_element_type=F32)


def _dot_nt(a, b):
    return lax.dot_general(a, b, (((1,), (1,)), ((), ())), preferred_element_type=F32)


def _dot_tn(a, b):
    return lax.dot_general(a, b, (((0,), (0,)), ((), ())), preferred_element_type=F32)


def _sigmoid(x):
    return 0.5 * jnp.tanh(0.5 * x) + 0.5


def _softplus(x):
    return jnp.maximum(x, 0.0) + jnp.log1p(jnp.exp(-jnp.abs(x)))


def _layer_norm(r, g, b):
    mu = jnp.mean(r, axis=-1, keepdims=True)
    d = r - mu
    var = jnp.mean(d * d, axis=-1, keepdims=True)
    return d * lax.rsqrt(var + LN_EPS) * g + b


def _inproj_kernel(x_ref, w_ref, b_ref, o_ref, xb_ref):
    @pl.when(pl.program_id(1) == 0)
    def _():
        xb_ref[...] = x_ref[...].astype(BF16)

    o_ref[...] = (_dot(xb_ref[...], w_ref[...]) + b_ref[...]).astype(o_ref.dtype)


def _inproj(x, w, b, tm, tn):
    T, D = x.shape
    N = w.shape[1]
    return pl.pallas_call(
        _inproj_kernel,
        grid=(T // tm, N // tn),
        in_specs=[pl.BlockSpec((tm, D), lambda i, j: (i, 0)),
                  pl.BlockSpec((D, tn), lambda i, j: (0, j)),
                  pl.BlockSpec((1, tn), lambda i, j: (0, j))],
        out_specs=pl.BlockSpec((tm, tn), lambda i, j: (i, j)),
        out_shape=jax.ShapeDtypeStruct((T, N), BF16),
        scratch_shapes=[pltpu.VMEM((tm, D), BF16)],
        compiler_params=_cp(("parallel", "arbitrary")),
        name="inproj",
    )(x, w, b)


def _kvif_kernel(x_ref, w_ref, b_ref, kall_ref, vall_ref, k_ref, v_ref, if_ref):
    del kall_ref, vall_ref
    tm = x_ref.shape[0]
    r = _dot(x_ref[...].astype(BF16), w_ref[...]) + b_ref[...]
    for h in range(SB_HEADS):
        k_ref[pl.ds(h, tm, stride=SB_HEADS), :] = r[:, h * SB_HEAD_DIM:(h + 1) * SB_HEAD_DIM]
        v_ref[pl.ds(h, tm, stride=SB_HEADS), :] = r[:, SB_WIDTH + h * SB_HEAD_DIM:SB_WIDTH + (h + 1) * SB_HEAD_DIM]
    if_ref[...] = r[:, 2 * SB_WIDTH:]


def _kvif(x, w, b, k_all, v_all, layer, tm):
    T, D = x.shape
    nt = T // tm
    kv_spec = pl.BlockSpec((SB_HEADS * tm, SB_HEAD_DIM), lambda i: (layer * nt + i, 0))
    return pl.pallas_call(
        _kvif_kernel,
        grid=(nt,),
        in_specs=[pl.BlockSpec((tm, D), lambda i: (i, 0)),
                  pl.BlockSpec((D, N_KVIF), lambda i: (0, 0)),
                  pl.BlockSpec((1, N_KVIF), lambda i: (0, 0)),
                  pl.BlockSpec(memory_space=pl.ANY),
                  pl.BlockSpec(memory_space=pl.ANY)],
        out_specs=[kv_spec, kv_spec, pl.BlockSpec((tm, LANES), lambda i: (i, 0))],
        out_shape=[jax.ShapeDtypeStruct(k_all.shape, F32),
                   jax.ShapeDtypeStruct(v_all.shape, F32),
                   jax.ShapeDtypeStruct((T, LANES), F32)],
        input_output_aliases={3: 0, 4: 1},
        compiler_params=_cp(("parallel",)),
        name="inproj_kvif",
    )(x, w, b, k_all, v_all)


def _unwritten(shape):
    return pl.pallas_call(lambda o_ref: None, out_specs=pl.BlockSpec(memory_space=pl.ANY),
                          out_shape=jax.ShapeDtypeStruct(shape, F32), name="kv_state_buffer")()


def _lru_kernel(ax_ref, ag_ref, cbuf_ref, h0_ref, cw_ref, cb_ref, wr_ref, br_ref, wi_ref, bi_ref,
                lam_ref, ya_ref, hl_ref, ext_ref, hc_ref, *, first_frame):
    tl = ax_ref.shape[1]
    li = pl.program_id(1)

    @pl.when(li == 0)
    def _():
        ext_ref[0:8, :] = cbuf_ref[0]
        hc_ref[...] = h0_ref[0]

    ext_ref[8:8 + tl, :] = ax_ref[0].astype(F32)
    ua = cb_ref[...] + ext_ref[5:5 + tl, :] * cw_ref[0:1, :]
    for j in range(1, CONV_W):
        ua = ua + ext_ref[5 + j:5 + j + tl, :] * cw_ref[j:j + 1, :]
    ext_ref[5:8, :] = ext_ref[5 + tl:8 + tl, :]

    uab = ua.astype(BF16)
    ng = D_RNN // LRU_GROUP
    r_pre = jnp.concatenate(
        [_dot(uab[:, g * LRU_GROUP:(g + 1) * LRU_GROUP], wr_ref[g]) for g in range(ng)], axis=1)
    i_pre = jnp.concatenate(
        [_dot(uab[:, g * LRU_GROUP:(g + 1) * LRU_GROUP], wi_ref[g]) for g in range(ng)], axis=1)
    r = _sigmoid(r_pre + br_ref[...])
    ig = _sigmoid(i_pre + bi_ref[...])
    log_a = (-LRU_C) * r * _softplus(-lam_ref[...])
    a = jnp.exp(log_a)
    th = jnp.tanh(log_a)
    mult = jnp.sqrt(-2.0 * th / (1.0 - th))
    row = lax.broadcasted_iota(jnp.int32, (tl, D_RNN), 0)
    if first_frame:
        mult = jnp.where((row == 0) & (li == 0), 1.0, mult)
    bx = mult * (ig * ua)

    a = a.reshape(tl // 8, 8, D_RNN)
    bx = bx.reshape(tl // 8, 8, D_RNN)
    sub = lax.broadcasted_iota(jnp.int32, (tl // 8, 8, D_RNN), 1)
    for d in (1, 2, 4):
        a_sh = jnp.where(sub < d, 1.0, pltpu.roll(a, d, 1))
        b_sh = jnp.where(sub < d, 0.0, pltpu.roll(bx, d, 1))
        bx = a * b_sh + bx
        a = a * a_sh
    carry = hc_ref[...]
    groups = []
    for g in range(tl // 8):
        hg = bx[g] + a[g] * carry
        carry = hg[7:8, :]
        groups.append(hg)
    h = jnp.concatenate(groups, axis=0)
    hc_ref[...] = carry
    hl_ref[0] = carry

    gx = ag_ref[0].astype(F32)
    gelu = 0.5 * gx * (1.0 + jnp.tanh(0.7978845608028654 * (gx + 0.044715 * (gx * gx * gx))))
    ya_ref[0] = (h * gelu).astype(ya_ref.dtype)


def _lru(z3, cbuf8, h0, cw8, cb, wr, br, wi, bi, lam, tl, first_frame):
    B, S, _ = z3.shape
    full = lambda shape: pl.BlockSpec(shape, lambda b, l: (0,) * len(shape))
    return pl.pallas_call(
        functools.partial(_lru_kernel, first_frame=first_frame),
        grid=(B, S // tl),
        in_specs=[pl.BlockSpec((1, tl, D_RNN), lambda b, l: (b, l, OFF_AX // D_RNN)),
                  pl.BlockSpec((1, tl, D_RNN), lambda b, l: (b, l, OFF_AG // D_RNN)),
                  pl.BlockSpec((1, 8, D_RNN), lambda b, l: (b, 0, 0)),
                  pl.BlockSpec((1, 1, D_RNN), lambda b, l: (b, 0, 0)),
                  full((8, D_RNN)), full((1, D_RNN)),
                  full((D_RNN // LRU_GROUP, LRU_GROUP, LRU_GROUP)), full((1, D_RNN)),
                  full((D_RNN // LRU_GROUP, LRU_GROUP, LRU_GROUP)), full((1, D_RNN)),
                  full((1, D_RNN))],
        out_specs=[pl.BlockSpec((1, tl, D_RNN), lambda b, l: (b, l, 0)),
                   pl.BlockSpec((1, 1, D_RNN), lambda b, l: (b, 0, 0))],
        out_shape=[jax.ShapeDtypeStruct((B, S, D_RNN), BF16),
                   jax.ShapeDtypeStruct((B, 1, D_RNN), F32)],
        scratch_shapes=[pltpu.VMEM((tl + 8, D_RNN), F32), pltpu.VMEM((1, D_RNN), F32)],
        compiler_params=_cp(("parallel", "arbitrary")),
        name="conv_rglru",
    )(z3, z3, cbuf8, h0, cw8, cb, wr, br, wi, bi, lam)


def _sb_kernel(q_ref, k_ref, v_ref, kn_ref, vn_ref, o_ref, acc_ref, c_ref, *, tq, q_off, n_new):
    tk = KEY_BLOCK
    rs = min(tq, tk)
    ns = tq // rs
    qi = pl.program_id(2)
    q = (q_ref[0].astype(F32) * (SB_HEAD_DIM ** -0.5)).astype(BF16)
    qpos = q_off + qi * tq + lax.broadcasted_iota(jnp.int32, (tq, tk), 0)
    lane = lax.broadcasted_iota(jnp.int32, (rs, tk), 1)
    jj = lax.broadcasted_iota(jnp.int32, (tk, tk), 0)
    ll = lax.broadcasted_iota(jnp.int32, (tk, tk), 1)
    suffix = jnp.where(jj >= ll, 1.0, 0.0).astype(BF16)
    acc_ref[...] = jnp.zeros_like(acc_ref)
    c_ref[...] = jnp.zeros_like(c_ref)
    diag = [(q_off + qi * tq + (i + 1) * rs - 2) // tk for i in range(ns)]
    never = jnp.int32(2 ** 30)
    group = lax.broadcasted_iota(jnp.int32, (tq, 1), 0) // rs
    diag_row = jnp.zeros((tq, 1), jnp.int32)
    for i in range(ns):
        diag_row = jnp.where(group == i, diag[i], diag_row)

    def cond(carry):
        s, cmin = carry
        return (s <= diag[ns - 1]) & (cmin < SB_DEAD_MASS)

    head = pl.program_id(1)

    def head_rows(ref, start, n):
        return ref[0, pl.ds(SB_HEADS * start + head, n, stride=SB_HEADS), :]

    def step(s, ks, vs, kpos):
        zs = [_dot_nt(q[i * rs:(i + 1) * rs, :], ks[i]) for i in range(ns)]
        z = jnp.concatenate(zs, axis=0) if ns > 1 else zs[0]
        earlier = (jnp.concatenate(kpos, axis=0) if ns > 1 else kpos[0]) < qpos
        u = jnp.where(earlier, _softplus(z), 0.0)
        u_hi = u.astype(BF16)
        u_lo = (u - u_hi.astype(F32)).astype(BF16)
        suf = _dot(u_hi, suffix) + _dot(u_lo, suffix)
        c = c_ref[...]
        w = jnp.where(earlier, jnp.exp(z - suf - c), 0.0).astype(BF16)
        pv = [_dot(w[i * rs:(i + 1) * rs, :], vs[i]) for i in range(ns)]
        acc_ref[...] += jnp.concatenate(pv, axis=0) if ns > 1 else pv[0]
        c_new = c + suf[:, 0:1]
        c_ref[...] = c_new
        return s + 1, jnp.min(jnp.where(diag_row > s, c_new, jnp.inf))

    def body(carry):
        s, _ = carry
        ks, vs, kpos = [], [], []
        for i in range(ns):
            kb = diag[i] - s
            start = pl.multiple_of(jnp.maximum(kb, 0) * tk, tk)
            ks.append(head_rows(k_ref, start, tk).astype(BF16))
            vs.append(head_rows(v_ref, start, tk).astype(BF16))
            kpos.append(jnp.where(kb >= 0, start, never) + lane)
        return step(s, ks, vs, kpos)

    first = (jnp.int32(0), jnp.float32(0.0))
    if n_new:
        pad = jnp.zeros((tk - n_new, SB_HEAD_DIM), BF16)
        first = step(jnp.int32(0), [jnp.concatenate([head_rows(kn_ref, 0, n_new).astype(BF16), pad], axis=0)],
                     [jnp.concatenate([head_rows(vn_ref, 0, n_new).astype(BF16), pad], axis=0)],
                     [diag[0] * tk + lane])
    lax.while_loop(cond, body, first)
    o_ref[0] = acc_ref[...].astype(o_ref.dtype)


def _sb_attention(q_arr, k_arr, v_arr, kn_arr, vn_arr, q_cb, layer, tq, q_off, n_new):
    B, Tq, _ = q_arr.shape
    Tk = k_arr.shape[1] // SB_HEADS
    nrows = SB_HEADS * (n_new if n_new else 8)
    assert n_new == 0 or (Tq == tq == n_new <= KEY_BLOCK and q_off % KEY_BLOCK == 0 and q_off == Tk)
    return pl.pallas_call(
        functools.partial(_sb_kernel, tq=tq, q_off=q_off, n_new=n_new),
        grid=(B, SB_HEADS, Tq // tq),
        in_specs=[pl.BlockSpec((1, tq, SB_HEAD_DIM), lambda b, h, i: (b, i, q_cb + h)),
                  pl.BlockSpec((1, Tk * SB_HEADS, SB_HEAD_DIM), lambda b, h, i: (layer * B + b, 0, 0)),
                  pl.BlockSpec((1, Tk * SB_HEADS, SB_HEAD_DIM), lambda b, h, i: (layer * B + b, 0, 0)),
                  pl.BlockSpec((1, nrows, SB_HEAD_DIM), lambda b, h, i: (layer * B + b, 0, 0)),
                  pl.BlockSpec((1, nrows, SB_HEAD_DIM), lambda b, h, i: (layer * B + b, 0, 0))],
        out_specs=pl.BlockSpec((1, tq, SB_HEAD_DIM), lambda b, h, i: (b, i, h)),
        out_shape=jax.ShapeDtypeStruct((B, Tq, SB_WIDTH), BF16),
        scratch_shapes=[pltpu.VMEM((tq, SB_HEAD_DIM), F32), pltpu.VMEM((tq, 1), F32)],
        compiler_params=_cp(("parallel", "parallel", "arbitrary")),
        name="stick_breaking",
    )(q_arr, k_arr, v_arr, kn_arr, vn_arr)


def _mlstm_kernel(mq_ref, mk_ref, mv_ref, mo_ref, if_ref, c0_ref, n0_ref, m0_ref, g_ref,
                  yc_ref, c_out, n_out, m_out, c_s, n_s, m_s, *, ck):
    tl = mq_ref.shape[1]
    nh = ML_HEADS
    li = pl.program_id(1)

    @pl.when(li == 0)
    def _():
        c_s[...] = c0_ref[0]
        n_s[...] = n0_ref[0]
        m_s[...] = jnp.broadcast_to(m0_ref[0], (nh, 1, LANES))

    t_i = lax.broadcasted_iota(jnp.int32, (nh, ck, ck), 1)
    s_i = lax.broadcasted_iota(jnp.int32, (nh, ck, ck), 2)
    causal = s_i <= t_i
    diag = s_i == t_i
    gain = jnp.stack([g_ref[:, h * ML_V_DIM:(h + 1) * ML_V_DIM] for h in range(nh)])

    def heads(ref, r0, width):
        return jnp.stack([ref[0, pl.ds(r0, ck), h * width:(h + 1) * width] for h in range(nh)]).astype(F32)

    def chunk(ci, carry):
        r0 = pl.multiple_of(ci * ck, ck)
        ifb = if_ref[0, pl.ds(r0, ck), :]
        ig_col = jnp.stack([ifb[:, h:h + 1] for h in range(nh)])
        lf_col = -_softplus(-jnp.stack([ifb[:, nh + h:nh + h + 1] for h in range(nh)]))
        q4 = heads(mq_ref, r0, ML_QK_DIM)
        k4 = heads(mk_ref, r0, ML_QK_DIM) * (ML_QK_DIM ** -0.5)
        v4 = heads(mv_ref, r0, ML_V_DIM)
        o4 = heads(mo_ref, r0, ML_V_DIM)
        ig_row = jnp.sum(jnp.where(diag, ig_col, 0.0), axis=1, keepdims=True)
        lf_row = jnp.sum(jnp.where(diag, lf_col, 0.0), axis=1, keepdims=True)
        b_col = jnp.sum(jnp.where(causal, lf_row, 0.0), axis=2, keepdims=True)
        b_row = jnp.sum(jnp.where(t_i <= s_i, lf_col, 0.0), axis=1, keepdims=True)
        m_prev = m_s[:, :, 0:1]
        dmat = jnp.where(causal, b_col - b_row + ig_row, -jnp.inf)
        inter = b_col + m_prev
        m_t = jnp.maximum(inter, jnp.max(dmat, axis=2, keepdims=True))
        s_inter = jnp.exp(inter - m_t)
        qb = q4.astype(BF16)
        kb = k4.astype(BF16)
        vb = v4.astype(BF16)
        wqk = jnp.exp(dmat - m_t) * jnp.stack([_dot_nt(qb[h], kb[h]) for h in range(nh)])
        c_prev = c_s[...]
        n_prev = n_s[...]
        cb = c_prev.astype(BF16)
        wb = wqk.astype(BF16)
        num = (s_inter * jnp.stack([_dot(qb[h], cb[h]) for h in range(nh)])
               + jnp.stack([_dot(wb[h], vb[h]) for h in range(nh)]))
        den = (s_inter * jnp.sum(q4 * n_prev, axis=2, keepdims=True)
               + jnp.sum(wqk, axis=2, keepdims=True))
        hh = num / jnp.maximum(jnp.abs(den), jnp.exp(-m_t))
        b_end = b_col[:, ck - 1:ck, :]
        g_col = b_end - b_col + ig_col
        m_new = jnp.maximum(b_end + m_prev, jnp.max(g_col, axis=1, keepdims=True))
        s_old = jnp.exp(b_end + m_prev - m_new)
        kw = k4 * jnp.exp(g_col - m_new)
        kwb = kw.astype(BF16)
        c_s[...] = s_old * c_prev + jnp.stack([_dot_tn(kwb[h], vb[h]) for h in range(nh)])
        n_s[...] = s_old * n_prev + jnp.sum(kw, axis=1, keepdims=True)
        m_s[...] = jnp.broadcast_to(m_new, (nh, 1, LANES))
        mu = jnp.mean(hh, axis=2, keepdims=True)
        dlt = hh - mu
        var = jnp.mean(dlt * dlt, axis=2, keepdims=True)
        out = (dlt * lax.rsqrt(var + LN_EPS) * gain * _sigmoid(o4)).astype(yc_ref.dtype)
        for h in range(nh):
            yc_ref[0, pl.ds(r0, ck), h * ML_V_DIM:(h + 1) * ML_V_DIM] = out[h]
        return carry

    lax.fori_loop(0, tl // ck, chunk, 0)
    c_out[0] = c_s[...]
    n_out[0] = n_s[...]
    m_out[0] = m_s[:, :, 0:1]


def _mlstm(z3, if3, c0, n0, m0p, g, tl, ck):
    B, S, _ = z3.shape
    return pl.pallas_call(
        functools.partial(_mlstm_kernel, ck=ck),
        grid=(B, S // tl),
        in_specs=[pl.BlockSpec((1, tl, ML_QK_WIDTH), lambda b, l: (b, l, OFF_MQ // ML_QK_WIDTH)),
                  pl.BlockSpec((1, tl, ML_QK_WIDTH), lambda b, l: (b, l, OFF_MK // ML_QK_WIDTH)),
                  pl.BlockSpec((1, tl, ML_V_WIDTH), lambda b, l: (b, l, OFF_MV // ML_V_WIDTH)),
                  pl.BlockSpec((1, tl, ML_V_WIDTH), lambda b, l: (b, l, OFF_MO // ML_V_WIDTH)),
                  pl.BlockSpec((1, tl, LANES), lambda b, l: (b, l, 0)),
                  pl.BlockSpec((1, ML_HEADS, ML_QK_DIM, ML_V_DIM), lambda b, l: (b, 0, 0, 0)),
                  pl.BlockSpec((1, ML_HEADS, 1, ML_QK_DIM), lambda b, l: (b, 0, 0, 0)),
                  pl.BlockSpec((1, ML_HEADS, 1, 1), lambda b, l: (b, 0, 0, 0)),
                  pl.BlockSpec((1, ML_V_WIDTH), lambda b, l: (0, 0))],
        out_specs=[pl.BlockSpec((1, tl, ML_V_WIDTH), lambda b, l: (b, l, 0)),
                   pl.BlockSpec((1, ML_HEADS, ML_QK_DIM, ML_V_DIM), lambda b, l: (b, 0, 0, 0)),
                   pl.BlockSpec((1, ML_HEADS, 1, ML_QK_DIM), lambda b, l: (b, 0, 0, 0)),
                   pl.BlockSpec((1, ML_HEADS, 1, 1), lambda b, l: (b, 0, 0, 0))],
        out_shape=[jax.ShapeDtypeStruct((B, S, ML_V_WIDTH), BF16),
                   jax.ShapeDtypeStruct((B, ML_HEADS, ML_QK_DIM, ML_V_DIM), F32),
                   jax.ShapeDtypeStruct((B, ML_HEADS, 1, ML_QK_DIM), F32),
                   jax.ShapeDtypeStruct((B, ML_HEADS, 1, 1), F32)],
        scratch_shapes=[pltpu.VMEM((ML_HEADS, ML_QK_DIM, ML_V_DIM), F32),
                        pltpu.VMEM((ML_HEADS, 1, ML_QK_DIM), F32),
                        pltpu.VMEM((ML_HEADS, 1, LANES), F32)],
        compiler_params=_cp(("parallel", "arbitrary")),
        name="mlstm",
    )(z3, z3, z3, z3, if3, c0, n0, m0p, g)


def _merge_kernel(ya_ref, yb_ref, yc_ref, g0_ref, g1_ref, g2_ref, x_ref, wpa_ref, wpb_ref, wpc_ref,
                  wo_ref, lg_ref, lb_ref, rw_ref, rb_ref, tri_ref,
                  o_ref, topi_ref, gate_ref, rank_ref, cnt_ref, run_ref, xprev_ref, *, alpha):
    i = pl.program_id(0)

    @pl.when(i == 0)
    def _():
        run_ref[...] = jnp.zeros_like(run_ref)
        xprev_ref[...] = jnp.zeros_like(xprev_ref)

    _route(xprev_ref[...], jnp.where(i > 0, 1.0, 0.0), rw_ref, rb_ref, tri_ref, topi_ref, gate_ref,
           rank_ref, cnt_ref, run_ref)
    mixed = _sigmoid(g0_ref[...].astype(F32)) * _dot(ya_ref[...], wpa_ref[...])
    mixed = mixed + _sigmoid(g1_ref[...].astype(F32)) * _dot(yb_ref[...], wpb_ref[...])
    mixed = mixed + _sigmoid(g2_ref[...].astype(F32)) * _dot(yc_ref[...], wpc_ref[...])
    r = alpha * x_ref[...] + _dot(mixed.astype(BF16), wo_ref[...])
    x1 = _layer_norm(r, lg_ref[...], lb_ref[...])
    o_ref[...] = x1
    xprev_ref[...] = x1


def _merge(ya, yb, yc, z, x, wpa, wpb, wpc, wo, lg, lb, rw, rb, tri, tm, alpha):
    T = x.shape[0]
    nt = T // tm
    full = lambda shape: pl.BlockSpec(shape, lambda i: (0,) * len(shape))
    tile = lambda width, col=0: pl.BlockSpec((tm, width), lambda i: (jnp.minimum(i, nt - 1), col))
    routed = pl.BlockSpec((tm, LANES), lambda i: (jnp.maximum(i - 1, 0), 0))
    mgb = OFF_MG // D_MODEL
    return pl.pallas_call(
        functools.partial(_merge_kernel, alpha=alpha),
        grid=(nt + 1,),
        in_specs=[tile(D_RNN), tile(SB_WIDTH), tile(ML_V_WIDTH),
                  tile(D_MODEL, mgb), tile(D_MODEL, mgb + 1), tile(D_MODEL, mgb + 2), tile(D_MODEL),
                  full((D_RNN, D_MODEL)), full((SB_WIDTH, D_MODEL)), full((ML_V_WIDTH, D_MODEL)),
                  full((D_MODEL, D_MODEL)), full((1, D_MODEL)), full((1, D_MODEL)),
                  full((D_MODEL, LANES)), full((1, LANES)), full((tm, tm))],
        out_specs=[tile(D_MODEL), routed, routed, routed, full((1, LANES))],
        out_shape=[jax.ShapeDtypeStruct((T, D_MODEL), F32),
                   jax.ShapeDtypeStruct((T, LANES), jnp.int32),
                   jax.ShapeDtypeStruct((T, LANES), F32),
                   jax.ShapeDtypeStruct((T, LANES), jnp.int32),
                   jax.ShapeDtypeStruct((1, LANES), jnp.int32)],
        scratch_shapes=[pltpu.VMEM((1, LANES), F32), pltpu.VMEM((tm, D_MODEL), F32)],
        compiler_params=_cp(("arbitrary",)),
        name="merge_ln1_router",
    )(ya, yb, yc, z, z, z, x, wpa, wpb, wpc, wo, lg, lb, rw, rb, tri)


def _route(x, live, rw_ref, rb_ref, tri_ref, topi_ref, gate_ref, rank_ref, cnt_ref, run_ref):
    tm = x.shape[0]
    l = _dot(x.astype(BF16), rw_ref[...]) + rb_ref[...]
    lane = lax.broadcasted_iota(jnp.int32, (tm, LANES), 1)
    lane_f = lane.astype(F32)
    vals, idxs, hots = [], [], []
    for _ in range(TOP_K):
        m = jnp.max(l, axis=1, keepdims=True)
        idx = jnp.min(jnp.where(l == m, lane_f, float(LANES)), axis=1, keepdims=True)
        hot = lane_f == idx
        vals.append(m)
        idxs.append(idx)
        hots.append(hot)
        l = jnp.where(hot, -jnp.inf, l)
    ex = [jnp.exp(v - vals[0]) for v in vals]
    den = ex[0] + ex[1] + ex[2] + ex[3]
    onehot = jnp.zeros((tm, LANES), F32)
    for hot in hots:
        onehot = onehot + jnp.where(hot, 1.0, 0.0)
    before = _dot(tri_ref[...], onehot.astype(BF16)) + run_ref[...]
    topi = jnp.zeros((tm, LANES), F32)
    gate = jnp.zeros((tm, LANES), F32)
    rank = jnp.zeros((tm, LANES), F32)
    for k in range(TOP_K):
        rk = jnp.sum(jnp.where(hots[k], before, 0.0), axis=1, keepdims=True)
        topi = jnp.where(lane == k, idxs[k], topi)
        gate = jnp.where(lane == k, ex[k] / den, gate)
        rank = jnp.where(lane == k, rk, rank)
    run_new = run_ref[...] + live * jnp.sum(onehot, axis=0, keepdims=True)
    run_ref[...] = run_new
    topi_ref[...] = topi.astype(jnp.int32)
    gate_ref[...] = gate
    rank_ref[...] = rank.astype(jnp.int32)
    cnt_ref[...] = run_new.astype(jnp.int32)


ROW_CHUNKS = D_MODEL // 2 // LANES
U32 = jnp.uint32


def _tile_copy(src_ref, src_row, dst_ref, dst_row, sem):
    src = src_ref.at[pl.ds(pl.multiple_of(src_row * ROW_CHUNKS, ROW_CHUNKS), ROW_CHUNKS)]
    dst = dst_ref.at[pl.ds(pl.multiple_of(dst_row * ROW_CHUNKS, ROW_CHUNKS), ROW_CHUNKS)]
    return pltpu.make_async_copy(src, dst, sem)


def _interleave_store(ref, row0, value):
    n = value.shape[0]
    bits = pltpu.bitcast(value.astype(BF16).astype(F32), U32)
    half = D_MODEL // 2
    packed = lax.shift_right_logical(bits[:, :half], U32(16)) | bits[:, half:]
    for c in range(ROW_CHUNKS):
        ref[pl.ds(row0 * ROW_CHUNKS + c, n, stride=ROW_CHUNKS), :] = packed[:, c * LANES:(c + 1) * LANES]


def _interleave_load(ref, row0, n):
    packed = jnp.concatenate(
        [ref[pl.ds(row0 * ROW_CHUNKS + c, n, stride=ROW_CHUNKS), :] for c in range(ROW_CHUNKS)], axis=1)
    lo = pltpu.bitcast(lax.shift_left(packed, U32(16)), F32)
    hi = pltpu.bitcast(packed & U32(0xFFFF0000), F32)
    return jnp.concatenate([lo, hi], axis=1)


def _dispatch_kernel(padlo_ref, padn_ref, dest_ref, x_ref, xs_hbm, xi, ztile, sem, *, ts):
    @pl.when(pl.program_id(0) == 0)
    def _():
        ztile[...] = jnp.zeros_like(ztile)

        def fill(e, c):
            lax.fori_loop(0, padn_ref[e],
                          lambda r, c2: (_tile_copy(ztile, 0, xs_hbm, padlo_ref[e] + r, sem).start(), c2)[1], 0)
            return c

        def fill_done(e, c):
            lax.fori_loop(0, padn_ref[e],
                          lambda r, c2: (_tile_copy(ztile, 0, xs_hbm, 0, sem).wait(), c2)[1], 0)
            return c

        lax.fori_loop(0, N_EXPERTS, fill, 0)
        lax.fori_loop(0, N_EXPERTS, fill_done, 0)

    _interleave_store(xi, 0, x_ref[...])

    def issue(t, c):
        for k in range(TOP_K):
            _tile_copy(xi, t, xs_hbm, dest_ref[TOP_K * t + k], sem).start(priority=k % 2)
        return c

    lax.fori_loop(0, ts, issue, 0)

    def drain(t, c):
        for k in range(TOP_K):
            _tile_copy(xi, 0, xs_hbm, 0, sem).wait()
        return c

    lax.fori_loop(0, ts, drain, 0)


def _dispatch(pad_lo, pad_n, dest_flat, x, rows, ts):
    T = x.shape[0]
    grid_spec = pltpu.PrefetchScalarGridSpec(
        num_scalar_prefetch=2,
        grid=(T // ts,),
        in_specs=[pl.BlockSpec((TOP_K * ts,), lambda i, lo, n: (i,), memory_space=pltpu.SMEM),
                  pl.BlockSpec((ts, D_MODEL), lambda i, lo, n: (i, 0))],
        out_specs=pl.BlockSpec(memory_space=pl.ANY),
        scratch_shapes=[pltpu.VMEM((ts * ROW_CHUNKS, LANES), U32), pltpu.VMEM((8, LANES), U32),
                        pltpu.SemaphoreType.DMA(())],
    )
    return pl.pallas_call(
        functools.partial(_dispatch_kernel, ts=ts),
        grid_spec=grid_spec,
        out_shape=jax.ShapeDtypeStruct((rows * ROW_CHUNKS, LANES), U32),
        compiler_params=_cp(("arbitrary",)),
        name="moe_dispatch",
    )(pad_lo, pad_n, dest_flat, x)


def _experts_kernel(be_ref, nu_ref, xs_ref, w1_ref, b1_ref, w2_ref, b2_ref, y_ref, w1b_ref, w2b_ref, *, m):
    j = pl.program_id(0)

    @pl.when((j == 0) | (be_ref[j] != be_ref[jnp.maximum(j - 1, 0)]))
    def _():
        w1b_ref[...] = w1_ref[0, 0].astype(BF16)
        w2b_ref[...] = w2_ref[0, 0].astype(BF16)

    @pl.when(j < nu_ref[0])
    def _():
        gu = _dot(_interleave_load(xs_ref, 0, m).astype(BF16), w1b_ref[...]) + b1_ref[0]
        g_ = jnp.minimum(gu[:, :D_FF], SWIGLU_LIMIT)
        up = jnp.clip(gu[:, D_FF:], -SWIGLU_LIMIT, SWIGLU_LIMIT)
        act = (up + 1.0) * g_ * _sigmoid(SWIGLU_ALPHA * g_)
        _interleave_store(y_ref, 0, _dot(act.astype(BF16), w2b_ref[...]) + b2_ref[0])

    @pl.when(pl.program_id(0) >= nu_ref[0])
    def _():
        y_ref[...] = jnp.zeros_like(y_ref)


def _experts(block_e, n_used, xs, w1, b1, w2, b2, m, layer):
    nblk = xs.shape[0] // (m * ROW_CHUNKS)
    grid_spec = pltpu.PrefetchScalarGridSpec(
        num_scalar_prefetch=2,
        grid=(nblk,),
        in_specs=[pl.BlockSpec((m * ROW_CHUNKS, LANES), lambda j, be, nu: (jnp.minimum(j, nu[0] - 1), 0)),
                  pl.BlockSpec((1, 1, D_MODEL, 2 * D_FF), lambda j, be, nu: (layer, be[j], 0, 0)),
                  pl.BlockSpec((1, 1, 2 * D_FF), lambda j, be, nu: (be[j], 0, 0)),
                  pl.BlockSpec((1, 1, D_FF, D_MODEL), lambda j, be, nu: (layer, be[j], 0, 0)),
                  pl.BlockSpec((1, 1, D_MODEL), lambda j, be, nu: (be[j], 0, 0))],
        out_specs=pl.BlockSpec((m * ROW_CHUNKS, LANES), lambda j, be, nu: (j, 0)),
        scratch_shapes=[pltpu.VMEM((D_MODEL, 2 * D_FF), BF16), pltpu.VMEM((D_FF, D_MODEL), BF16)],
    )
    return pl.pallas_call(
        functools.partial(_experts_kernel, m=m),
        grid_spec=grid_spec,
        out_shape=jax.ShapeDtypeStruct(xs.shape, U32),
        compiler_params=_cp(("arbitrary",)),
        name="moe_experts",
    )(block_e, n_used, xs, w1, b1, w2, b2)


def _combine_kernel(dest_a, dest_b, dest_next, x_ref, p_ref, gate_ref, y_hbm, wp_ref, wg_ref, bg_ref,
                    lg_ref, lb_ref, o_ref, ybuf_a, ybuf_b, sem_a, sem_b, *, alpha, tm, halves):
    i = pl.program_id(0)
    bufs = [(ybuf_a, sem_a), (ybuf_b, sem_b)][:halves]
    dests = [dest_a, dest_b][:halves]

    def issue(dest_ref, buf, sem):
        def one(t, c):
            for k in range(TOP_K):
                _tile_copy(y_hbm, dest_ref[TOP_K * t + k], buf, k * tm + t, sem).start(priority=k % 2)
            return c

        lax.fori_loop(0, tm, one, 0)

    def drain(buf, sem):
        def one(t, c):
            for k in range(TOP_K):
                _tile_copy(y_hbm, 0, buf, 0, sem).wait()
            return c

        lax.fori_loop(0, tm, one, 0)

    @pl.when(i == 0)
    def _():
        issue(dests[0], *bufs[0])

    for h in range(halves):
        if h + 1 < halves:
            issue(dests[h + 1], *bufs[h + 1])
        rows = pl.ds(h * tm, tm)
        x = x_ref[rows, :]
        ple = (_sigmoid(_dot(x.astype(BF16), wg_ref[...]) + bg_ref[...])
               * _dot(p_ref[rows, :].astype(BF16), wp_ref[...]))
        r = alpha * x + ple
        drain(*bufs[h])
        for k in range(TOP_K):
            r = r + gate_ref[rows, k:k + 1] * _interleave_load(bufs[h][0], k * tm, tm)
        o_ref[rows, :] = _layer_norm(r, lg_ref[...], lb_ref[...])
        if h == 0:
            @pl.when(i + 1 < pl.num_programs(0))
            def _():
                issue(dest_next, *bufs[0])


def _combine(dest_flat, x, p, gate, y, wp, wg, bg, lg, lb, tm, alpha):
    T = x.shape[0]
    ntiles = T // tm
    halves = 2 if ntiles % 2 == 0 else 1
    tb = halves * tm
    full = lambda shape: pl.BlockSpec(shape, lambda i: (0,) * len(shape))
    dest_spec = lambda fn: pl.BlockSpec((TOP_K * tm,), fn, memory_space=pltpu.SMEM)
    return pl.pallas_call(
        functools.partial(_combine_kernel, alpha=alpha, tm=tm, halves=halves),
        grid=(ntiles // halves,),
        in_specs=[dest_spec(lambda i: (halves * i,)),
                  dest_spec(lambda i: (halves * i + halves - 1,)),
                  dest_spec(lambda i: (jnp.minimum(halves * (i + 1), ntiles - 1),)),
                  pl.BlockSpec((tb, D_MODEL), lambda i: (i, 0)),
                  pl.BlockSpec((tb, D_PLE), lambda i: (i, 0)),
                  pl.BlockSpec((tb, LANES), lambda i: (i, 0)),
                  pl.BlockSpec(memory_space=pl.ANY),
                  full((D_PLE, D_MODEL)), full((D_MODEL, D_MODEL)), full((1, D_MODEL)),
                  full((1, D_MODEL)), full((1, D_MODEL))],
        out_specs=pl.BlockSpec((tb, D_MODEL), lambda i: (i, 0)),
        out_shape=jax.ShapeDtypeStruct((T, D_MODEL), F32),
        scratch_shapes=[pltpu.VMEM((TOP_K * tm * ROW_CHUNKS, LANES), U32),
                        pltpu.VMEM((TOP_K * tm * ROW_CHUNKS, LANES), U32),
                        pltpu.SemaphoreType.DMA(()), pltpu.SemaphoreType.DMA(())],
        compiler_params=_cp(("arbitrary",)),
        name="moe_combine_ln2",
    )(dest_flat, dest_flat, dest_flat, x, p, gate, y, wp, wg, bg, lg, lb)


def _pick(n, pref):
    t = min(n, pref)
    while n % t:
        t //= 2
    return t


def _layer(x, p, lw, conv_buf, lru_h, ml_c, ml_n, ml_m, k_past, v_past, k_all, v_all, prompt, alpha, layer):
    B, L, _ = x.shape
    T = B * L
    xf = x.reshape(T, D_MODEL)

    z = _inproj(xf, lw["w_in"], lw["b_in"], _pick(T, 2048), 1024)
    k_all, v_all, ifg = _kvif(xf, lw["w_kvif"], lw["b_kvif"], k_all, v_all, layer, _pick(T, 1024))
    z3 = z.reshape(B, L, N_MAIN)
    k_rows = k_all.reshape(-1, L * SB_HEADS, SB_HEAD_DIM)
    v_rows = v_all.reshape(-1, L * SB_HEADS, SB_HEAD_DIM)
    conv_new = z3[:, L - (CONV_W - 1):, OFF_AX:OFF_AX + D_RNN].astype(F32)

    cbuf8 = jnp.concatenate([jnp.zeros((B, 8 - (CONV_W - 1), D_RNN), F32), conv_buf.astype(F32)], axis=1)
    ya, lru_new = _lru(z3, cbuf8, lru_h.reshape(B, 1, D_RNN).astype(F32), lw["conv_w8"], lw["conv_b"],
                       lw["wr"], lw["br"], lw["wi"], lw["bi"], lw["lam"], _pick(L, 512), prompt)

    if prompt:
        yb = _sb_attention(z3, k_rows, v_rows, k_rows, v_rows, OFF_Q // SB_HEAD_DIM, layer,
                           _pick(L, 2048), 0, 0)
    else:
        yb = _sb_attention(z3, k_past, v_past, k_rows, v_rows, OFF_Q // SB_HEAD_DIM, layer,
                           L, k_past.shape[1] // SB_HEADS, L)

    ck = _pick(L, 128)
    yc, c_new, n_new, m_new = _mlstm(z3, ifg.reshape(B, L, LANES), ml_c.astype(F32),
                                     ml_n.astype(F32).reshape(B, ML_HEADS, 1, ML_QK_DIM),
                                     ml_m.astype(F32).reshape(B, ML_HEADS, 1, 1),
                                     lw["ml_g"], _pick(L, 512), ck)
    n_new = n_new.reshape(B, ML_HEADS, ML_QK_DIM)
    m_new = m_new.reshape(B, ML_HEADS)

    tr = _pick(T, 512)
    tri = jnp.tril(jnp.ones((tr, tr), BF16), -1)
    x1, topi, gate, rank, cnt = _merge(
        ya.reshape(T, D_RNN), yb.reshape(T, SB_WIDTH), yc.reshape(T, ML_V_WIDTH), z, xf,
        lw["w_pa"], lw["w_pb"], lw["w_pc"], lw["w_out"], lw["ln1_g"], lw["ln1_b"],
        lw["router_w"], lw["router_b"], tri, tr, alpha)

    TK = T * TOP_K
    m_rows = max(16, min(1024, TK // N_EXPERTS))
    nb = TK // m_rows + N_EXPERTS
    counts = cnt[0, :N_EXPERTS]
    padded = (counts + (m_rows - 1)) // m_rows * m_rows
    pend = jnp.cumsum(padded)
    off = (pend - padded).astype(jnp.int32)
    blk_start = jnp.arange(nb, dtype=jnp.int32) * m_rows
    block_e = jnp.minimum(jnp.sum((pend[None, :] <= blk_start[:, None]).astype(jnp.int32), axis=1),
                          N_EXPERTS - 1).astype(jnp.int32)
    n_used = (pend[-1:] // m_rows).astype(jnp.int32)
    experts = jnp.arange(N_EXPERTS, dtype=jnp.int32)
    top4 = topi[:, :TOP_K]
    dest = rank[:, :TOP_K] + jnp.sum(jnp.where(top4[:, :, None] == experts, off, 0), axis=-1)
    dest_flat = dest.reshape(TK).astype(jnp.int32)
    xs = _dispatch(off + counts, (padded - counts).astype(jnp.int32), dest_flat, x1, nb * m_rows,
                   _pick(T, 256))
    y = _experts(block_e, n_used, xs, lw["exp_w1"], lw["exp_b1"], lw["exp_w2"], lw["exp_b2"], m_rows, layer)
    x2 = _combine(dest_flat, x1, p.reshape(T, D_PLE), gate, y, lw["ple_w"], lw["ple_gate_w"],
                  lw["ple_gate_b"], lw["ln2_g"], lw["ln2_b"], _pick(T, 256), alpha)

    return (x2.reshape(B, L, D_MODEL), k_all, v_all,
            (conv_new, lru_new.reshape(B, D_RNN), c_new, n_new, m_new))


def _block_diag_groups(w):
    per = LRU_GROUP // LRU_BLOCK
    w4 = w.reshape(D_RNN // LRU_GROUP, per, LRU_BLOCK, LRU_BLOCK)
    eye = jnp.eye(per, dtype=w.dtype)
    return jnp.einsum("gacd,ab->gacbd", w4, eye).reshape(D_RNN // LRU_GROUP, LRU_GROUP, LRU_GROUP)


def _split_in(w):
    main = jnp.concatenate([w[..., :ORIG_K], w[..., ORIG_MQ:ORIG_IF], w[..., ORIG_MG:]], axis=-1)
    pad = jnp.zeros(w.shape[:-1] + (LANES - 2 * ML_HEADS,), w.dtype)
    kvif = jnp.concatenate([w[..., ORIG_K:ORIG_MQ], w[..., ORIG_IF:ORIG_MG], pad], axis=-1)
    return main, kvif


def kernel(x_prompt, x_sample, cache_sb_k, cache_sb_v, state_conv, state_lru, state_mlstm_c, state_mlstm_n, state_mlstm_m, p_prompt, p_sample, w_in, b_in, conv_w, conv_b, lru_wr, lru_br, lru_wi, lru_bi, lru_lambda, ml_norm_g, w_pa, w_pb, w_pc, w_out, ln1_g, ln1_b, router_w, router_b, exp_w1, exp_b1, exp_w2, exp_b2, ple_w, ple_gate_w, ple_gate_b, ln2_g, ln2_b):
    depth = w_in.shape[0]
    alpha = (2 * depth) ** 0.25
    B = x_prompt.shape[0]
    y_prompt, y_sample = x_prompt, x_sample
    S = x_prompt.shape[1]
    Bs, Ls = x_sample.shape[:2]
    P = cache_sb_k.shape[2]
    k_cache = cache_sb_k.astype(F32).reshape(depth * Bs, P * SB_HEADS, SB_HEAD_DIM)
    v_cache = cache_sb_v.astype(F32).reshape(depth * Bs, P * SB_HEADS, SB_HEAD_DIM)
    kp, vp = (_unwritten((depth * B * S * SB_HEADS, SB_HEAD_DIM)) for _ in range(2))
    ks, vs = (_unwritten((depth * Bs * Ls * SB_HEADS, SB_HEAD_DIM)) for _ in range(2))
    st_p, st_s = [], []
    for i in range(depth):
        row = lambda a: a[i].reshape(1, -1).astype(F32)
        w_main, w_kvif = _split_in(w_in[i])
        b_main, b_kvif = _split_in(b_in[i])
        lw = dict(
            w_in=w_main.astype(BF16), b_in=b_main.reshape(1, -1).astype(F32),
            w_kvif=w_kvif.astype(BF16), b_kvif=b_kvif.reshape(1, -1).astype(F32),
            conv_w8=jnp.pad(conv_w[i].astype(F32), ((0, 8 - CONV_W), (0, 0))), conv_b=row(conv_b),
            wr=_block_diag_groups(lru_wr[i]).astype(BF16), br=row(lru_br),
            wi=_block_diag_groups(lru_wi[i]).astype(BF16), bi=row(lru_bi), lam=row(lru_lambda),
            ml_g=row(ml_norm_g),
            w_pa=w_pa[i].astype(BF16), w_pb=w_pb[i].astype(BF16), w_pc=w_pc[i].astype(BF16),
            w_out=w_out[i].astype(BF16), ln1_g=row(ln1_g), ln1_b=row(ln1_b),
            router_w=jnp.pad(router_w[i], ((0, 0), (0, LANES - N_EXPERTS))).astype(BF16),
            router_b=jnp.pad(router_b[i].astype(F32), (0, LANES - N_EXPERTS),
                             constant_values=-1e30).reshape(1, LANES),
            exp_w1=exp_w1.astype(F32), exp_b1=exp_b1[i].reshape(N_EXPERTS, 1, 2 * D_FF).astype(F32),
            exp_w2=exp_w2.astype(F32), exp_b2=exp_b2[i].reshape(N_EXPERTS, 1, D_MODEL).astype(F32),
            ple_w=ple_w[i].astype(BF16), ple_gate_w=ple_gate_w[i].astype(BF16), ple_gate_b=row(ple_gate_b),
            ln2_g=row(ln2_g), ln2_b=row(ln2_b),
        )
        y_prompt, kp, vp, sp = _layer(y_prompt, p_prompt[i], lw,
                                      jnp.zeros((B, CONV_W - 1, D_RNN), F32), jnp.zeros((B, D_RNN), F32),
                                      jnp.zeros((B, ML_HEADS, ML_QK_DIM, ML_V_DIM), F32),
                                      jnp.zeros((B, ML_HEADS, ML_QK_DIM), F32), jnp.zeros((B, ML_HEADS), F32),
                                      None, None, kp, vp, True, alpha, i)
        y_sample, ks, vs, ss = _layer(y_sample, p_sample[i], lw, state_conv[i], state_lru[i], state_mlstm_c[i],
                                      state_mlstm_n[i], state_mlstm_m[i], k_cache, v_cache, ks, vs,
                                      False, alpha, i)
        st_p.append(sp)
        st_s.append(ss)
    conv_p, lru_p, c_p, n_p, m_p = [jnp.stack(s) for s in zip(*st_p)]
    conv_s, lru_s, c_s, n_s, m_s = [jnp.stack(s) for s in zip(*st_s)]
    kv_p = (depth, B, S, SB_HEADS, SB_HEAD_DIM)
    kv_s = (depth, Bs, Ls, SB_HEADS, SB_HEAD_DIM)
    return (y_prompt, y_sample, kp.reshape(kv_p), vp.reshape(kv_p), conv_p, lru_p, c_p, n_p, m_p,
            ks.reshape(kv_s), vs.reshape(kv_s), conv_s, lru_s, c_s, n_s, m_s)
```

```python
import functools

import jax
import jax.numpy as jnp
from jax import lax
from jax.experimental import pallas as pl
from jax.experimental.pallas import tpu as pltpu

F32 = jnp.float32
BF16 = jnp.bfloat16

D_MODEL = 1024
D_RNN = 1024
N_LRU_BLOCKS = 16
LRU_BLOCK = D_RNN // N_LRU_BLOCKS
LRU_GROUP = 256
CONV_W = 4
LRU_C = 8.0
SB_HEADS = 4
SB_HEAD_DIM = 128
SB_WIDTH = SB_HEADS * SB_HEAD_DIM
ML_HEADS = 4
ML_QK_DIM = 64
ML_V_DIM = 128
ML_QK_WIDTH = ML_HEADS * ML_QK_DIM
ML_V_WIDTH = ML_HEADS * ML_V_DIM
N_BRANCH = 3
N_EXPERTS = 32
TOP_K = 4
D_FF = 512
SWIGLU_LIMIT = 7.0
SWIGLU_ALPHA = 1.702
D_PLE = 256
LN_EPS = 1e-5
LANES = 128
KEY_BLOCK = 128

OFF_AX, OFF_AG = 0, 1024
OFF_Q = 2048
OFF_MQ, OFF_MK, OFF_MV, OFF_MO = 2560, 2816, 3072, 3584
OFF_MG = 4096
N_MAIN = 7168
N_KVIF = 2 * SB_WIDTH + LANES
ORIG_K, ORIG_MQ, ORIG_IF, ORIG_MG = 2560, 3584, 5120, 5128
SB_DEAD_MASS = 88.0
VMEM_LIMIT = 56 * 1024 * 1024


def _cp(sem, vmem=VMEM_LIMIT):
    return pltpu.CompilerParams(dimension_semantics=sem, vmem_limit_bytes=vmem)


def _dot(a, b):
    return jnp.dot(a, b, preferred_element_type=F32)


def _dot_nt(a, b):
    return lax.dot_general(a, b, (((1,), (1,)), ((), ())), preferred_element_type=F32)


def _dot_tn(a, b):
    return lax.dot_general(a, b, (((0,), (0,)), ((), ())), preferred_element_type=F32)


def _sigmoid(x):
    return 0.5 * jnp.tanh(0.5 * x) + 0.5


def _softplus(x):
    return jnp.maximum(x, 0.0) + jnp.log1p(jnp.exp(-jnp.abs(x)))


def _layer_norm(r, g, b):
    mu = jnp.mean(r, axis=-1, keepdims=True)
    d = r - mu
    var = jnp.mean(d * d, axis=-1, keepdims=True)
    return d * lax.rsqrt(var + LN_EPS) * g + b


def _inproj_kernel(x_ref, w_ref, b_ref, o_ref, xb_ref):
    @pl.when(pl.program_id(1) == 0)
    def _():
        xb_ref[...] = x_ref[...].astype(BF16)

    o_ref[...] = (_dot(xb_ref[...], w_ref[...]) + b_ref[...]).astype(o_ref.dtype)


def _inproj(x, w, b, tm, tn):
    T, D = x.shape
    N = w.shape[1]
    return pl.pallas_call(
        _inproj_kernel,
        grid=(T // tm, N // tn),
        in_specs=[pl.BlockSpec((tm, D), lambda i, j: (i, 0)),
                  pl.BlockSpec((D, tn), lambda i, j: (0, j)),
                  pl.BlockSpec((1, tn), lambda i, j: (0, j))],
        out_specs=pl.BlockSpec((tm, tn), lambda i, j: (i, j)),
        out_shape=jax.ShapeDtypeStruct((T, N), BF16),
        scratch_shapes=[pltpu.VMEM((tm, D), BF16)],
        compiler_params=_cp(("parallel", "arbitrary")),
        name="inproj",
    )(x, w, b)


def _kvif_kernel(x_ref, w_ref, b_ref, kall_ref, vall_ref, k_ref, v_ref, if_ref):
    del kall_ref, vall_ref
    tm = x_ref.shape[0]
    r = _dot(x_ref[...].astype(BF16), w_ref[...]) + b_ref[...]
    for h in range(SB_HEADS):
        k_ref[pl.ds(h, tm, stride=SB_HEADS), :] = r[:, h * SB_HEAD_DIM:(h + 1) * SB_HEAD_DIM]
        v_ref[pl.ds(h, tm, stride=SB_HEADS), :] = r[:, SB_WIDTH + h * SB_HEAD_DIM:SB_WIDTH + (h + 1) * SB_HEAD_DIM]
    if_ref[...] = r[:, 2 * SB_WIDTH:]


def _kvif(x, w, b, k_all, v_all, layer, tm):
    T, D = x.shape
    nt = T // tm
    kv_spec = pl.BlockSpec((SB_HEADS * tm, SB_HEAD_DIM), lambda i: (layer * nt + i, 0))
    return pl.pallas_call(
        _kvif_kernel,
        grid=(nt,),
        in_specs=[pl.BlockSpec((tm, D), lambda i: (i, 0)),
                  pl.BlockSpec((D, N_KVIF), lambda i: (0, 0)),
                  pl.BlockSpec((1, N_KVIF), lambda i: (0, 0)),
                  pl.BlockSpec(memory_space=pl.ANY),
                  pl.BlockSpec(memory_space=pl.ANY)],
        out_specs=[kv_spec, kv_spec, pl.BlockSpec((tm, LANES), lambda i: (i, 0))],
        out_shape=[jax.ShapeDtypeStruct(k_all.shape, F32),
                   jax.ShapeDtypeStruct(v_all.shape, F32),
                   jax.ShapeDtypeStruct((T, LANES), F32)],
        input_output_aliases={3: 0, 4: 1},
        compiler_params=_cp(("parallel",)),
        name="inproj_kvif",
    )(x, w, b, k_all, v_all)


def _unwritten(shape):
    return pl.pallas_call(lambda o_ref: None, out_specs=pl.BlockSpec(memory_space=pl.ANY),
                          out_shape=jax.ShapeDtypeStruct(shape, F32), name="kv_state_buffer")()


def _lru_kernel(ax_ref, ag_ref, cbuf_ref, h0_ref, cw_ref, cb_ref, wr_ref, br_ref, wi_ref, bi_ref,
                lam_ref, ya_ref, hl_ref, ext_ref, hc_ref, *, first_frame):
    tl = ax_ref.shape[1]
    li = pl.program_id(1)

    @pl.when(li == 0)
    def _():
        ext_ref[0:8, :] = cbuf_ref[0]
        hc_ref[...] = h0_ref[0]

    ext_ref[8:8 + tl, :] = ax_ref[0].astype(F32)
    ua = cb_ref[...] + ext_ref[5:5 + tl, :] * cw_ref[0:1, :]
    for j in range(1, CONV_W):
        ua = ua + ext_ref[5 + j:5 + j + tl, :] * cw_ref[j:j + 1, :]
    ext_ref[5:8, :] = ext_ref[5 + tl:8 + tl, :]

    uab = ua.astype(BF16)
    ng = D_RNN // LRU_GROUP
    r_pre = jnp.concatenate(
        [_dot(uab[:, g * LRU_GROUP:(g + 1) * LRU_GROUP], wr_ref[g]) for g in range(ng)], axis=1)
    i_pre = jnp.concatenate(
        [_dot(uab[:, g * LRU_GROUP:(g + 1) * LRU_GROUP], wi_ref[g]) for g in range(ng)], axis=1)
    r = _sigmoid(r_pre + br_ref[...])
    ig = _sigmoid(i_pre + bi_ref[...])
    log_a = (-LRU_C) * r * _softplus(-lam_ref[...])
    a = jnp.exp(log_a)
    th = jnp.tanh(log_a)
    mult = jnp.sqrt(-2.0 * th / (1.0 - th))
    row = lax.broadcasted_iota(jnp.int32, (tl, D_RNN), 0)
    if first_frame:
        mult = jnp.where((row == 0) & (li == 0), 1.0, mult)
    bx = mult * (ig * ua)

    a = a.reshape(tl // 8, 8, D_RNN)
    bx = bx.reshape(tl // 8, 8, D_RNN)
    sub = lax.broadcasted_iota(jnp.int32, (tl // 8, 8, D_RNN), 1)
    for d in (1, 2, 4):
        a_sh = jnp.where(sub < d, 1.0, pltpu.roll(a, d, 1))
        b_sh = jnp.where(sub < d, 0.0, pltpu.roll(bx, d, 1))
        bx = a * b_sh + bx
        a = a * a_sh
    carry = hc_ref[...]
    groups = []
    for g in range(tl // 8):
        hg = bx[g] + a[g] * carry
        carry = hg[7:8, :]
        groups.append(hg)
    h = jnp.concatenate(groups, axis=0)
    hc_ref[...] = carry
    hl_ref[0] = carry

    gx = ag_ref[0].astype(F32)
    gelu = 0.5 * gx * (1.0 + jnp.tanh(0.7978845608028654 * (gx + 0.044715 * (gx * gx * gx))))
    ya_ref[0] = (h * gelu).astype(ya_ref.dtype)


def _lru(z3, cbuf8, h0, cw8, cb, wr, br, wi, bi, lam, tl, first_frame):
    B, S, _ = z3.shape
    full = lambda shape: pl.BlockSpec(shape, lambda b, l: (0,) * len(shape))
    return pl.pallas_call(
        functools.partial(_lru_kernel, first_frame=first_frame),
        grid=(B, S // tl),
        in_specs=[pl.BlockSpec((1, tl, D_RNN), lambda b, l: (b, l, OFF_AX // D_RNN)),
                  pl.BlockSpec((1, tl, D_RNN), lambda b, l: (b, l, OFF_AG // D_RNN)),
                  pl.BlockSpec((1, 8, D_RNN), lambda b, l: (b, 0, 0)),
                  pl.BlockSpec((1, 1, D_RNN), lambda b, l: (b, 0, 0)),
                  full((8, D_RNN)), full((1, D_RNN)),
                  full((D_RNN // LRU_GROUP, LRU_GROUP, LRU_GROUP)), full((1, D_RNN)),
                  full((D_RNN // LRU_GROUP, LRU_GROUP, LRU_GROUP)), full((1, D_RNN)),
                  full((1, D_RNN))],
        out_specs=[pl.BlockSpec((1, tl, D_RNN), lambda b, l: (b, l, 0)),
                   pl.BlockSpec((1, 1, D_RNN), lambda b, l: (b, 0, 0))],
        out_shape=[jax.ShapeDtypeStruct((B, S, D_RNN), BF16),
                   jax.ShapeDtypeStruct((B, 1, D_RNN), F32)],
        scratch_shapes=[pltpu.VMEM((tl + 8, D_RNN), F32), pltpu.VMEM((1, D_RNN), F32)],
        compiler_params=_cp(("parallel", "arbitrary")),
        name="conv_rglru",
    )(z3, z3, cbuf8, h0, cw8, cb, wr, br, wi, bi, lam)


def _sb_kernel(q_ref, k_ref, v_ref, kn_ref, vn_ref, o_ref, acc_ref, c_ref, *, tq, q_off, n_new):
    tk = KEY_BLOCK
    rs = min(tq, tk)
    ns = tq // rs
    qi = pl.program_id(2)
    q = (q_ref[0].astype(F32) * (SB_HEAD_DIM ** -0.5)).astype(BF16)
    qpos = q_off + qi * tq + lax.broadcasted_iota(jnp.int32, (tq, tk), 0)
    lane = lax.broadcasted_iota(jnp.int32, (rs, tk), 1)
    jj = lax.broadcasted_iota(jnp.int32, (tk, tk), 0)
    ll = lax.broadcasted_iota(jnp.int32, (tk, tk), 1)
    suffix = jnp.where(jj >= ll, 1.0, 0.0).astype(BF16)
    acc_ref[...] = jnp.zeros_like(acc_ref)
    c_ref[...] = jnp.zeros_like(c_ref)
    diag = [(q_off + qi * tq + (i + 1) * rs - 2) // tk for i in range(ns)]
    never = jnp.int32(2 ** 30)
    group = lax.broadcasted_iota(jnp.int32, (tq, 1), 0) // rs
    diag_row = jnp.zeros((tq, 1), jnp.int32)
    for i in range(ns):
        diag_row = jnp.where(group == i, diag[i], diag_row)

    def cond(carry):
        s, cmin = carry
        return (s <= diag[ns - 1]) & (cmin < SB_DEAD_MASS)

    head = pl.program_id(1)

    def head_rows(ref, start, n):
        return ref[0, pl.ds(SB_HEADS * start + head, n, stride=SB_HEADS), :]

    def step(s, ks, vs, kpos):
        zs = [_dot_nt(q[i * rs:(i + 1) * rs, :], ks[i]) for i in range(ns)]
        z = jnp.concatenate(zs, axis=0) if ns > 1 else zs[0]
        earlier = (jnp.concatenate(kpos, axis=0) if ns > 1 else kpos[0]) < qpos
        u = jnp.where(earlier, _softplus(z), 0.0)
        u_hi = u.astype(BF16)
        u_lo = (u - u_hi.astype(F32)).astype(BF16)
        suf = _dot(u_hi, suffix) + _dot(u_lo, suffix)
        c = c_ref[...]
        w = jnp.where(earlier, jnp.exp(z - suf - c), 0.0).astype(BF16)
        pv = [_dot(w[i * rs:(i + 1) * rs, :], vs[i]) for i in range(ns)]
        acc_ref[...] += jnp.concatenate(pv, axis=0) if ns > 1 else pv[0]
        c_new = c + suf[:, 0:1]
        c_ref[...] = c_new
        return s + 1, jnp.min(jnp.where(diag_row > s, c_new, jnp.inf))

    def body(carry):
        s, _ = carry
        ks, vs, kpos = [], [], []
        for i in range(ns):
            kb = diag[i] - s
            start = pl.multiple_of(jnp.maximum(kb, 0) * tk, tk)
            ks.append(head_rows(k_ref, start, tk).astype(BF16))
            vs.append(head_rows(v_ref, start, tk).astype(BF16))
            kpos.append(jnp.where(kb >= 0, start, never) + lane)
        return step(s, ks, vs, kpos)

    first = (jnp.int32(0), jnp.float32(0.0))
    if n_new:
        pad = jnp.zeros((tk - n_new, SB_HEAD_DIM), BF16)
        first = step(jnp.int32(0), [jnp.concatenate([head_rows(kn_ref, 0, n_new).astype(BF16), pad], axis=0)],
                     [jnp.concatenate([head_rows(vn_ref, 0, n_new).astype(BF16), pad], axis=0)],
                     [diag[0] * tk + lane])
    lax.while_loop(cond, body, first)
    o_ref[0] = acc_ref[...].astype(o_ref.dtype)


def _sb_attention(q_arr, k_arr, v_arr, kn_arr, vn_arr, q_cb, layer, tq, q_off, n_new):
    B, Tq, _ = q_arr.shape
    Tk = k_arr.shape[1] // SB_HEADS
    nrows = SB_HEADS * (n_new if n_new else 8)
    assert n_new == 0 or (Tq == tq == n_new <= KEY_BLOCK and q_off % KEY_BLOCK == 0 and q_off == Tk)
    return pl.pallas_call(
        functools.partial(_sb_kernel, tq=tq, q_off=q_off, n_new=n_new),
        grid=(B, SB_HEADS, Tq // tq),
        in_specs=[pl.BlockSpec((1, tq, SB_HEAD_DIM), lambda b, h, i: (b, i, q_cb + h)),
                  pl.BlockSpec((1, Tk * SB_HEADS, SB_HEAD_DIM), lambda b, h, i: (layer * B + b, 0, 0)),
                  pl.BlockSpec((1, Tk * SB_HEADS, SB_HEAD_DIM), lambda b, h, i: (layer * B + b, 0, 0)),
                  pl.BlockSpec((1, nrows, SB_HEAD_DIM), lambda b, h, i: (layer * B + b, 0, 0)),
                  pl.BlockSpec((1, nrows, SB_HEAD_DIM), lambda b, h, i: (layer * B + b, 0, 0))],
        out_specs=pl.BlockSpec((1, tq, SB_HEAD_DIM), lambda b, h, i: (b, i, h)),
        out_shape=jax.ShapeDtypeStruct((B, Tq, SB_WIDTH), BF16),
        scratch_shapes=[pltpu.VMEM((tq, SB_HEAD_DIM), F32), pltpu.VMEM((tq, 1), F32)],
        compiler_params=_cp(("parallel", "parallel", "arbitrary")),
        name="stick_breaking",
    )(q_arr, k_arr, v_arr, kn_arr, vn_arr)


def _mlstm_kernel(mq_ref, mk_ref, mv_ref, mo_ref, if_ref, c0_ref, n0_ref, m0_ref, g_ref,
                  yc_ref, c_out, n_out, m_out, c_s, n_s, m_s, *, ck):
    tl = mq_ref.shape[1]
    nh = ML_HEADS
    li = pl.program_id(1)

    @pl.when(li == 0)
    def _():
        c_s[...] = c0_ref[0]
        n_s[...] = n0_ref[0]
        m_s[...] = jnp.broadcast_to(m0_ref[0], (nh, 1, LANES))

    t_i = lax.broadcasted_iota(jnp.int32, (nh, ck, ck), 1)
    s_i = lax.broadcasted_iota(jnp.int32, (nh, ck, ck), 2)
    causal = s_i <= t_i
    diag = s_i == t_i
    gain = jnp.stack([g_ref[:, h * ML_V_DIM:(h + 1) * ML_V_DIM] for h in range(nh)])

    def heads(ref, r0, width):
        return jnp.stack([ref[0, pl.ds(r0, ck), h * width:(h + 1) * width] for h in range(nh)]).astype(F32)

    def chunk(ci, carry):
        r0 = pl.multiple_of(ci * ck, ck)
        ifb = if_ref[0, pl.ds(r0, ck), :]
        ig_col = jnp.stack([ifb[:, h:h + 1] for h in range(nh)])
        lf_col = -_softplus(-jnp.stack([ifb[:, nh + h:nh + h + 1] for h in range(nh)]))
        q4 = heads(mq_ref, r0, ML_QK_DIM)
        k4 = heads(mk_ref, r0, ML_QK_DIM) * (ML_QK_DIM ** -0.5)
        v4 = heads(mv_ref, r0, ML_V_DIM)
        o4 = heads(mo_ref, r0, ML_V_DIM)
        ig_row = jnp.sum(jnp.where(diag, ig_col, 0.0), axis=1, keepdims=True)
        lf_row = jnp.sum(jnp.where(diag, lf_col, 0.0), axis=1, keepdims=True)
        b_col = jnp.sum(jnp.where(causal, lf_row, 0.0), axis=2, keepdims=True)
        b_row = jnp.sum(jnp.where(t_i <= s_i, lf_col, 0.0), axis=1, keepdims=True)
        m_prev = m_s[:, :, 0:1]
        dmat = jnp.where(causal, b_col - b_row + ig_row, -jnp.inf)
        inter = b_col + m_prev
        m_t = jnp.maximum(inter, jnp.max(dmat, axis=2, keepdims=True))
        s_inter = jnp.exp(inter - m_t)
        qb = q4.astype(BF16)
        kb = k4.astype(BF16)
        vb = v4.astype(BF16)
        wqk = jnp.exp(dmat - m_t) * jnp.stack([_dot_nt(qb[h], kb[h]) for h in range(nh)])
        c_prev = c_s[...]
        n_prev = n_s[...]
        cb = c_prev.astype(BF16)
        wb = wqk.astype(BF16)
        num = (s_inter * jnp.stack([_dot(qb[h], cb[h]) for h in range(nh)])
               + jnp.stack([_dot(wb[h], vb[h]) for h in range(nh)]))
        den = (s_inter * jnp.sum(q4 * n_prev, axis=2, keepdims=True)
               + jnp.sum(wqk, axis=2, keepdims=True))
        hh = num / jnp.maximum(jnp.abs(den), jnp.exp(-m_t))
        b_end = b_col[:, ck - 1:ck, :]
        g_col = b_end - b_col + ig_col
        m_new = jnp.maximum(b_end + m_prev, jnp.max(g_col, axis=1, keepdims=True))
        s_old = jnp.exp(b_end + m_prev - m_new)
        kw = k4 * jnp.exp(g_col - m_new)
        kwb = kw.astype(BF16)
        c_s[...] = s_old * c_prev + jnp.stack([_dot_tn(kwb[h], vb[h]) for h in range(nh)])
        n_s[...] = s_old * n_prev + jnp.sum(kw, axis=1, keepdims=True)
        m_s[...] = jnp.broadcast_to(m_new, (nh, 1, LANES))
        mu = jnp.mean(hh, axis=2, keepdims=True)
        dlt = hh - mu
        var = jnp.mean(dlt * dlt, axis=2, keepdims=True)
        out = (dlt * lax.rsqrt(var + LN_EPS) * gain * _sigmoid(o4)).astype(yc_ref.dtype)
        for h in range(nh):
            yc_ref[0, pl.ds(r0, ck), h * ML_V_DIM:(h + 1) * ML_V_DIM] = out[h]
        return carry

    lax.fori_loop(0, tl // ck, chunk, 0)
    c_out[0] = c_s[...]
    n_out[0] = n_s[...]
    m_out[0] = m_s[:, :, 0:1]


def _mlstm(z3, if3, c0, n0, m0p, g, tl, ck):
    B, S, _ = z3.shape
    return pl.pallas_call(
        functools.partial(_mlstm_kernel, ck=ck),
        grid=(B, S // tl),
        in_specs=[pl.BlockSpec((1, tl, ML_QK_WIDTH), lambda b, l: (b, l, OFF_MQ // ML_QK_WIDTH)),
                  pl.BlockSpec((1, tl, ML_QK_WIDTH), lambda b, l: (b, l, OFF_MK // ML_QK_WIDTH)),
                  pl.BlockSpec((1, tl, ML_V_WIDTH), lambda b, l: (b, l, OFF_MV // ML_V_WIDTH)),
                  pl.BlockSpec((1, tl, ML_V_WIDTH), lambda b, l: (b, l, OFF_MO // ML_V_WIDTH)),
                  pl.BlockSpec((1, tl, LANES), lambda b, l: (b, l, 0)),
                  pl.BlockSpec((1, ML_HEADS, ML_QK_DIM, ML_V_DIM), lambda b, l: (b, 0, 0, 0)),
                  pl.BlockSpec((1, ML_HEADS, 1, ML_QK_DIM), lambda b, l: (b, 0, 0, 0)),
                  pl.BlockSpec((1, ML_HEADS, 1, 1), lambda b, l: (b, 0, 0, 0)),
                  pl.BlockSpec((1, ML_V_WIDTH), lambda b, l: (0, 0))],
        out_specs=[pl.BlockSpec((1, tl, ML_V_WIDTH), lambda b, l: (b, l, 0)),
                   pl.BlockSpec((1, ML_HEADS, ML_QK_DIM, ML_V_DIM), lambda b, l: (b, 0, 0, 0)),
                   pl.BlockSpec((1, ML_HEADS, 1, ML_QK_DIM), lambda b, l: (b, 0, 0, 0)),
                   pl.BlockSpec((1, ML_HEADS, 1, 1), lambda b, l: (b, 0, 0, 0))],
        out_shape=[jax.ShapeDtypeStruct((B, S, ML_V_WIDTH), BF16),
                   jax.ShapeDtypeStruct((B, ML_HEADS, ML_QK_DIM, ML_V_DIM), F32),
                   jax.ShapeDtypeStruct((B, ML_HEADS, 1, ML_QK_DIM), F32),
                   jax.ShapeDtypeStruct((B, ML_HEADS, 1, 1), F32)],
        scratch_shapes=[pltpu.VMEM((ML_HEADS, ML_QK_DIM, ML_V_DIM), F32),
                        pltpu.VMEM((ML_HEADS, 1, ML_QK_DIM), F32),
                        pltpu.VMEM((ML_HEADS, 1, LANES), F32)],
        compiler_params=_cp(("parallel", "arbitrary")),
        name="mlstm",
    )(z3, z3, z3, z3, if3, c0, n0, m0p, g)


def _merge_kernel(ya_ref, yb_ref, yc_ref, g0_ref, g1_ref, g2_ref, x_ref, wpa_ref, wpb_ref, wpc_ref,
                  wo_ref, lg_ref, lb_ref, rw_ref, rb_ref, tri_ref,
                  o_ref, topi_ref, gate_ref, rank_ref, cnt_ref, run_ref, xprev_ref, *, alpha):
    i = pl.program_id(0)

    @pl.when(i == 0)
    def _():
        run_ref[...] = jnp.zeros_like(run_ref)
        xprev_ref[...] = jnp.zeros_like(xprev_ref)

    _route(xprev_ref[...], jnp.where(i > 0, 1.0, 0.0), rw_ref, rb_ref, tri_ref, topi_ref, gate_ref,
           rank_ref, cnt_ref, run_ref)
    mixed = _sigmoid(g0_ref[...].astype(F32)) * _dot(ya_ref[...], wpa_ref[...])
    mixed = mixed + _sigmoid(g1_ref[...].astype(F32)) * _dot(yb_ref[...], wpb_ref[...])
    mixed = mixed + _sigmoid(g2_ref[...].astype(F32)) * _dot(yc_ref[...], wpc_ref[...])
    r = alpha * x_ref[...] + _dot(mixed.astype(BF16), wo_ref[...])
    x1 = _layer_norm(r, lg_ref[...], lb_ref[...])
    o_ref[...] = x1
    xprev_ref[...] = x1


def _merge(ya, yb, yc, z, x, wpa, wpb, wpc, wo, lg, lb, rw, rb, tri, tm, alpha):
    T = x.shape[0]
    nt = T // tm
    full = lambda shape: pl.BlockSpec(shape, lambda i: (0,) * len(shape))
    tile = lambda width, col=0: pl.BlockSpec((tm, width), lambda i: (jnp.minimum(i, nt - 1), col))
    routed = pl.BlockSpec((tm, LANES), lambda i: (jnp.maximum(i - 1, 0), 0))
    mgb = OFF_MG // D_MODEL
    return pl.pallas_call(
        functools.partial(_merge_kernel, alpha=alpha),
        grid=(nt + 1,),
        in_specs=[tile(D_RNN), tile(SB_WIDTH), tile(ML_V_WIDTH),
                  tile(D_MODEL, mgb), tile(D_MODEL, mgb + 1), tile(D_MODEL, mgb + 2), tile(D_MODEL),
                  full((D_RNN, D_MODEL)), full((SB_WIDTH, D_MODEL)), full((ML_V_WIDTH, D_MODEL)),
                  full((D_MODEL, D_MODEL)), full((1, D_MODEL)), full((1, D_MODEL)),
                  full((D_MODEL, LANES)), full((1, LANES)), full((tm, tm))],
        out_specs=[tile(D_MODEL), routed, routed, routed, full((1, LANES))],
        out_shape=[jax.ShapeDtypeStruct((T, D_MODEL), F32),
                   jax.ShapeDtypeStruct((T, LANES), jnp.int32),
                   jax.ShapeDtypeStruct((T, LANES), F32),
                   jax.ShapeDtypeStruct((T, LANES), jnp.int32),
                   jax.ShapeDtypeStruct((1, LANES), jnp.int32)],
        scratch_shapes=[pltpu.VMEM((1, LANES), F32), pltpu.VMEM((tm, D_MODEL), F32)],
        compiler_params=_cp(("arbitrary",)),
        name="merge_ln1_router",
    )(ya, yb, yc, z, z, z, x, wpa, wpb, wpc, wo, lg, lb, rw, rb, tri)


def _route(x, live, rw_ref, rb_ref, tri_ref, topi_ref, gate_ref, rank_ref, cnt_ref, run_ref):
    tm = x.shape[0]
    l = _dot(x.astype(BF16), rw_ref[...]) + rb_ref[...]
    lane = lax.broadcasted_iota(jnp.int32, (tm, LANES), 1)
    lane_f = lane.astype(F32)
    vals, idxs, hots = [], [], []
    for _ in range(TOP_K):
        m = jnp.max(l, axis=1, keepdims=True)
        idx = jnp.min(jnp.where(l == m, lane_f, float(LANES)), axis=1, keepdims=True)
        hot = lane_f == idx
        vals.append(m)
        idxs.append(idx)
        hots.append(hot)
        l = jnp.where(hot, -jnp.inf, l)
    ex = [jnp.exp(v - vals[0]) for v in vals]
    den = ex[0] + ex[1] + ex[2] + ex[3]
    onehot = jnp.zeros((tm, LANES), F32)
    for hot in hots:
        onehot = onehot + jnp.where(hot, 1.0, 0.0)
    before = _dot(tri_ref[...], onehot.astype(BF16)) + run_ref[...]
    topi = jnp.zeros((tm, LANES), F32)
    gate = jnp.zeros((tm, LANES), F32)
    rank = jnp.zeros((tm, LANES), F32)
    for k in range(TOP_K):
        rk = jnp.sum(jnp.where(hots[k], before, 0.0), axis=1, keepdims=True)
        topi = jnp.where(lane == k, idxs[k], topi)
        gate = jnp.where(lane == k, ex[k] / den, gate)
        rank = jnp.where(lane == k, rk, rank)
    run_new = run_ref[...] + live * jnp.sum(onehot, axis=0, keepdims=True)
    run_ref[...] = run_new
    topi_ref[...] = topi.astype(jnp.int32)
    gate_ref[...] = gate
    rank_ref[...] = rank.astype(jnp.int32)
    cnt_ref[...] = run_new.astype(jnp.int32)


ROW_CHUNKS = D_MODEL // 2 // LANES
U32 = jnp.uint32


def _tile_copy(src_ref, src_row, dst_ref, dst_row, sem):
    src = src_ref.at[pl.ds(pl.multiple_of(src_row * ROW_CHUNKS, ROW_CHUNKS), ROW_CHUNKS)]
    dst = dst_ref.at[pl.ds(pl.multiple_of(dst_row * ROW_CHUNKS, ROW_CHUNKS), ROW_CHUNKS)]
    return pltpu.make_async_copy(src, dst, sem)


def _interleave_store(ref, row0, value):
    n = value.shape[0]
    bits = pltpu.bitcast(value.astype(BF16).astype(F32), U32)
    half = D_MODEL // 2
    packed = lax.shift_right_logical(bits[:, :half], U32(16)) | bits[:, half:]
    for c in range(ROW_CHUNKS):
        ref[pl.ds(row0 * ROW_CHUNKS + c, n, stride=ROW_CHUNKS), :] = packed[:, c * LANES:(c + 1) * LANES]


def _interleave_load(ref, row0, n):
    packed = jnp.concatenate(
        [ref[pl.ds(row0 * ROW_CHUNKS + c, n, stride=ROW_CHUNKS), :] for c in range(ROW_CHUNKS)], axis=1)
    lo = pltpu.bitcast(lax.shift_left(packed, U32(16)), F32)
    hi = pltpu.bitcast(packed & U32(0xFFFF0000), F32)
    return jnp.concatenate([lo, hi], axis=1)


def _dispatch_kernel(padlo_ref, padn_ref, dest_ref, x_ref, xs_hbm, xi, ztile, sem, *, ts):
    @pl.when(pl.program_id(0) == 0)
    def _():
        ztile[...] = jnp.zeros_like(ztile)

        def fill(e, c):
            lax.fori_loop(0, padn_ref[e],
                          lambda r, c2: (_tile_copy(ztile, 0, xs_hbm, padlo_ref[e] + r, sem).start(), c2)[1], 0)
            return c

        def fill_done(e, c):
            lax.fori_loop(0, padn_ref[e],
                          lambda r, c2: (_tile_copy(ztile, 0, xs_hbm, 0, sem).wait(), c2)[1], 0)
            return c

        lax.fori_loop(0, N_EXPERTS, fill, 0)
        lax.fori_loop(0, N_EXPERTS, fill_done, 0)

    _interleave_store(xi, 0, x_ref[...])

    def issue(t, c):
        for k in range(TOP_K):
            _tile_copy(xi, t, xs_hbm, dest_ref[TOP_K * t + k], sem).start(priority=k % 2)
        return c

    lax.fori_loop(0, ts, issue, 0)

    def drain(t, c):
        for k in range(TOP_K):
            _tile_copy(xi, 0, xs_hbm, 0, sem).wait()
        return c

    lax.fori_loop(0, ts, drain, 0)


def _dispatch(pad_lo, pad_n, dest_flat, x, rows, ts):
    T = x.shape[0]
    grid_spec = pltpu.PrefetchScalarGridSpec(
        num_scalar_prefetch=2,
        grid=(T // ts,),
        in_specs=[pl.BlockSpec((TOP_K * ts,), lambda i, lo, n: (i,), memory_space=pltpu.SMEM),
                  pl.BlockSpec((ts, D_MODEL), lambda i, lo, n: (i, 0))],
        out_specs=pl.BlockSpec(memory_space=pl.ANY),
        scratch_shapes=[pltpu.VMEM((ts * ROW_CHUNKS, LANES), U32), pltpu.VMEM((8, LANES), U32),
                        pltpu.SemaphoreType.DMA(())],
    )
    return pl.pallas_call(
        functools.partial(_dispatch_kernel, ts=ts),
        grid_spec=grid_spec,
        out_shape=jax.ShapeDtypeStruct((rows * ROW_CHUNKS, LANES), U32),
        compiler_params=_cp(("arbitrary",)),
        name="moe_dispatch",
    )(pad_lo, pad_n, dest_flat, x)


def _experts_kernel(be_ref, nu_ref, xs_ref, w1_ref, b1_ref, w2_ref, b2_ref, y_ref, w1b_ref, w2b_ref, *, m):
    j = pl.program_id(0)

    @pl.when((j == 0) | (be_ref[j] != be_ref[jnp.maximum(j - 1, 0)]))
    def _():
        w1b_ref[...] = w1_ref[0, 0].astype(BF16)
        w2b_ref[...] = w2_ref[0, 0].astype(BF16)

    @pl.when(j < nu_ref[0])
    def _():
        gu = _dot(_interleave_load(xs_ref, 0, m).astype(BF16), w1b_ref[...]) + b1_ref[0]
        g_ = jnp.minimum(gu[:, :D_FF], SWIGLU_LIMIT)
        up = jnp.clip(gu[:, D_FF:], -SWIGLU_LIMIT, SWIGLU_LIMIT)
        act = (up + 1.0) * g_ * _sigmoid(SWIGLU_ALPHA * g_)
        _interleave_store(y_ref, 0, _dot(act.astype(BF16), w2b_ref[...]) + b2_ref[0])

    @pl.when(pl.program_id(0) >= nu_ref[0])
    def _():
        y_ref[...] = jnp.zeros_like(y_ref)


def _experts(block_e, n_used, xs, w1, b1, w2, b2, m, layer):
    nblk = xs.shape[0] // (m * ROW_CHUNKS)
    grid_spec = pltpu.PrefetchScalarGridSpec(
        num_scalar_prefetch=2,
        grid=(nblk,),
        in_specs=[pl.BlockSpec((m * ROW_CHUNKS, LANES), lambda j, be, nu: (jnp.minimum(j, nu[0] - 1), 0)),
                  pl.BlockSpec((1, 1, D_MODEL, 2 * D_FF), lambda j, be, nu: (layer, be[j], 0, 0)),
                  pl.BlockSpec((1, 1, 2 * D_FF), lambda j, be, nu: (be[j], 0, 0)),
                  pl.BlockSpec((1, 1, D_FF, D_MODEL), lambda j, be, nu: (layer, be[j], 0, 0)),
                  pl.BlockSpec((1, 1, D_MODEL), lambda j, be, nu: (be[j], 0, 0))],
        out_specs=pl.BlockSpec((m * ROW_CHUNKS, LANES), lambda j, be, nu: (j, 0)),
        scratch_shapes=[pltpu.VMEM((D_MODEL, 2 * D_FF), BF16), pltpu.VMEM((D_FF, D_MODEL), BF16)],
    )
    return pl.pallas_call(
        functools.partial(_experts_kernel, m=m),
        grid_spec=grid_spec,
        out_shape=jax.ShapeDtypeStruct(xs.shape, U32),
        compiler_params=_cp(("arbitrary",)),
        name="moe_experts",
    )(block_e, n_used, xs, w1, b1, w2, b2)


def _combine_kernel(dest_a, dest_b, dest_next, x_ref, p_ref, gate_ref, y_hbm, wp_ref, wg_ref, bg_ref,
                    lg_ref, lb_ref, o_ref, ybuf_a, ybuf_b, sem_a, sem_b, *, alpha, tm, halves):
    i = pl.program_id(0)
    bufs = [(ybuf_a, sem_a), (ybuf_b, sem_b)][:halves]
    dests = [dest_a, dest_b][:halves]

    def issue(dest_ref, buf, sem):
        def one(t, c):
            for k in range(TOP_K):
                _tile_copy(y_hbm, dest_ref[TOP_K * t + k], buf, k * tm + t, sem).start(priority=k % 2)
            return c

        lax.fori_loop(0, tm, one, 0)

    def drain(buf, sem):
        def one(t, c):
            for k in range(TOP_K):
                _tile_copy(y_hbm, 0, buf, 0, sem).wait()
            return c

        lax.fori_loop(0, tm, one, 0)

    @pl.when(i == 0)
    def _():
        issue(dests[0], *bufs[0])

    for h in range(halves):
        if h + 1 < halves:
            issue(dests[h + 1], *bufs[h + 1])
        rows = pl.ds(h * tm, tm)
        x = x_ref[rows, :]
        ple = (_sigmoid(_dot(x.astype(BF16), wg_ref[...]) + bg_ref[...])
               * _dot(p_ref[rows, :].astype(BF16), wp_ref[...]))
        r = alpha * x + ple
        drain(*bufs[h])
        for k in range(TOP_K):
            r = r + gate_ref[rows, k:k + 1] * _interleave_load(bufs[h][0], k * tm, tm)
        o_ref[rows, :] = _layer_norm(r, lg_ref[...], lb_ref[...])
        if h == 0:
            @pl.when(i + 1 < pl.num_programs(0))
            def _():
                issue(dest_next, *bufs[0])


def _combine(dest_flat, x, p, gate, y, wp, wg, bg, lg, lb, tm, alpha):
    T = x.shape[0]
    ntiles = T // tm
    halves = 2 if ntiles % 2 == 0 else 1
    tb = halves * tm
    full = lambda shape: pl.BlockSpec(shape, lambda i: (0,) * len(shape))
    dest_spec = lambda fn: pl.BlockSpec((TOP_K * tm,), fn, memory_space=pltpu.SMEM)
    return pl.pallas_call(
        functools.partial(_combine_kernel, alpha=alpha, tm=tm, halves=halves),
        grid=(ntiles // halves,),
        in_specs=[dest_spec(lambda i: (halves * i,)),
                  dest_spec(lambda i: (halves * i + halves - 1,)),
                  dest_spec(lambda i: (jnp.minimum(halves * (i + 1), ntiles - 1),)),
                  pl.BlockSpec((tb, D_MODEL), lambda i: (i, 0)),
                  pl.BlockSpec((tb, D_PLE), lambda i: (i, 0)),
                  pl.BlockSpec((tb, LANES), lambda i: (i, 0)),
                  pl.BlockSpec(memory_space=pl.ANY),
                  full((D_PLE, D_MODEL)), full((D_MODEL, D_MODEL)), full((1, D_MODEL)),
                  full((1, D_MODEL)), full((1, D_MODEL))],
        out_specs=pl.BlockSpec((tb, D_MODEL), lambda i: (i, 0)),
        out_shape=jax.ShapeDtypeStruct((T, D_MODEL), F32),
        scratch_shapes=[pltpu.VMEM((TOP_K * tm * ROW_CHUNKS, LANES), U32),
                        pltpu.VMEM((TOP_K * tm * ROW_CHUNKS, LANES), U32),
                        pltpu.SemaphoreType.DMA(()), pltpu.SemaphoreType.DMA(())],
        compiler_params=_cp(("arbitrary",)),
        name="moe_combine_ln2",
    )(dest_flat, dest_flat, dest_flat, x, p, gate, y, wp, wg, bg, lg, lb)


def _pick(n, pref):
    t = min(n, pref)
    while n % t:
        t //= 2
    return t


def _layer(x, p, lw, conv_buf, lru_h, ml_c, ml_n, ml_m, k_past, v_past, k_all, v_all, prompt, alpha, layer):
    B, L, _ = x.shape
    T = B * L
    xf = x.reshape(T, D_MODEL)

    z = _inproj(xf, lw["w_in"], lw["b_in"], _pick(T, 2048), 1024)
    k_all, v_all, ifg = _kvif(xf, lw["w_kvif"], lw["b_kvif"], k_all, v_all, layer, _pick(T, 1024))
    z3 = z.reshape(B, L, N_MAIN)
    k_rows = k_all.reshape(-1, L * SB_HEADS, SB_HEAD_DIM)
    v_rows = v_all.reshape(-1, L * SB_HEADS, SB_HEAD_DIM)
    conv_new = z3[:, L - (CONV_W - 1):, OFF_AX:OFF_AX + D_RNN].astype(F32)

    cbuf8 = jnp.concatenate([jnp.zeros((B, 8 - (CONV_W - 1), D_RNN), F32), conv_buf.astype(F32)], axis=1)
    ya, lru_new = _lru(z3, cbuf8, lru_h.reshape(B, 1, D_RNN).astype(F32), lw["conv_w8"], lw["conv_b"],
                       lw["wr"], lw["br"], lw["wi"], lw["bi"], lw["lam"], _pick(L, 512), prompt)

    if prompt:
        yb = _sb_attention(z3, k_rows, v_rows, k_rows, v_rows, OFF_Q // SB_HEAD_DIM, layer,
                           _pick(L, 2048), 0, 0)
    else:
        yb = _sb_attention(z3, k_past, v_past, k_rows, v_rows, OFF_Q // SB_HEAD_DIM, layer,
                           L, k_past.shape[1] // SB_HEADS, L)

    ck = _pick(L, 128)
    yc, c_new, n_new, m_new = _mlstm(z3, ifg.reshape(B, L, LANES), ml_c.astype(F32),
                                     ml_n.astype(F32).reshape(B, ML_HEADS, 1, ML_QK_DIM),
                                     ml_m.astype(F32).reshape(B, ML_HEADS, 1, 1),
                                     lw["ml_g"], _pick(L, 512), ck)
    n_new = n_new.reshape(B, ML_HEADS, ML_QK_DIM)
    m_new = m_new.reshape(B, ML_HEADS)

    tr = _pick(T, 512)
    tri = jnp.tril(jnp.ones((tr, tr), BF16), -1)
    x1, topi, gate, rank, cnt = _merge(
        ya.reshape(T, D_RNN), yb.reshape(T, SB_WIDTH), yc.reshape(T, ML_V_WIDTH), z, xf,
        lw["w_pa"], lw["w_pb"], lw["w_pc"], lw["w_out"], lw["ln1_g"], lw["ln1_b"],
        lw["router_w"], lw["router_b"], tri, tr, alpha)

    TK = T * TOP_K
    m_rows = max(16, min(512, TK // N_EXPERTS))
    nb = TK // m_rows + N_EXPERTS
    counts = cnt[0, :N_EXPERTS]
    padded = (counts + (m_rows - 1)) // m_rows * m_rows
    pend = jnp.cumsum(padded)
    off = (pend - padded).astype(jnp.int32)
    blk_start = jnp.arange(nb, dtype=jnp.int32) * m_rows
    block_e = jnp.minimum(jnp.sum((pend[None, :] <= blk_start[:, None]).astype(jnp.int32), axis=1),
                          N_EXPERTS - 1).astype(jnp.int32)
    n_used = (pend[-1:] // m_rows).astype(jnp.int32)
    experts = jnp.arange(N_EXPERTS, dtype=jnp.int32)
    top4 = topi[:, :TOP_K]
    dest = rank[:, :TOP_K] + jnp.sum(jnp.where(top4[:, :, None] == experts, off, 0), axis=-1)
    dest_flat = dest.reshape(TK).astype(jnp.int32)
    xs = _dispatch(off + counts, (padded - counts).astype(jnp.int32), dest_flat, x1, nb * m_rows,
                   _pick(T, 256))
    y = _experts(block_e, n_used, xs, lw["exp_w1"], lw["exp_b1"], lw["exp_w2"], lw["exp_b2"], m_rows, layer)
    x2 = _combine(dest_flat, x1, p.reshape(T, D_PLE), gate, y, lw["ple_w"], lw["ple_gate_w"],
                  lw["ple_gate_b"], lw["ln2_g"], lw["ln2_b"], _pick(T, 256), alpha)

    return (x2.reshape(B, L, D_MODEL), k_all, v_all,
            (conv_new, lru_new.reshape(B, D_RNN), c_new, n_new, m_new))


def _block_diag_groups(w):
    per = LRU_GROUP // LRU_BLOCK
    w4 = w.reshape(D_RNN // LRU_GROUP, per, LRU_BLOCK, LRU_BLOCK)
    eye = jnp.eye(per, dtype=w.dtype)
    return jnp.einsum("gacd,ab->gacbd", w4, eye).reshape(D_RNN // LRU_GROUP, LRU_GROUP, LRU_GROUP)


def _split_in(w):
    main = jnp.concatenate([w[..., :ORIG_K], w[..., ORIG_MQ:ORIG_IF], w[..., ORIG_MG:]], axis=-1)
    pad = jnp.zeros(w.shape[:-1] + (LANES - 2 * ML_HEADS,), w.dtype)
    kvif = jnp.concatenate([w[..., ORIG_K:ORIG_MQ], w[..., ORIG_IF:ORIG_MG], pad], axis=-1)
    return main, kvif


def kernel(x_prompt, x_sample, cache_sb_k, cache_sb_v, state_conv, state_lru, state_mlstm_c, state_mlstm_n, state_mlstm_m, p_prompt, p_sample, w_in, b_in, conv_w, conv_b, lru_wr, lru_br, lru_wi, lru_bi, lru_lambda, ml_norm_g, w_pa, w_pb, w_pc, w_out, ln1_g, ln1_b, router_w, router_b, exp_w1, exp_b1, exp_w2, exp_b2, ple_w, ple_gate_w, ple_gate_b, ln2_g, ln2_b):
    depth = w_in.shape[0]
    alpha = (2 * depth) ** 0.25
    B = x_prompt.shape[0]
    y_prompt, y_sample = x_prompt, x_sample
    S = x_prompt.shape[1]
    Bs, Ls = x_sample.shape[:2]
    P = cache_sb_k.shape[2]
    k_cache = cache_sb_k.astype(F32).reshape(depth * Bs, P * SB_HEADS, SB_HEAD_DIM)
    v_cache = cache_sb_v.astype(F32).reshape(depth * Bs, P * SB_HEADS, SB_HEAD_DIM)
    kp, vp = (_unwritten((depth * B * S * SB_HEADS, SB_HEAD_DIM)) for _ in range(2))
    ks, vs = (_unwritten((depth * Bs * Ls * SB_HEADS, SB_HEAD_DIM)) for _ in range(2))
    st_p, st_s = [], []
    for i in range(depth):
        row = lambda a: a[i].reshape(1, -1).astype(F32)
        w_main, w_kvif = _split_in(w_in[i])
        b_main, b_kvif = _split_in(b_in[i])
        lw = dict(
            w_in=w_main.astype(BF16), b_in=b_main.reshape(1, -1).astype(F32),
            w_kvif=w_kvif.astype(BF16), b_kvif=b_kvif.reshape(1, -1).astype(F32),
            conv_w8=jnp.pad(conv_w[i].astype(F32), ((0, 8 - CONV_W), (0, 0))), conv_b=row(conv_b),
            wr=_block_diag_groups(lru_wr[i]).astype(BF16), br=row(lru_br),
            wi=_block_diag_groups(lru_wi[i]).astype(BF16), bi=row(lru_bi), lam=row(lru_lambda),
            ml_g=row(ml_norm_g),
            w_pa=w_pa[i].astype(BF16), w_pb=w_pb[i].astype(BF16), w_pc=w_pc[i].astype(BF16),
            w_out=w_out[i].astype(BF16), ln1_g=row(ln1_g), ln1_b=row(ln1_b),
            router_w=jnp.pad(router_w[i], ((0, 0), (0, LANES - N_EXPERTS))).astype(BF16),
            router_b=jnp.pad(router_b[i].astype(F32), (0, LANES - N_EXPERTS),
                             constant_values=-1e30).reshape(1, LANES),
            exp_w1=exp_w1.astype(F32), exp_b1=exp_b1[i].reshape(N_EXPERTS, 1, 2 * D_FF).astype(F32),
            exp_w2=exp_w2.astype(F32), exp_b2=exp_b2[i].reshape(N_EXPERTS, 1, D_MODEL).astype(F32),
            ple_w=ple_w[i].astype(BF16), ple_gate_w=ple_gate_w[i].astype(BF16), ple_gate_b=row(ple_gate_b),
            ln2_g=row(ln2_g), ln2_b=row(ln2_b),
        )
        y_prompt, kp, vp, sp = _layer(y_prompt, p_prompt[i], lw,
                                      jnp.zeros((B, CONV_W - 1, D_RNN), F32), jnp.zeros((B, D_RNN), F32),
                                      jnp.zeros((B, ML_HEADS, ML_QK_DIM, ML_V_DIM), F32),
                                      jnp.zeros((B, ML_HEADS, ML_QK_DIM), F32), jnp.zeros((B, ML_HEADS), F32),
                                      None, None, kp, vp, True, alpha, i)
        y_sample, ks, vs, ss = _layer(y_sample, p_sample[i], lw, state_conv[i], state_lru[i], state_mlstm_c[i],
                                      state_mlstm_n[i], state_mlstm_m[i], k_cache, v_cache, ks, vs,
                                      False, alpha, i)
        st_p.append(sp)
        st_s.append(ss)
    conv_p, lru_p, c_p, n_p, m_p = [jnp.stack(s) for s in zip(*st_p)]
    conv_s, lru_s, c_s, n_s, m_s = [jnp.stack(s) for s in zip(*st_s)]
    kv_p = (depth, B, S, SB_HEADS, SB_HEAD_DIM)
    kv_s = (depth, Bs, Ls, SB_HEADS, SB_HEAD_DIM)
    return (y_prompt, y_sample, kp.reshape(kv_p), vp.reshape(kv_p), conv_p, lru_p, c_p, n_p, m_p,
            ks.reshape(kv_s), vs.reshape(kv_s), conv_s, lru_s, c_s, n_s, m_s)
```
